```python
import math
import jax, jax.numpy as jnp
from jax import lax
import numpy as np

D_MODEL = 1024
BATCH = 32
SEQ = 256
DEPTH = 2
DEC_BATCH = 2
DEC_SEQ = 4096
PAST_LEN = 256

GRID_W = 64
N_EVEN = (DEPTH + 1) // 2
N_ODD = DEPTH // 2
MIX_W = D_MODEL
HY_CH = MIX_W // 2
S5_CH = MIX_W - HY_CH
S5_GROUP = 16
S5_GROUPS = S5_CH // S5_GROUP
S5_STATE = 64
HY_EMB = 33
HY_BANDS = (HY_EMB - 1) // 2
HY_ORDER = 64
HY_TARGET = 1e-2
HY_FAST_DECAY = 0.3
HY_SLOW_DECAY = 1.5
NA_HEADS = 16
NA_HEAD_DIM = MIX_W // NA_HEADS
NA_WIN_R = 8
NA_WIN_C = 16
N_EXPERTS = 32
TOP_K = 4
D_FF = D_MODEL
SWIGLU_LIMIT = 7.0
SWIGLU_ALPHA = 1.702
MOE_BLOCK = 256
Q_BLOCK = 128
LN_EPS = 1e-5
DN_ALPHA = (2 * DEPTH) ** 0.25
DN_BETA = (8 * DEPTH) ** -0.25

kernel_name = 'hybrid_hyena_s5_natten_moe_prefix_step'


def layer_norm(x, g, b):
    xf = x.astype(jnp.float32)
    mu = jnp.mean(xf, -1, keepdims=True)
    var = jnp.mean(jnp.square(xf - mu), -1, keepdims=True)
    return ((xf - mu) * lax.rsqrt(var + LN_EPS) * g.astype(jnp.float32) + b.astype(jnp.float32)).astype(x.dtype)


def modulation(cvec, w_mod, b_mod):
    m = jax.nn.silu(cvec) @ w_mod + b_mod
    m = m.reshape(m.shape[:-1] + (6, D_MODEL))
    return [m[..., i, :][..., None, :] for i in range(6)]


def short_conv(u, w, b):
    L = u.shape[1]
    up = jnp.pad(u, ((0, 0), (1, 1), (0, 0)))
    return up[:, :L] * w[0] + up[:, 1:L + 1] * w[1] + up[:, 2:] * w[2] + b


def hyena_filter(L, w1, b1, w2, b2, w3, freq):
    t = jnp.linspace(0.0, 1.0, L, dtype=jnp.float32)[:, None]
    w = 2.0 * math.pi * jnp.arange(L, dtype=jnp.float32)[:, None] / L
    f = jnp.linspace(1e-4, HY_BANDS - 1, HY_BANDS, dtype=jnp.float32)[None, :]
    z = jnp.concatenate([t, jnp.cos(f * w), -jnp.sin(f * w)], -1)
    fq = freq.astype(jnp.float32)
    h = jnp.sin(fq * (z @ w1.astype(jnp.float32) + b1.astype(jnp.float32)))
    h = jnp.sin(fq * (h @ w2.astype(jnp.float32) + b2.astype(jnp.float32)))
    h = h @ w3.astype(jnp.float32)
    max_decay = math.log(HY_TARGET) / HY_FAST_DECAY
    min_decay = math.log(HY_TARGET) / HY_SLOW_DECAY
    deltas = jnp.linspace(min_decay, max_decay, HY_CH, dtype=jnp.float32)
    decay = jnp.exp(-t * jnp.abs(deltas))
    return h[:, :HY_CH] * decay, h[:, HY_CH:] * decay


def bidir_long_conv(v, hf, hb):
    L, C = v.shape[1], v.shape[2]
    h = jnp.concatenate([hf[:1] + hb[:1], hf[1:], jnp.zeros((1, C), jnp.float32), hb[:0:-1]], 0)
    n = 2 * L
    vf = jnp.fft.rfft(v.astype(jnp.float32), n=n, axis=1)
    hfreq = jnp.fft.rfft(h, n=n, axis=0)
    return jnp.fft.irfft(vf * hfreq[None], n=n, axis=1)[:, :L]


def hyena_mixer(u, conv_w, conv_b, f_w1, f_b1, f_w2, f_b2, f_w3, freq, bias):
    u = short_conv(u, conv_w, conv_b)
    x0, x1, v = u[..., :HY_CH], u[..., HY_CH:2 * HY_CH], u[..., 2 * HY_CH:]
    hf, hb = hyena_filter(u.shape[1], f_w1, f_b1, f_w2, f_b2, f_w3, freq)
    v = v * x1
    v = bidir_long_conv(v, hf, hb).astype(u.dtype) + v * bias
    return v * x0


def diag_scan(lam_bar, bu, h0, reverse):
    if reverse:
        bu = jnp.flip(bu, 1)
    bu = bu.at[:, 0].add(lam_bar * h0)
    a = jnp.broadcast_to(lam_bar, bu.shape)

    def combine(e1, e2):
        a1, b1 = e1
        a2, b2 = e2
        return a1 * a2, a2 * b1 + b2

    _, h = lax.associative_scan(combine, (a, bu), axis=1)
    if reverse:
        h = jnp.flip(h, 1)
    return h


def s5_mixer(u, h0, lam_re, lam_im, log_dt, b_re, b_im, c_re, c_im, d, glu_w, glu_b, return_final):
    bsz, L, _ = u.shape
    ug = u.astype(jnp.float32).reshape(bsz, L, S5_GROUPS, S5_GROUP)
    y = d.astype(jnp.float32) * ug
    finals = []
    for r in range(2):
        lam = lax.complex(lam_re[r].astype(jnp.float32), lam_im[r].astype(jnp.float32))
        dt = jnp.exp(log_dt[r].astype(jnp.float32))[:, None]
        lam_bar = jnp.exp(lam * dt)
        bmat = lax.complex(b_re[r].astype(jnp.float32), b_im[r].astype(jnp.float32))
        b_bar = ((lam_bar - 1.0) / lam)[..., None] * bmat
        bu = jnp.einsum('gnc,blgc->blgn', b_bar, ug)
        h = diag_scan(lam_bar, bu, h0[:, r], reverse=(r == 1))
        cmat = lax.complex(c_re[r].astype(jnp.float32), c_im[r].astype(jnp.float32))
        y = y + jnp.einsum('gcn,blgn->blgc', cmat, h).real
        if return_final:
            finals.append(h[:, -1] if r == 0 else h[:, 0])
    y = jax.nn.gelu(y.reshape(bsz, L, S5_CH))
    z = y @ glu_w.astype(jnp.float32) + glu_b.astype(jnp.float32)
    out = (z[..., :S5_CH] * jax.nn.sigmoid(z[..., S5_CH:])).astype(u.dtype)
    if return_final:
        return out, jnp.stack(finals, 1)
    return out, None


def ctx_attention(q, k, v):
    bsz, L, H, hd = q.shape
    scale = hd ** -0.5
    nb = L // Q_BLOCK
    qb = jnp.moveaxis(q.reshape(bsz, nb, Q_BLOCK, H, hd), 1, 0)

    def block(qi):
        s = jnp.einsum('bqhd,bkhd->bhqk', qi, k).astype(jnp.float32) * scale
        p = jax.nn.softmax(s, axis=-1).astype(v.dtype)
        return jnp.einsum('bhqk,bkhd->bqhd', p, v)

    o = lax.map(block, qb)
    return jnp.moveaxis(o, 0, 1).reshape(bsz, L, H * hd)


def na_latent(q, k, v, kc, vc, rpb):
    bsz, L, H, hd = q.shape
    rows = L // GRID_W
    wr = min(NA_WIN_R, rows)
    scale = hd ** -0.5
    q, k, v = (t.reshape(bsz, rows, GRID_W, H, hd) for t in (q, k, v))
    j = np.arange(GRID_W)
    cs = np.clip(j - NA_WIN_C // 2, 0, GRID_W - NA_WIN_C)
    col_idx = cs[:, None] + np.arange(NA_WIN_C)[None, :]
    col_rel = col_idx - j[:, None] + NA_WIN_C - 1
    n_win = wr * NA_WIN_C

    def one_row(r):
        rs = jnp.clip(r - wr // 2, 0, rows - wr)
        row_rel = rs + jnp.arange(wr) - r + NA_WIN_R - 1
        kb = lax.dynamic_slice_in_dim(k, rs, wr, axis=1)[:, :, col_idx]
        vb = lax.dynamic_slice_in_dim(v, rs, wr, axis=1)[:, :, col_idx]
        qr = lax.dynamic_index_in_dim(q, r, axis=1, keepdims=False)
        bias = rpb[:, row_rel[None, :, None], col_rel[:, None, :]].astype(jnp.float32)
        s_win = jnp.einsum('bjhd,bijkhd->bhjik', qr, kb).astype(jnp.float32) * scale + bias
        s_ctx = jnp.einsum('bjhd,bchd->bhjc', qr, kc).astype(jnp.float32) * scale
        logits = jnp.concatenate([s_win.reshape(bsz, H, GRID_W, n_win), s_ctx], -1)
        p = jax.nn.softmax(logits, axis=-1)
        p_win = p[..., :n_win].reshape(bsz, H, GRID_W, wr, NA_WIN_C).astype(v.dtype)
        p_ctx = p[..., n_win:].astype(v.dtype)
        return jnp.einsum('bhjik,bijkhd->bjhd', p_win, vb) + jnp.einsum('bhjc,bchd->bjhd', p_ctx, vc)

    out = lax.map(one_row, jnp.arange(rows))
    return jnp.moveaxis(out, 0, 1).reshape(bsz, L, H * hd)


def moe(x, router_w, router_b, w1, b1, w2, b2):
    shp = x.shape
    xt = x.reshape(-1, D_MODEL)
    T = xt.shape[0]
    logits = (xt @ router_w + router_b).astype(jnp.float32)
    top_v, top_i = lax.top_k(logits, TOP_K)
    gates = jax.nn.softmax(top_v, axis=-1)
    n_assign = T * TOP_K
    flat_e = top_i.reshape(-1)
    flat_tok = jnp.arange(n_assign, dtype=jnp.int32) // TOP_K
    order = jnp.argsort(flat_e)
    sorted_e = flat_e[order]
    counts = jnp.bincount(flat_e, length=N_EXPERTS)
    padded = (counts + MOE_BLOCK - 1) // MOE_BLOCK * MOE_BLOCK
    start = jnp.cumsum(counts) - counts
    pend = jnp.cumsum(padded)
    pstart = pend - padded
    dest = pstart[sorted_e] + jnp.arange(n_assign) - start[sorted_e]
    n_blocks = -(-n_assign // MOE_BLOCK) + N_EXPERTS
    n_slots = n_blocks * MOE_BLOCK
    slot_tok = jnp.full((n_slots,), T, jnp.int32).at[dest].set(flat_tok[order])
    slot_gate = jnp.zeros((n_slots,), jnp.float32).at[dest].set(gates.reshape(-1)[order])
    block_e = jnp.clip(jnp.searchsorted(pend, jnp.arange(n_blocks) * MOE_BLOCK, side='right'), 0, N_EXPERTS - 1)
    x_pad = jnp.concatenate([xt, jnp.zeros((1, D_MODEL), xt.dtype)], 0)
    xin = x_pad[slot_tok].reshape(n_blocks, MOE_BLOCK, D_MODEL)

    def expert_block(args):
        xb, e = args
        h = xb @ w1[e] + b1[e]
        g = jnp.minimum(h[:, :D_FF], SWIGLU_LIMIT)
        u = jnp.clip(h[:, D_FF:], -SWIGLU_LIMIT, SWIGLU_LIMIT)
        a = g * jax.nn.sigmoid(SWIGLU_ALPHA * g) * (u + 1.0)
        return a @ w2[e] + b2[e]

    yb = lax.map(expert_block, (xin, block_e)).reshape(n_slots, D_MODEL)
    y = jnp.zeros((T + 1, D_MODEL), jnp.float32).at[slot_tok].add(yb.astype(jnp.float32) * slot_gate[:, None])[:T]
    return y.astype(x.dtype).reshape(shp)


def setup_inputs(seed: int = 0) -> dict:
    key = jax.random.key(seed)
    keys = iter(jax.random.split(key, 64))

    def nrm(shape, s):
        return jax.random.normal(next(keys), shape, jnp.float32) * s

    D = D_MODEL
    G, N = S5_GROUPS, S5_STATE
    inp = {}
    inp['x_prompt'] = nrm((BATCH, SEQ, D), 1.0)
    inp['x_sample'] = nrm((DEC_BATCH, DEC_SEQ, D), 1.0)
    inp['state_s5'] = nrm((DEC_BATCH, N_EVEN, 2, G, N, 2), 0.5)
    inp['cache_na_k'] = nrm((DEC_BATCH, N_ODD, PAST_LEN, NA_HEADS, NA_HEAD_DIM), 1.0)
    inp['cache_na_v'] = nrm((DEC_BATCH, N_ODD, PAST_LEN, NA_HEADS, NA_HEAD_DIM), 1.0)
    inp['c'] = nrm((DEC_BATCH, D), 1.0)
    inp['c_ctx'] = nrm((D,), 1.0)
    inp['w_mod'] = nrm((DEPTH, D, 6 * D), 0.5 * D ** -0.5)
    inp['b_mod'] = nrm((DEPTH, 6 * D), 0.02)
    inp['ln1_g'] = 1.0 + nrm((DEPTH, D), 0.02)
    inp['ln1_b'] = nrm((DEPTH, D), 0.02)
    inp['ln2_g'] = 1.0 + nrm((DEPTH, D), 0.02)
    inp['ln2_b'] = nrm((DEPTH, D), 0.02)
    inp['ev_w_in'] = nrm((N_EVEN, D, 3 * HY_CH + S5_CH), D ** -0.5)
    inp['ev_w_out'] = nrm((N_EVEN, MIX_W, D), DN_BETA * MIX_W ** -0.5)
    inp['hy_conv_w'] = nrm((N_EVEN, 3, 3 * HY_CH), 3 ** -0.5)
    inp['hy_conv_b'] = nrm((N_EVEN, 3 * HY_CH), 0.02)
    inp['hy_f_w1'] = nrm((N_EVEN, HY_EMB, HY_ORDER), HY_EMB ** -0.5)
    inp['hy_f_b1'] = nrm((N_EVEN, HY_ORDER), 0.02)
    inp['hy_f_w2'] = nrm((N_EVEN, HY_ORDER, HY_ORDER), HY_ORDER ** -0.5)
    inp['hy_f_b2'] = nrm((N_EVEN, HY_ORDER), 0.02)
    inp['hy_f_w3'] = nrm((N_EVEN, HY_ORDER, 2 * HY_CH), 0.1 * HY_ORDER ** -0.5)
    inp['hy_f_freq'] = 1.0 + nrm((N_EVEN, HY_ORDER), 0.02)
    inp['hy_bias'] = nrm((N_EVEN, HY_CH), 0.5)
    inp['s5_lam_re'] = -0.5 + nrm((N_EVEN, 2, G, N), 0.01)
    inp['s5_lam_im'] = math.pi * jnp.arange(N, dtype=jnp.float32) + nrm((N_EVEN, 2, G, N), 0.01)
    inp['s5_log_dt'] = jax.random.uniform(next(keys), (N_EVEN, 2, G), jnp.float32, math.log(1e-3), math.log(1e-1))
    inp['s5_b_re'] = nrm((N_EVEN, 2, G, N, S5_GROUP), (2 * S5_GROUP) ** -0.5)
    inp['s5_b_im'] = nrm((N_EVEN, 2, G, N, S5_GROUP), (2 * S5_GROUP) ** -0.5)
    inp['s5_c_re'] = nrm((N_EVEN, 2, G, S5_GROUP, N), (2 * N) ** -0.5)
    inp['s5_c_im'] = nrm((N_EVEN, 2, G, S5_GROUP, N), (2 * N) ** -0.5)
    inp['s5_d'] = nrm((N_EVEN, G, S5_GROUP), 1.0)
    inp['s5_glu_w'] = nrm((N_EVEN, S5_CH, 2 * S5_CH), S5_CH ** -0.5)
    inp['s5_glu_b'] = nrm((N_EVEN, 2 * S5_CH), 0.02)
    inp['od_w_in'] = nrm((N_ODD, D, 3 * MIX_W), D ** -0.5)
    inp['od_w_out'] = nrm((N_ODD, MIX_W, D), DN_BETA * MIX_W ** -0.5)
    inp['na_rpb'] = nrm((N_ODD, NA_HEADS, 2 * NA_WIN_R - 1, 2 * NA_WIN_C - 1), 0.02)
    inp['router_w'] = nrm((DEPTH, D, N_EXPERTS), D ** -0.5)
    inp['router_b'] = nrm((DEPTH, N_EXPERTS), 0.01)
    inp['moe_w1'] = nrm((DEPTH, N_EXPERTS, D, 2 * D_FF), D ** -0.5)
    inp['moe_b1'] = nrm((DEPTH, N_EXPERTS, 2 * D_FF), 0.02)
    inp['moe_w2'] = nrm((DEPTH, N_EXPERTS, D_FF, D), DN_BETA * D_FF ** -0.5)
    inp['moe_b2'] = nrm((DEPTH, N_EXPERTS, D), 0.02)
    return inp


def reference(x_prompt, x_sample, state_s5, cache_na_k, cache_na_v, c, c_ctx,
              w_mod, b_mod, ln1_g, ln1_b, ln2_g, ln2_b,
              ev_w_in, ev_w_out, hy_conv_w, hy_conv_b, hy_f_w1, hy_f_b1, hy_f_w2, hy_f_b2,
              hy_f_w3, hy_f_freq, hy_bias,
              s5_lam_re, s5_lam_im, s5_log_dt, s5_b_re, s5_b_im, s5_c_re, s5_c_im, s5_d,
              s5_glu_w, s5_glu_b,
              od_w_in, od_w_out, na_rpb,
              router_w, router_b, moe_w1, moe_b1, moe_w2, moe_b2):

    def even_mixer(h, i, h0, return_final):
        u = h @ ev_w_in[i]
        y_hy = hyena_mixer(u[..., :3 * HY_CH], hy_conv_w[i], hy_conv_b[i], hy_f_w1[i], hy_f_b1[i],
                           hy_f_w2[i], hy_f_b2[i], hy_f_w3[i], hy_f_freq[i], hy_bias[i])
        y_s5, fin = s5_mixer(u[..., 3 * HY_CH:], h0, s5_lam_re[i], s5_lam_im[i], s5_log_dt[i],
                             s5_b_re[i], s5_b_im[i], s5_c_re[i], s5_c_im[i], s5_d[i],
                             s5_glu_w[i], s5_glu_b[i], return_final)
        return jnp.concatenate([y_hy, y_s5], -1) @ ev_w_out[i], fin

    def qkv_proj(h, i):
        bsz, L, _ = h.shape
        qkv = (h @ od_w_in[i]).reshape(bsz, L, 3, NA_HEADS, NA_HEAD_DIM)
        return qkv[:, :, 0], qkv[:, :, 1], qkv[:, :, 2]

    def channel_mixer(x, l, shift, scale, gate):
        h = x * (1.0 + scale) + shift
        y = moe(h, router_w[l], router_b[l], moe_w1[l], moe_b1[l], moe_w2[l], moe_b2[l])
        return layer_norm(DN_ALPHA * x + gate * y, ln2_g[l], ln2_b[l])

    xp = x_prompt
    bp = xp.shape[0]
    s5_states, k_list, v_list = [], [], []
    for l in range(DEPTH):
        i = l // 2
        sh1, sc1, g1, sh2, sc2, g2 = modulation(c_ctx, w_mod[l], b_mod[l])
        h = xp * (1.0 + sc1) + sh1
        if l % 2 == 0:
            h0 = jnp.zeros((bp, 2, S5_GROUPS, S5_STATE), jnp.complex64)
            y, fin = even_mixer(h, i, h0, True)
            s5_states.append(jnp.stack([fin.real, fin.imag], -1))
        else:
            q, k, v = qkv_proj(h, i)
            k_list.append(k)
            v_list.append(v)
            y = ctx_attention(q, k, v) @ od_w_out[i]
        xp = layer_norm(DN_ALPHA * xp + g1 * y, ln1_g[l], ln1_b[l])
        xp = channel_mixer(xp, l, sh2, sc2, g2)
    y_prompt = xp
    new_state_s5 = jnp.stack(s5_states, 1).astype(x_prompt.dtype)
    new_cache_na_k = jnp.stack(k_list, 1)
    new_cache_na_v = jnp.stack(v_list, 1)

    xs = x_sample
    for l in range(DEPTH):
        i = l // 2
        sh1, sc1, g1, sh2, sc2, g2 = modulation(c, w_mod[l], b_mod[l])
        h = xs * (1.0 + sc1) + sh1
        if l % 2 == 0:
            st = state_s5[:, i]
            h0 = lax.complex(st[..., 0].astype(jnp.float32), st[..., 1].astype(jnp.float32))
            y, _ = even_mixer(h, i, h0, False)
        else:
            q, k, v = qkv_proj(h, i)
            y = na_latent(q, k, v, cache_na_k[:, i], cache_na_v[:, i], na_rpb[i]) @ od_w_out[i]
        xs = layer_norm(DN_ALPHA * xs + g1 * y, ln1_g[l], ln1_b[l])
        xs = channel_mixer(xs, l, sh2, sc2, g2)
    y_sample = xs

    return (y_prompt, y_sample, new_state_s5, new_cache_na_k, new_cache_na_v)
```

```python
import functools
import math

import jax
import jax.numpy as jnp
import numpy as np
from jax import lax
from jax.experimental import pallas as pl
from jax.experimental.pallas import tpu as pltpu

F32 = jnp.float32
BF16 = jnp.bfloat16
HIGHEST = lax.Precision.HIGHEST

D = 1024
N_CTX_SEQ, CTX_LEN = 32, 256
N_LAT_SEQ, LAT_LEN = 2, 4096
T_CTX = N_CTX_SEQ * CTX_LEN
T_LAT = N_LAT_SEQ * LAT_LEN
T_ALL = T_CTX + T_LAT
DEPTH = 2
HY_CH = 512
S5_CH = 512
S5_GROUPS, S5_GROUP, S5_STATE = 32, 16, 64
S5_NS = S5_GROUPS * S5_STATE
HY_EMB, HY_BANDS, HY_ORDER = 33, 16, 64
N_HEADS, HEAD_DIM = 16, 64
GRID_W, WIN_R, WIN_C = 64, 8, 16
N_EXPERTS, TOP_K, D_FF = 32, 4, 1024
SWIGLU_LIMIT, SWIGLU_ALPHA = 7.0, 1.702
LN_EPS = 1e-5
DN_ALPHA = (2 * DEPTH) ** 0.25
NEG_BIG = -1e30

ROW_TILE = 256
MOE_TILE = 256
N_ASSIGN = T_ALL * TOP_K
N_MOE_BLOCKS = N_ASSIGN // MOE_TILE
N_MOE_ITEMS = N_MOE_BLOCKS + N_EXPERTS - 1
COMB_TILE = 128
S5_CHUNK = 256
S5_SEG = S5_CHUNK // 8
S5_COLS = 512
VMEM_LIMIT = 56 * 1024 * 1024


def _cparams(sem, vmem=None):
    return pltpu.CompilerParams(dimension_semantics=sem, vmem_limit_bytes=vmem or VMEM_LIMIT)


def _mod_index(i, rows_per_tile):
    n_ctx = T_CTX // rows_per_tile
    per_lat = LAT_LEN // rows_per_tile
    return jnp.where(i < n_ctx, 0, 1 + (i - n_ctx) // per_lat)


def _mm_kernel(a_ref, b_ref, bias_ref, o_ref, acc_ref, *, nk, precise, silu_a):
    k = pl.program_id(3)

    @pl.when(k == 0)
    def _():
        acc_ref[...] = jnp.zeros_like(acc_ref)

    a = a_ref[...]
    if silu_a:
        a = a * jax.nn.sigmoid(a)
    if precise:
        acc_ref[...] += jnp.dot(a, b_ref[...], preferred_element_type=F32, precision=HIGHEST)
    else:
        acc_ref[...] += jnp.dot(a.astype(BF16), b_ref[...].astype(BF16), preferred_element_type=F32)

    @pl.when(k == nk - 1)
    def _():
        o_ref[...] = acc_ref[...] + bias_ref[...]


def matmul(a, b, bias=None, *, tm, tn, tk, precise=False, silu_a=False):
    squeeze = b.ndim == 2
    if squeeze:
        b = b[None]
    nb, kdim, n = b.shape
    m = a.shape[0]
    if bias is None:
        bias = jnp.zeros((1, n), F32)
    nk = kdim // tk
    out = pl.pallas_call(
        functools.partial(_mm_kernel, nk=nk, precise=precise, silu_a=silu_a),
        grid=(nb, m // tm, n // tn, nk),
        in_specs=[
            pl.BlockSpec((tm, tk), lambda s, i, j, k: (i, k)),
            pl.BlockSpec((None, tk, tn), lambda s, i, j, k: (s, k, j)),
            pl.BlockSpec((1, tn), lambda s, i, j, k: (0, j)),
        ],
        out_specs=pl.BlockSpec((None, tm, tn), lambda s, i, j, k: (s, i, j)),
        out_shape=jax.ShapeDtypeStruct((nb, m, n), F32),
        scratch_shapes=[pltpu.VMEM((tm, tn), F32)],
        compiler_params=_cparams(("parallel", "parallel", "parallel", "arbitrary")),
        name="matmul",
    )(a, b, bias)
    return out[0] if squeeze else out


def _modlinear_kernel(x_ref, mod_ref, w_ref, o_ref):
    m = mod_ref[...]
    h = x_ref[...] * (1.0 + m[1:2]) + m[0:1]
    o_ref[...] = jnp.dot(h.astype(BF16), w_ref[...], preferred_element_type=F32)


def modlinear(x, mod, w_bf16):
    n = w_bf16.shape[1]
    return pl.pallas_call(
        _modlinear_kernel,
        grid=(T_ALL // ROW_TILE,),
        in_specs=[
            pl.BlockSpec((ROW_TILE, D), lambda i: (i, 0)),
            pl.BlockSpec((None, 8, D), lambda i: (_mod_index(i, ROW_TILE), 0, 0)),
            pl.BlockSpec((D, n), lambda i: (0, 0)),
        ],
        out_specs=pl.BlockSpec((ROW_TILE, n), lambda i: (i, 0)),
        out_shape=jax.ShapeDtypeStruct((T_ALL, n), F32),
        compiler_params=_cparams(("parallel",)),
        name="modlinear",
    )(x, mod, w_bf16)


def _filter_kernel(z_ref, w1_ref, b1_ref, w2_ref, b2_ref, w3_ref, fq_ref, dl_ref, hsum_ref, hdiff_ref, *, tile):
    z = z_ref[...]
    fq = fq_ref[...]
    h = jnp.sin(fq * (jnp.dot(z, w1_ref[...], preferred_element_type=F32, precision=HIGHEST) + b1_ref[...]))
    h = jnp.sin(fq * (jnp.dot(h, w2_ref[...], preferred_element_type=F32, precision=HIGHEST) + b2_ref[...]))
    h = jnp.dot(h, w3_ref[...], preferred_element_type=F32, precision=HIGHEST)
    decay = jnp.exp(-z[:, 0:1] * dl_ref[...])
    hf = h[:, :HY_CH] * decay
    hb = h[:, HY_CH:] * decay
    row = lax.broadcasted_iota(jnp.int32, (tile, 1), 0) + pl.program_id(0) * tile
    hsum_ref[...] = hf + hb
    hdiff_ref[...] = jnp.where(row == 0, hf + hb, hf - hb)


def hyena_filter_taps(seq_len, w1, b1, w2, b2, w3, freq):
    t = jnp.linspace(0.0, 1.0, seq_len, dtype=F32)[:, None]
    w = 2.0 * math.pi * jnp.arange(seq_len, dtype=F32)[:, None] / seq_len
    f = jnp.linspace(1e-4, HY_BANDS - 1, HY_BANDS, dtype=F32)[None, :]
    z = jnp.concatenate([t, jnp.cos(f * w), -jnp.sin(f * w)], -1)
    z = jnp.pad(z, ((0, 0), (0, 128 - HY_EMB)))
    pad_o = 128 - HY_ORDER
    w1p = jnp.pad(w1, ((0, 128 - HY_EMB), (0, pad_o)))
    w2p = jnp.pad(w2, ((0, pad_o), (0, pad_o)))
    w3p = jnp.pad(w3, ((0, pad_o), (0, 0)))
    b1p = jnp.pad(b1, (0, pad_o))[None]
    b2p = jnp.pad(b2, (0, pad_o))[None]
    fqp = jnp.pad(freq, (0, pad_o))[None]
    max_decay = math.log(1e-2) / 0.3
    min_decay = math.log(1e-2) / 1.5
    absdelta = jnp.abs(jnp.linspace(min_decay, max_decay, HY_CH, dtype=F32))[None]
    tile = 256
    full = lambda shp: pl.BlockSpec(shp, lambda i: (0, 0))
    return pl.pallas_call(
        functools.partial(_filter_kernel, tile=tile),
        grid=(seq_len // tile,),
        in_specs=[pl.BlockSpec((tile, 128), lambda i: (i, 0)), full((128, 128)), full((1, 128)), full((128, 128)),
                  full((1, 128)), full((128, 2 * HY_CH)), full((1, 128)), full((1, HY_CH))],
        out_specs=[pl.BlockSpec((tile, HY_CH), lambda i: (i, 0))] * 2,
        out_shape=[jax.ShapeDtypeStruct((seq_len, HY_CH), F32)] * 2,
        compiler_params=_cparams(("parallel",)),
        name="hyena_filter",
    )(z, w1p, b1p, w2p, b2p, w3p, fqp, absdelta)


def _dft_tables(seq_len):
    a = jnp.arange(seq_len, dtype=jnp.int32)
    idx = ((2 * a[:, None] + 1) * a[None, :]) % (4 * seq_len)
    ang = idx.astype(F32) * (math.pi / (2 * seq_len))
    cm = jnp.cos(ang)
    nsm = -jnp.sin(ang)
    return cm.astype(BF16), nsm.astype(BF16), cm.T.astype(BF16), nsm.T.astype(BF16)


def _hyena_pre_kernel(u0_ref, u1_ref, u2_ref, w0_ref, w1_ref, w2_ref, b0_ref, b1_ref, b2_ref,
                      vx_ref, vxb_ref, x0_ref, *, seq_len):
    row = lax.broadcasted_iota(jnp.int32, (seq_len, 1), 0)

    def short_conv(u_ref, w_ref, b_ref):
        a = u_ref[...]
        w = w_ref[...]
        prev = jnp.where(row == 0, 0.0, pltpu.roll(a, 1, 0))
        nxt = jnp.where(row == seq_len - 1, 0.0, pltpu.roll(a, seq_len - 1, 0))
        return prev * w[0:1] + a * w[1:2] + nxt * w[2:3] + b_ref[...]

    x0 = short_conv(u0_ref, w0_ref, b0_ref)
    x1 = short_conv(u1_ref, w1_ref, b1_ref)
    v = short_conv(u2_ref, w2_ref, b2_ref)
    vx = v * x1
    vx_ref[...] = vx
    vxb_ref[...] = vx.astype(BF16)
    x0_ref[...] = x0


def hyena_pre(u, conv_w, conv_b, *, n_seq, seq_len, row_off):
    cb = 128
    ncb = HY_CH // cb
    rb0 = row_off // seq_len
    uspec = lambda part: pl.BlockSpec((seq_len, cb), lambda b, j: (rb0 + b, part * ncb + j))
    wspec = lambda part: pl.BlockSpec((3, cb), lambda b, j: (0, part * ncb + j))
    bspec = lambda part: pl.BlockSpec((1, cb), lambda b, j: (0, part * ncb + j))
    ospec = pl.BlockSpec((seq_len, cb), lambda b, j: (b, j))
    rows = n_seq * seq_len
    return pl.pallas_call(
        functools.partial(_hyena_pre_kernel, seq_len=seq_len),
        grid=(n_seq, ncb),
        in_specs=[uspec(0), uspec(1), uspec(2), wspec(0), wspec(1), wspec(2), bspec(0), bspec(1), bspec(2)],
        out_specs=[ospec, ospec, ospec],
        out_shape=[jax.ShapeDtypeStruct((rows, HY_CH), F32), jax.ShapeDtypeStruct((rows, HY_CH), BF16),
                   jax.ShapeDtypeStruct((rows, HY_CH), F32)],
        compiler_params=_cparams(("parallel", "parallel")),
        name="hyena_pre",
    )(u, u, u, conv_w, conv_w, conv_w, conv_b, conv_b, conv_b)


def _dft_fwd_kernel(cm_ref, nsm_ref, v_ref, hr_ref, hi_ref, zr_ref, zi_ref, accr_ref, acci_ref, *, nk):
    k = pl.program_id(3)

    @pl.when(k == 0)
    def _():
        accr_ref[...] = jnp.zeros_like(accr_ref)
        acci_ref[...] = jnp.zeros_like(acci_ref)

    v = v_ref[...]
    accr_ref[...] += jnp.dot(cm_ref[...], v, preferred_element_type=F32)
    acci_ref[...] += jnp.dot(nsm_ref[...], v, preferred_element_type=F32)

    @pl.when(k == nk - 1)
    def _():
        xr, xi = accr_ref[...], acci_ref[...]
        hr, hi = hr_ref[...], hi_ref[...]
        zr_ref[...] = (xr * hr - xi * hi).astype(BF16)
        zi_ref[...] = (xr * hi + xi * hr).astype(BF16)


def dft_fwd(cm, nsm, vxb, hr, hi, *, n_seq, seq_len, tile):
    nk = seq_len // tile
    tn = HY_CH
    v3 = vxb.reshape(n_seq, seq_len, HY_CH)
    zspec = pl.BlockSpec((None, tile, tn), lambda s, i, j, k: (s, i, j))
    return pl.pallas_call(
        functools.partial(_dft_fwd_kernel, nk=nk),
        grid=(n_seq, seq_len // tile, HY_CH // tn, nk),
        in_specs=[
            pl.BlockSpec((tile, tile), lambda s, i, j, k: (i, k)),
            pl.BlockSpec((tile, tile), lambda s, i, j, k: (i, k)),
            pl.BlockSpec((None, tile, tn), lambda s, i, j, k: (s, k, j)),
            pl.BlockSpec((tile, tn), lambda s, i, j, k: (i, j)),
            pl.BlockSpec((tile, tn), lambda s, i, j, k: (i, j)),
        ],
        out_specs=[zspec, zspec],
        out_shape=[jax.ShapeDtypeStruct((n_seq, seq_len, HY_CH), BF16)] * 2,
        scratch_shapes=[pltpu.VMEM((tile, tn), F32), pltpu.VMEM((tile, tn), F32)],
        compiler_params=_cparams(("parallel", "parallel", "parallel", "arbitrary")),
        name="dft_fwd",
    )(cm, nsm, v3, hr, hi)


def _dft_inv_kernel(cmt_ref, nsmt_ref, zr_ref, zi_ref, vx_ref, x0_ref, bias_ref, o_ref, acc_ref, *, nk, inv_len):
    k = pl.program_id(3)

    @pl.when(k == 0)
    def _():
        acc_ref[...] = jnp.zeros_like(acc_ref)

    acc_ref[...] += (jnp.dot(cmt_ref[...], zr_ref[...], preferred_element_type=F32)
                     + jnp.dot(nsmt_ref[...], zi_ref[...], preferred_element_type=F32))

    @pl.when(k == nk - 1)
    def _():
        conv = acc_ref[...] * inv_len
        o_ref[...] = (conv + vx_ref[...] * bias_ref[...]) * x0_ref[...]


def dft_inv(cmt, nsmt, zr, zi, vx, x0, bias, *, n_seq, seq_len, tile):
    nk = seq_len // tile
    tn = HY_CH
    zspec = pl.BlockSpec((None, tile, tn), lambda s, i, j, k: (s, k, j))
    espec = pl.BlockSpec((None, tile, tn), lambda s, i, j, k: (s, i, j))
    out = pl.pallas_call(
        functools.partial(_dft_inv_kernel, nk=nk, inv_len=1.0 / seq_len),
        grid=(n_seq, seq_len // tile, HY_CH // tn, nk),
        in_specs=[
            pl.BlockSpec((tile, tile), lambda s, i, j, k: (i, k)),
            pl.BlockSpec((tile, tile), lambda s, i, j, k: (i, k)),
            zspec, zspec, espec, espec,
            pl.BlockSpec((1, tn), lambda s, i, j, k: (0, j)),
        ],
        out_specs=espec,
        out_shape=jax.ShapeDtypeStruct((n_seq, seq_len, HY_CH), F32),
        scratch_shapes=[pltpu.VMEM((tile, tn), F32)],
        compiler_params=_cparams(("parallel", "parallel", "parallel", "arbitrary")),
        name="dft_inv",
    )(cmt, nsmt, zr, zi, vx.reshape(n_seq, seq_len, HY_CH), x0.reshape(n_seq, seq_len, HY_CH), bias)
    return out.reshape(n_seq * seq_len, HY_CH)


def hyena_group(u, conv_w, conv_b, taps, bias, *, n_seq, seq_len, row_off, tile):
    hsum, hdiff = taps
    cm, nsm, cmt, nsmt = _dft_tables(seq_len)
    tt = min(tile, 512)
    hr = matmul(cm, hsum, tm=tt, tn=HY_CH, tk=tt)
    hi = matmul(nsm, hdiff, tm=tt, tn=HY_CH, tk=tt)
    vx, vxb, x0 = hyena_pre(u, conv_w, conv_b, n_seq=n_seq, seq_len=seq_len, row_off=row_off)
    zr, zi = dft_fwd(cm, nsm, vxb, hr, hi, n_seq=n_seq, seq_len=seq_len, tile=tile)
    return dft_inv(cmt, nsmt, zr, zi, vx, x0, bias, n_seq=n_seq, seq_len=seq_len, tile=tile)


def _s5_tables(lam_re, lam_im, log_dt, b_re, b_im, c_re, c_im, reverse):
    dt = jnp.exp(log_dt)[:, None]
    a = lam_re * dt
    b = lam_im * dt
    mag = jnp.exp(a)
    lbr, lbi = mag * jnp.cos(b), mag * jnp.sin(b)
    den = lam_re * lam_re + lam_im * lam_im
    qr = ((lbr - 1.0) * lam_re + lbi * lam_im) / den
    qi = (lbi * lam_re - (lbr - 1.0) * lam_im) / den
    bbr = qr[..., None] * b_re - qi[..., None] * b_im
    bbi = qr[..., None] * b_im + qi[..., None] * b_re
    eye = jnp.eye(S5_GROUPS, dtype=F32)

    def in_block(m):
        return jnp.einsum("gnc,gh->gchn", m, eye).reshape(S5_CH, S5_NS)

    def out_block(m):
        return jnp.einsum("gcn,gh->gnhc", m, eye).reshape(S5_NS, S5_CH)

    b_blk = jnp.concatenate([in_block(bbr), in_block(bbi)], axis=1).astype(BF16)
    c_blk = jnp.concatenate([out_block(c_re), out_block(-c_im)], axis=0).astype(BF16)
    flat = lambda m: m.reshape(1, S5_NS)
    lam = jnp.concatenate([flat(lbr), flat(lbi)], axis=1)
    steps = jnp.arange(1, S5_SEG + 1, dtype=F32)
    if reverse:
        steps = steps[::-1]
    pa = a.reshape(1, S5_NS) * steps[:, None]
    pb = b.reshape(1, S5_NS) * steps[:, None]
    pw = jnp.concatenate([jnp.exp(pa) * jnp.cos(pb), jnp.exp(pa) * jnp.sin(pb)], axis=1)
    sa, sb = flat(a) * S5_SEG, flat(b) * S5_SEG
    lam_seg = jnp.concatenate([jnp.exp(sa) * jnp.cos(sb), jnp.exp(sa) * jnp.sin(sb)], axis=1)
    return b_blk, c_blk, lam, lam_seg, pw


def _s5_kernel(u_ref, bblk_ref, cblk_ref, lam_ref, lseg_ref, pw_ref, h0_ref, y_ref, fin_ref,
               bu_ref, hend_ref, cin_ref, carry_ref, *, n_chunks, reverse):
    j = pl.program_id(1)
    ns = S5_NS

    @pl.when(j == 0)
    def _():
        carry_ref[...] = h0_ref[...]

    bu_ref[...] = jnp.dot(u_ref[...].astype(BF16), bblk_ref[...], preferred_element_type=F32)

    for cb in range(ns // S5_COLS):
        re_cols = pl.ds(cb * S5_COLS, S5_COLS)
        im_cols = pl.ds(ns + cb * S5_COLS, S5_COLS)
        lr = jnp.broadcast_to(lam_ref[:, re_cols], (8, S5_COLS))
        li = jnp.broadcast_to(lam_ref[:, im_cols], (8, S5_COLS))

        def step(kk, carry, re_cols=re_cols, im_cols=im_cols, lr=lr, li=li):
            hr, hi = carry
            k = (S5_SEG - 1 - kk) if reverse else kk
            rows = pl.ds(pl.multiple_of(k * 8, 8), 8)
            nr = lr * hr - li * hi + bu_ref[rows, re_cols]
            ni = lr * hi + li * hr + bu_ref[rows, im_cols]
            bu_ref[rows, re_cols] = nr
            bu_ref[rows, im_cols] = ni
            return nr, ni

        zero = jnp.zeros((8, S5_COLS), F32)
        hr, hi = lax.fori_loop(0, S5_SEG, step, (zero, zero), unroll=4)
        hend_ref[:, re_cols] = hr
        hend_ref[:, im_cols] = hi

    cr = carry_ref[:, :ns]
    ci = carry_ref[:, ns:]
    lsr = lseg_ref[:, :ns]
    lsi = lseg_ref[:, ns:]
    for p in range(8):
        s = 7 - p if reverse else p
        cin_ref[s:s + 1, :ns] = cr
        cin_ref[s:s + 1, ns:] = ci
        er = hend_ref[s:s + 1, :ns]
        ei = hend_ref[s:s + 1, ns:]
        cr, ci = er + lsr * cr - lsi * ci, ei + lsr * ci + lsi * cr
    carry_ref[:, :ns] = cr
    carry_ref[:, ns:] = ci

    for cb in range(ns // S5_COLS):
        re_cols = pl.ds(cb * S5_COLS, S5_COLS)
        im_cols = pl.ds(ns + cb * S5_COLS, S5_COLS)
        cinr = cin_ref[:, re_cols]
        cini = cin_ref[:, im_cols]

        def fix(k, _, re_cols=re_cols, im_cols=im_cols, cinr=cinr, cini=cini):
            rows = pl.ds(pl.multiple_of(k * 8, 8), 8)
            pr = pw_ref[pl.ds(k, 1), re_cols]
            pi = pw_ref[pl.ds(k, 1), im_cols]
            bu_ref[rows, re_cols] += pr * cinr - pi * cini
            bu_ref[rows, im_cols] += pr * cini + pi * cinr
            return 0

        lax.fori_loop(0, S5_SEG, fix, 0, unroll=4)

    y_ref[...] = jnp.dot(bu_ref[...].astype(BF16), cblk_ref[...], preferred_element_type=F32)

    @pl.when(j == n_chunks - 1)
    def _():
        fin_ref[...] = carry_ref[...]


def s5_direction(u_perm, tables, h0, *, n_seq, seq_len, reverse):
    b_blk, c_blk, lam, lam_seg, pw = tables
    n_chunks = seq_len // S5_CHUNK
    chunk = (lambda b, j: (b * n_chunks + (n_chunks - 1 - j), 0)) if reverse else (lambda b, j: (b * n_chunks + j, 0))
    const = lambda shp: pl.BlockSpec(shp, lambda b, j: (0, 0))
    y, fin = pl.pallas_call(
        functools.partial(_s5_kernel, n_chunks=n_chunks, reverse=reverse),
        grid=(n_seq, n_chunks),
        in_specs=[
            pl.BlockSpec((S5_CHUNK, S5_CH), chunk),
            const((S5_CH, 2 * S5_NS)), const((2 * S5_NS, S5_CH)), const((1, 2 * S5_NS)), const((1, 2 * S5_NS)),
            const((S5_SEG, 2 * S5_NS)),
            pl.BlockSpec((None, 1, 2 * S5_NS), lambda b, j: (b, 0, 0)),
        ],
        out_specs=[pl.BlockSpec((S5_CHUNK, S5_CH), chunk),
                   pl.BlockSpec((None, 1, 2 * S5_NS), lambda b, j: (b, 0, 0))],
        out_shape=[jax.ShapeDtypeStruct((n_seq * seq_len, S5_CH), F32),
                   jax.ShapeDtypeStruct((n_seq, 1, 2 * S5_NS), F32)],
        scratch_shapes=[pltpu.VMEM((S5_CHUNK, 2 * S5_NS), F32), pltpu.VMEM((8, 2 * S5_NS), F32),
                        pltpu.VMEM((8, 2 * S5_NS), F32), pltpu.VMEM((1, 2 * S5_NS), F32)],
        compiler_params=_cparams(("parallel", "arbitrary")),
        name="s5_bwd" if reverse else "s5_fwd",
    )(u_perm, b_blk, c_blk, lam, lam_seg, pw, h0)
    return y, fin


def _interleave(x, n_seq, seq_len):
    nc = seq_len // S5_CHUNK
    c = x.shape[-1]
    return x.reshape(n_seq, nc, 8, S5_SEG, c).transpose(0, 1, 3, 2, 4).reshape(n_seq * seq_len, c)


def _deinterleave(x, n_seq, seq_len):
    nc = seq_len // S5_CHUNK
    c = x.shape[-1]
    return x.reshape(n_seq, nc, S5_SEG, 8, c).transpose(0, 1, 3, 2, 4).reshape(n_seq * seq_len, c)


def _s5_post_kernel(u_ref, yf_ref, yb_ref, d_ref, w_ref, b_ref, o_ref):
    y = d_ref[...] * u_ref[...] + yf_ref[...] + yb_ref[...]
    cdf = 0.5 * (1.0 + jnp.tanh(math.sqrt(2.0 / math.pi) * (y + 0.044715 * (y * y * y))))
    g = y * cdf
    z = jnp.dot(g.astype(BF16), w_ref[...], preferred_element_type=F32) + b_ref[...]
    o_ref[...] = z[:, :S5_CH] * jax.nn.sigmoid(z[:, S5_CH:])


def s5_post(u, yf, yb, d, glu_w, glu_b):
    row = pl.BlockSpec((ROW_TILE, S5_CH), lambda i: (i, 0))
    return pl.pallas_call(
        _s5_post_kernel,
        grid=(T_ALL // ROW_TILE,),
        in_specs=[pl.BlockSpec((ROW_TILE, S5_CH), lambda i: (i, 3 * HY_CH // S5_CH)), row, row,
                  pl.BlockSpec((1, S5_CH), lambda i: (0, 0)),
                  pl.BlockSpec((S5_CH, 2 * S5_CH), lambda i: (0, 0)),
                  pl.BlockSpec((1, 2 * S5_CH), lambda i: (0, 0))],
        out_specs=row,
        out_shape=jax.ShapeDtypeStruct((T_ALL, S5_CH), F32),
        compiler_params=_cparams(("parallel",)),
        name="s5_post",
    )(u, yf, yb, d, glu_w, glu_b)


def _head_mask(shape):
    return lax.broadcasted_iota(jnp.int32, shape, 1) < HEAD_DIM


def _ctx_attn_kernel(q_ref, k_ref, v_ref, o_ref):
    q = q_ref[...]
    k = k_ref[...].astype(BF16)
    v = v_ref[...].astype(BF16)
    low = _head_mask(q.shape)
    outs = []
    for hh in range(2):
        qm = jnp.where(low if hh == 0 else ~low, q, 0.0).astype(BF16)
        s = lax.dot_general(qm, k, (((1,), (1,)), ((), ())), preferred_element_type=F32) * (HEAD_DIM ** -0.5)
        m = jnp.max(s, axis=-1, keepdims=True)
        p = jnp.exp(s - m)
        den = jnp.sum(p, axis=-1, keepdims=True)
        outs.append(jnp.dot(p.astype(BF16), v, preferred_element_type=F32) / den)
    o_ref[...] = jnp.where(low, outs[0], outs[1])


def ctx_attention(qkv):
    hp = N_HEADS // 2
    blk = lambda part: pl.BlockSpec((CTX_LEN, 128), lambda b, h: (b, part * hp + h))
    return pl.pallas_call(
        _ctx_attn_kernel,
        grid=(N_CTX_SEQ, hp),
        in_specs=[blk(0), blk(1), blk(2)],
        out_specs=pl.BlockSpec((CTX_LEN, 128), lambda b, h: (b, h)),
        out_shape=jax.ShapeDtypeStruct((T_CTX, D), F32),
        compiler_params=_cparams(("parallel", "parallel")),
        name="ctx_attention",
    )(qkv, qkv, qkv)


NA_ROWS_PER_STEP = 8


def _na_kernel(q_ref, k_ref, v_ref, kc_ref, vc_ref, bias_ref, o_ref):
    rb = pl.program_id(2)
    kc = kc_ref[...]
    vc = vc_ref[...]
    scale = HEAD_DIM ** -0.5
    low = _head_mask((GRID_W, 128))
    n_rows = LAT_LEN // GRID_W
    for i in range(NA_ROWS_PER_STEP):
        r = rb * NA_ROWS_PER_STEP + i
        rs = jnp.clip(r - WIN_R // 2, 0, n_rows - WIN_R)
        dlt = r - rs
        rows = pl.ds(pl.multiple_of(rs * GRID_W, GRID_W), WIN_R * GRID_W)
        kw = k_ref[rows, :]
        vw = v_ref[rows, :]
        q = q_ref[i * GRID_W:(i + 1) * GRID_W, :]
        outs = []
        for hh in range(2):
            qm = jnp.where(low if hh == 0 else ~low, q, 0.0).astype(BF16)
            s = lax.dot_general(qm, kw, (((1,), (1,)), ((), ())), preferred_element_type=F32) * scale
            s = s + bias_ref[dlt, hh]
            sc = lax.dot_general(qm, kc, (((1,), (1,)), ((), ())), preferred_element_type=F32) * scale
            m = jnp.maximum(jnp.max(s, axis=-1, keepdims=True), jnp.max(sc, axis=-1, keepdims=True))
            p = jnp.exp(s - m)
            pc = jnp.exp(sc - m)
            den = jnp.sum(p, axis=-1, keepdims=True) + jnp.sum(pc, axis=-1, keepdims=True)
            o = (jnp.dot(p.astype(BF16), vw, preferred_element_type=F32)
                 + jnp.dot(pc.astype(BF16), vc, preferred_element_type=F32))
            outs.append(o / den)
        o_ref[i * GRID_W:(i + 1) * GRID_W, :] = jnp.where(low, outs[0], outs[1])


def _na_bias_table(rpb):
    j = np.arange(GRID_W)
    cs = np.clip(j - WIN_C // 2, 0, GRID_W - WIN_C)
    c = np.arange(GRID_W)
    inside = (c[None, :] >= cs[:, None]) & (c[None, :] < cs[:, None] + WIN_C)
    col_rel = np.clip(c[None, :] - j[:, None] + WIN_C - 1, 0, 2 * WIN_C - 2)
    dl = np.arange(WIN_R)
    iw = np.arange(WIN_R)
    row_rel = np.clip(iw[None, :] - dl[:, None] + WIN_R - 1, 0, 2 * WIN_R - 2)
    t = rpb[:, row_rel]
    t = t[:, :, :, col_rel]
    t = jnp.where(inside[None, None, None], t, NEG_BIG)
    t = t.transpose(1, 0, 3, 2, 4)
    return t.reshape(WIN_R, N_HEADS, GRID_W, WIN_R * GRID_W).astype(F32)


def na_attention(qkv, kvb, kcb, vcb, bias):
    hp = N_HEADS // 2
    q_rows = NA_ROWS_PER_STEP * GRID_W
    steps = LAT_LEN // q_rows
    q_off = T_CTX // q_rows
    return pl.pallas_call(
        _na_kernel,
        grid=(N_LAT_SEQ, hp, steps),
        in_specs=[
            pl.BlockSpec((q_rows, 128), lambda b, h, r: (q_off + b * steps + r, h)),
            pl.BlockSpec((LAT_LEN, 128), lambda b, h, r: (b, h)),
            pl.BlockSpec((LAT_LEN, 128), lambda b, h, r: (b, hp + h)),
            pl.BlockSpec((CTX_LEN, 128), lambda b, h, r: (b, h)),
            pl.BlockSpec((CTX_LEN, 128), lambda b, h, r: (b, h)),
            pl.BlockSpec((WIN_R, 2, GRID_W, WIN_R * GRID_W), lambda b, h, r: (0, h, 0, 0)),
        ],
        out_specs=pl.BlockSpec((q_rows, 128), lambda b, h, r: (b * steps + r, h)),
        out_shape=jax.ShapeDtypeStruct((T_LAT, D), F32),
        compiler_params=_cparams(("parallel", "parallel", "arbitrary")),
        name="na_attention",
    )(qkv, kvb, kvb, kcb, vcb, bias)


def _layer_norm(r, g, b):
    mu = jnp.mean(r, axis=-1, keepdims=True)
    c = r - mu
    var = jnp.mean(c * c, axis=-1, keepdims=True)
    return c * lax.rsqrt(var + LN_EPS) * g + b


def _proj_kernel(*refs, n_in):
    a_refs = refs[:n_in]
    w_refs = refs[n_in:2 * n_in]
    x_ref, mod_ref, g_ref, b_ref, rwh_ref, rwl_ref, rb_ref, x1_ref, h2_ref, gate_ref, rank_ref = refs[2 * n_in:]
    y = None
    for a_ref, w_ref in zip(a_refs, w_refs):
        part = jnp.dot(a_ref[...].astype(BF16), w_ref[...], preferred_element_type=F32)
        y = part if y is None else y + part
    m = mod_ref[...]
    x1 = _layer_norm(DN_ALPHA * x_ref[...] + m[2:3] * y, g_ref[...], b_ref[...])
    x1_ref[...] = x1
    h2 = x1 * (1.0 + m[4:5]) + m[3:4]
    h2_ref[...] = h2

    hh = h2.astype(BF16)
    hl = (h2 - hh.astype(F32)).astype(BF16)
    logits = (jnp.dot(hh, rwh_ref[...], preferred_element_type=F32)
              + jnp.dot(hh, rwl_ref[...], preferred_element_type=F32)
              + jnp.dot(hl, rwh_ref[...], preferred_element_type=F32)) + rb_ref[...]

    lane = lax.broadcasted_iota(jnp.int32, logits.shape, 1)
    rank = jnp.zeros(logits.shape, jnp.int32)
    work = logits
    vals = []
    sels = []
    for k in range(TOP_K):
        mx = jnp.max(work, axis=-1, keepdims=True)
        idx = jnp.min(jnp.where(work == mx, lane, 128), axis=-1, keepdims=True)
        sel = lane == idx
        rank = jnp.where(sel, k + 1, rank)
        work = jnp.where(sel, -jnp.inf, work)
        vals.append(mx)
        sels.append(sel)
    exps = [jnp.exp(v - vals[0]) for v in vals]
    den = exps[0] + exps[1] + exps[2] + exps[3]
    gate = jnp.zeros(logits.shape, F32)
    for sel, e in zip(sels, exps):
        gate = jnp.where(sel, e / den, gate)
    gate_ref[...] = gate
    rank_ref[...] = rank


def proj_res_ln(acts, weights, x, mod, ln_g, ln_b, rw_hi, rw_lo, rb):
    n_in = len(acts)
    row = lambda c: pl.BlockSpec((ROW_TILE, c), lambda i: (i, 0))
    full = lambda shp: pl.BlockSpec(shp, lambda i: (0, 0))
    in_specs = ([row(a.shape[1]) for a in acts] + [full(w.shape) for w in weights]
                + [row(D), pl.BlockSpec((None, 8, D), lambda i: (_mod_index(i, ROW_TILE), 0, 0)),
                   full((1, D)), full((1, D)), full((D, 128)), full((D, 128)), full((1, 128))])
    return pl.pallas_call(
        functools.partial(_proj_kernel, n_in=n_in),
        grid=(T_ALL // ROW_TILE,),
        in_specs=in_specs,
        out_specs=[row(D), row(D), row(128), row(128)],
        out_shape=[jax.ShapeDtypeStruct((T_ALL, D), F32), jax.ShapeDtypeStruct((T_ALL, D), F32),
                   jax.ShapeDtypeStruct((T_ALL, 128), F32), jax.ShapeDtypeStruct((T_ALL, 128), jnp.int32)],
        compiler_params=_cparams(("parallel",)),
        name="proj_res_ln",
    )(*acts, *weights, x, mod, ln_g, ln_b, rw_hi, rw_lo, rb)


def _dispatch_kernel(dest_ref, h_ref, xs_ref, sem):
    def copy(r, d):
        return pltpu.make_async_copy(h_ref.at[pl.ds(r, 1)], xs_ref.at[pl.ds(d, 1)], sem)

    def issue(r, _):
        for k in range(TOP_K):
            copy(r, dest_ref[r * TOP_K + k]).start()
        return 0

    lax.fori_loop(0, ROW_TILE, issue, 0, unroll=4)

    def drain(r, _):
        for k in range(TOP_K):
            copy(r, dest_ref[r * TOP_K + k]).wait()
        return 0

    lax.fori_loop(0, ROW_TILE, drain, 0, unroll=4)


def moe_dispatch(h2, dest_flat):
    return pl.pallas_call(
        _dispatch_kernel,
        grid=(T_ALL // ROW_TILE,),
        in_specs=[pl.BlockSpec((ROW_TILE * TOP_K,), lambda i: (i,), memory_space=pltpu.SMEM),
                  pl.BlockSpec((ROW_TILE, D), lambda i: (i, 0))],
        out_specs=pl.BlockSpec(memory_space=pl.ANY),
        out_shape=jax.ShapeDtypeStruct((N_ASSIGN, D), F32),
        scratch_shapes=[pltpu.SemaphoreType.DMA],
        compiler_params=_cparams(("arbitrary",)),
        name="moe_dispatch",
    )(dest_flat, h2)


def _moe_ffn_kernel(blk_ref, exp_ref, lo_ref, hi_ref, first_ref, x_ref, w1_ref, b1_ref, w2_ref, b2_ref, o_ref):
    i = pl.program_id(0)
    lo = lo_ref[i]
    hi = hi_ref[i]

    @pl.when(first_ref[i] == 1)
    def _():
        o_ref[...] = jnp.zeros_like(o_ref)

    @pl.when(hi > lo)
    def _():
        x = x_ref[...].astype(BF16)
        h = jnp.dot(x, w1_ref[...], preferred_element_type=F32) + b1_ref[...]
        g = jnp.minimum(h[:, :D_FF], SWIGLU_LIMIT)
        u = jnp.clip(h[:, D_FF:], -SWIGLU_LIMIT, SWIGLU_LIMIT)
        a = g * jax.nn.sigmoid(SWIGLU_ALPHA * g) * (u + 1.0)
        y = jnp.dot(a.astype(BF16), w2_ref[...], preferred_element_type=F32) + b2_ref[...]
        row = lax.broadcasted_iota(jnp.int32, (MOE_TILE, 1), 0)
        o_ref[...] += jnp.where((row >= lo) & (row < hi), y, 0.0)


def moe_ffn(items, xs, w1, b1, w2, b2):
    blk, exp, lo, hi, first = items
    grid_spec = pltpu.PrefetchScalarGridSpec(
        num_scalar_prefetch=5,
        grid=(N_MOE_ITEMS,),
        in_specs=[
            pl.BlockSpec((MOE_TILE, D), lambda i, blk, exp, lo, hi, first: (blk[i], 0)),
            pl.BlockSpec((None, D, 2 * D_FF), lambda i, blk, exp, lo, hi, first: (exp[i], 0, 0)),
            pl.BlockSpec((None, 1, 2 * D_FF), lambda i, blk, exp, lo, hi, first: (exp[i], 0, 0)),
            pl.BlockSpec((None, D_FF, D), lambda i, blk, exp, lo, hi, first: (exp[i], 0, 0)),
            pl.BlockSpec((None, 1, D), lambda i, blk, exp, lo, hi, first: (exp[i], 0, 0)),
        ],
        out_specs=pl.BlockSpec((MOE_TILE, D), lambda i, blk, exp, lo, hi, first: (blk[i], 0)),
    )
    return pl.pallas_call(
        _moe_ffn_kernel,
        grid_spec=grid_spec,
        out_shape=jax.ShapeDtypeStruct((N_ASSIGN, D), F32),
        compiler_params=_cparams(("arbitrary",)),
        name="moe_ffn",
    )(blk, exp, lo, hi, first, xs, w1, b1, w2, b2)


def _combine_kernel(dest_ref, ys_ref, gates_ref, x_ref, mod_ref, g_ref, b_ref, o_ref, buf_ref, sem):
    def copy(r, k, d):
        return pltpu.make_async_copy(ys_ref.at[pl.ds(d, 1)], buf_ref.at[k, pl.ds(r, 1)], sem)

    def issue(r, _):
        for k in range(TOP_K):
            copy(r, k, dest_ref[r * TOP_K + k]).start()
        return 0

    lax.fori_loop(0, COMB_TILE, issue, 0, unroll=4)

    def drain(r, _):
        for k in range(TOP_K):
            copy(r, k, dest_ref[r * TOP_K + k]).wait()
        return 0

    lax.fori_loop(0, COMB_TILE, drain, 0, unroll=4)

    gates = gates_ref[...]
    y = gates[:, 0:1] * buf_ref[0]
    for k in range(1, TOP_K):
        y = y + gates[:, k:k + 1] * buf_ref[k]
    m = mod_ref[...]
    o_ref[...] = _layer_norm(DN_ALPHA * x_ref[...] + m[5:6] * y, g_ref[...], b_ref[...])


def moe_combine(ys, dest_flat, gates, x1, mod, ln_g, ln_b):
    full = lambda shp: pl.BlockSpec(shp, lambda i: (0, 0))
    return pl.pallas_call(
        _combine_kernel,
        grid=(T_ALL // COMB_TILE,),
        in_specs=[pl.BlockSpec((COMB_TILE * TOP_K,), lambda i: (i,), memory_space=pltpu.SMEM),
                  pl.BlockSpec(memory_space=pl.ANY),
                  pl.BlockSpec((COMB_TILE, TOP_K), lambda i: (i, 0)),
                  pl.BlockSpec((COMB_TILE, D), lambda i: (i, 0)),
                  pl.BlockSpec((None, 8, D), lambda i: (_mod_index(i, COMB_TILE), 0, 0)),
                  full((1, D)), full((1, D))],
        out_specs=pl.BlockSpec((COMB_TILE, D), lambda i: (i, 0)),
        out_shape=jax.ShapeDtypeStruct((T_ALL, D), F32),
        scratch_shapes=[pltpu.VMEM((TOP_K, COMB_TILE, D), F32), pltpu.SemaphoreType.DMA],
        compiler_params=_cparams(("arbitrary",)),
        name="moe_combine",
    )(dest_flat, ys, gates, x1, mod, ln_g, ln_b)


def _routing_tables(gate_dense, rank_dense):
    rank = rank_dense[:, :N_EXPERTS]
    gate = gate_dense[:, :N_EXPERTS]
    sel = (rank > 0).astype(jnp.int32)
    csum = jnp.cumsum(sel, axis=0)
    counts = csum[-1]
    starts = jnp.cumsum(counts) - counts
    slot = starts[None, :] + csum - sel
    picks = [rank == k + 1 for k in range(TOP_K)]
    dest = jnp.stack([jnp.sum(jnp.where(p, slot, 0), axis=1) for p in picks], axis=1)
    gates = jnp.stack([jnp.sum(jnp.where(p, gate, 0.0), axis=1) for p in picks], axis=1)
    bnd = jnp.sort(jnp.concatenate([jnp.arange(N_MOE_BLOCKS + 1, dtype=jnp.int32) * MOE_TILE,
                                    starts[1:].astype(jnp.int32)]))
    a, b = bnd[:-1], bnd[1:]
    blk = jnp.minimum(a // MOE_TILE, N_MOE_BLOCKS - 1)
    lo = a - blk * MOE_TILE
    hi = b - blk * MOE_TILE
    exp = jnp.clip(jnp.searchsorted(starts, a, side="right") - 1, 0, N_EXPERTS - 1)
    first = jnp.concatenate([jnp.ones((1,), jnp.int32), (blk[1:] != blk[:-1]).astype(jnp.int32)])
    items = tuple(v.astype(jnp.int32) for v in (blk, exp, lo, hi, first))
    return dest.astype(jnp.int32).reshape(-1), gates, items


def moe_layer(h2, gate_dense, rank_dense, x1, mod, ln_g, ln_b, w1, b1, w2, b2):
    dest_flat, gates, items = _routing_tables(gate_dense, rank_dense)
    xs = moe_dispatch(h2, dest_flat)
    ys = moe_ffn(items, xs, w1, b1, w2, b2)
    return moe_combine(ys, dest_flat, gates, x1, mod, ln_g, ln_b)


def _router_operands(router_w, router_b):
    w = jnp.pad(router_w, ((0, 0), (0, 128 - N_EXPERTS)))
    hi = w.astype(BF16)
    lo = (w - hi.astype(F32)).astype(BF16)
    b = jnp.concatenate([router_b, jnp.full((128 - N_EXPERTS,), -jnp.inf, F32)])[None]
    return hi, lo, b


def kernel(x_prompt, x_sample, state_s5, cache_na_k, cache_na_v, c, c_ctx, w_mod, b_mod, ln1_g, ln1_b, ln2_g, ln2_b, ev_w_in, ev_w_out, hy_conv_w, hy_conv_b, hy_f_w1, hy_f_b1, hy_f_w2, hy_f_b2, hy_f_w3, hy_f_freq, hy_bias, s5_lam_re, s5_lam_im, s5_log_dt, s5_b_re, s5_b_im, s5_c_re, s5_c_im, s5_d, s5_glu_w, s5_glu_b, od_w_in, od_w_out, na_rpb, router_w, router_b, moe_w1, moe_b1, moe_w2, moe_b2):
    x = jnp.concatenate([x_prompt.reshape(T_CTX, D), x_sample.reshape(T_LAT, D)], axis=0)
    cvec = jnp.pad(jnp.concatenate([c_ctx[None], c], axis=0), ((0, 5), (0, 0)))

    new_state = None
    new_k = new_v = None
    for l in range(DEPTH):
        i = l // 2
        mod = matmul(cvec, w_mod[l], b_mod[l][None], tm=8, tn=512, tk=D, precise=True, silu_a=True)
        mod = jnp.pad(mod.reshape(8, 6, D)[:3], ((0, 0), (0, 2), (0, 0)))

        if l % 2 == 0:
            u = modlinear(x, mod, ev_w_in[i].astype(BF16))
            hy_parts = []
            for n_seq, seq_len, row_off, tile in ((N_CTX_SEQ, CTX_LEN, 0, CTX_LEN), (N_LAT_SEQ, LAT_LEN, T_CTX, 1024)):
                taps = hyena_filter_taps(seq_len, hy_f_w1[i], hy_f_b1[i], hy_f_w2[i], hy_f_b2[i], hy_f_w3[i],
                                         hy_f_freq[i])
                hy_parts.append(hyena_group(u, hy_conv_w[i], hy_conv_b[i][None], taps, hy_bias[i][None],
                                            n_seq=n_seq, seq_len=seq_len, row_off=row_off, tile=tile))
            y_hy = jnp.concatenate(hy_parts, axis=0)

            u_s5 = u[:, 3 * HY_CH:]
            h0_lat = state_s5[:, i]
            yf_parts, yb_parts, finals = [], [], []
            for r in range(2):
                tables = _s5_tables(s5_lam_re[i, r], s5_lam_im[i, r], s5_log_dt[i, r], s5_b_re[i, r], s5_b_im[i, r],
                                    s5_c_re[i, r], s5_c_im[i, r], reverse=(r == 1))
                h0_ctx = jnp.zeros((N_CTX_SEQ, 1, 2 * S5_NS), F32)
                h0_l = jnp.concatenate([h0_lat[:, r, :, :, 0].reshape(N_LAT_SEQ, 1, S5_NS),
                                        h0_lat[:, r, :, :, 1].reshape(N_LAT_SEQ, 1, S5_NS)], axis=-1)
                y_c, fin_c = s5_direction(_interleave(u_s5[:T_CTX], N_CTX_SEQ, CTX_LEN), tables, h0_ctx,
                                          n_seq=N_CTX_SEQ, seq_len=CTX_LEN, reverse=(r == 1))
                y_l, _ = s5_direction(_interleave(u_s5[T_CTX:], N_LAT_SEQ, LAT_LEN), tables, h0_l,
                                      n_seq=N_LAT_SEQ, seq_len=LAT_LEN, reverse=(r == 1))
                y_dir = jnp.concatenate([_deinterleave(y_c, N_CTX_SEQ, CTX_LEN),
                                         _deinterleave(y_l, N_LAT_SEQ, LAT_LEN)], axis=0)
                (yf_parts if r == 0 else yb_parts).append(y_dir)
                fin = fin_c.reshape(N_CTX_SEQ, 2, S5_GROUPS, S5_STATE)
                finals.append(jnp.stack([fin[:, 0], fin[:, 1]], axis=-1))
            new_state = jnp.stack(finals, axis=1)[:, None]
            y_s5 = s5_post(u, yf_parts[0], yb_parts[0], s5_d[i].reshape(1, S5_CH), s5_glu_w[i].astype(BF16),
                           s5_glu_b[i][None])
            acts = [y_hy, y_s5]
            w_out = ev_w_out[i].astype(BF16)
            weights = [w_out[:HY_CH], w_out[HY_CH:]]
        else:
            qkv = modlinear(x, mod, od_w_in[i].astype(BF16))
            new_k = qkv[:T_CTX, D:2 * D].reshape(N_CTX_SEQ, 1, CTX_LEN, N_HEADS, HEAD_DIM)
            new_v = qkv[:T_CTX, 2 * D:].reshape(N_CTX_SEQ, 1, CTX_LEN, N_HEADS, HEAD_DIM)
            attn_c = ctx_attention(qkv)
            kvb = qkv[T_CTX:, D:].astype(BF16)
            kcb = cache_na_k[:, i].reshape(N_LAT_SEQ * CTX_LEN, D).astype(BF16)
            vcb = cache_na_v[:, i].reshape(N_LAT_SEQ * CTX_LEN, D).astype(BF16)
            attn_l = na_attention(qkv, kvb, kcb, vcb, _na_bias_table(na_rpb[i]))
            acts = [jnp.concatenate([attn_c, attn_l], axis=0)]
            weights = [od_w_out[i].astype(BF16)]

        rw_hi, rw_lo, rb = _router_operands(router_w[l], router_b[l])
        x1, h2, gate_dense, rank_dense = proj_res_ln(acts, weights, x, mod, ln1_g[l][None], ln1_b[l][None],
                                                     rw_hi, rw_lo, rb)
        x = moe_layer(h2, gate_dense, rank_dense, x1, mod, ln2_g[l][None], ln2_b[l][None],
                      moe_w1[l].astype(BF16), moe_b1[l][:, None, :], moe_w2[l].astype(BF16), moe_b2[l][:, None, :])

    y_prompt = x[:T_CTX].reshape(N_CTX_SEQ, CTX_LEN, D)
    y_sample = x[T_CTX:].reshape(N_LAT_SEQ, LAT_LEN, D)
    return (y_prompt, y_sample, new_state, new_k, new_v)
```

```python
import functools
import math

import jax
import jax.numpy as jnp
import numpy as np
from jax import lax
from jax.experimental import pallas as pl
from jax.experimental.pallas import tpu as pltpu

F32 = jnp.float32
BF16 = jnp.bfloat16
HIGHEST = lax.Precision.HIGHEST

D = 1024
N_CTX_SEQ, CTX_LEN = 32, 256
N_LAT_SEQ, LAT_LEN = 2, 4096
T_CTX = N_CTX_SEQ * CTX_LEN
T_LAT = N_LAT_SEQ * LAT_LEN
T_ALL = T_CTX + T_LAT
DEPTH = 2
HY_CH = 512
S5_CH = 512
S5_GROUPS, S5_GROUP, S5_STATE = 32, 16, 64
S5_NS = S5_GROUPS * S5_STATE
HY_EMB, HY_BANDS, HY_ORDER = 33, 16, 64
N_HEADS, HEAD_DIM = 16, 64
GRID_W, WIN_R, WIN_C = 64, 8, 16
N_EXPERTS, TOP_K, D_FF = 32, 4, 1024
SWIGLU_LIMIT, SWIGLU_ALPHA = 7.0, 1.702
LN_EPS = 1e-5
DN_ALPHA = (2 * DEPTH) ** 0.25
NEG_BIG = -1e30

ROW_TILE = 256
MOE_TILE = 256
N_ASSIGN = T_ALL * TOP_K
N_MOE_BLOCKS = N_ASSIGN // MOE_TILE
N_MOE_ITEMS = N_MOE_BLOCKS + N_EXPERTS - 1
COMB_TILE = 128
S5_CHUNK = 256
S5_SEG = S5_CHUNK // 8
S5_COLS = 512
VMEM_LIMIT = 56 * 1024 * 1024


def _cparams(sem, vmem=None):
    return pltpu.CompilerParams(dimension_semantics=sem, vmem_limit_bytes=vmem or VMEM_LIMIT)


def _mod_index(i, rows_per_tile):
    n_ctx = T_CTX // rows_per_tile
    per_lat = LAT_LEN // rows_per_tile
    return jnp.where(i < n_ctx, 0, 1 + (i - n_ctx) // per_lat)


def _mm_kernel(a_ref, b_ref, bias_ref, o_ref, acc_ref, *, nk, precise, silu_a):
    k = pl.program_id(3)

    @pl.when(k == 0)
    def _():
        acc_ref[...] = jnp.zeros_like(acc_ref)

    a = a_ref[...]
    if silu_a:
        a = a * jax.nn.sigmoid(a)
    if precise:
        acc_ref[...] += jnp.dot(a, b_ref[...], preferred_element_type=F32, precision=HIGHEST)
    else:
        acc_ref[...] += jnp.dot(a.astype(BF16), b_ref[...].astype(BF16), preferred_element_type=F32)

    @pl.when(k == nk - 1)
    def _():
        o_ref[...] = acc_ref[...] + bias_ref[...]


def matmul(a, b, bias=None, *, tm, tn, tk, precise=False, silu_a=False):
    squeeze = b.ndim == 2
    if squeeze:
        b = b[None]
    nb, kdim, n = b.shape
    m = a.shape[0]
    if bias is None:
        bias = jnp.zeros((1, n), F32)
    nk = kdim // tk
    out = pl.pallas_call(
        functools.partial(_mm_kernel, nk=nk, precise=precise, silu_a=silu_a),
        grid=(nb, m // tm, n // tn, nk),
        in_specs=[
            pl.BlockSpec((tm, tk), lambda s, i, j, k: (i, k)),
            pl.BlockSpec((None, tk, tn), lambda s, i, j, k: (s, k, j)),
            pl.BlockSpec((1, tn), lambda s, i, j, k: (0, j)),
        ],
        out_specs=pl.BlockSpec((None, tm, tn), lambda s, i, j, k: (s, i, j)),
        out_shape=jax.ShapeDtypeStruct((nb, m, n), F32),
        scratch_shapes=[pltpu.VMEM((tm, tn), F32)],
        compiler_params=_cparams(("parallel", "parallel", "parallel", "arbitrary")),
        name="matmul",
    )(a, b, bias)
    return out[0] if squeeze else out


def _modlinear_kernel(x_ref, mod_ref, w_ref, o_ref):
    m = mod_ref[...]
    h = x_ref[...] * (1.0 + m[1:2]) + m[0:1]
    o_ref[...] = jnp.dot(h.astype(BF16), w_ref[...], preferred_element_type=F32)


def modlinear(x, mod, w_bf16):
    n = w_bf16.shape[1]
    return pl.pallas_call(
        _modlinear_kernel,
        grid=(T_ALL // ROW_TILE,),
        in_specs=[
            pl.BlockSpec((ROW_TILE, D), lambda i: (i, 0)),
            pl.BlockSpec((None, 8, D), lambda i: (_mod_index(i, ROW_TILE), 0, 0)),
            pl.BlockSpec((D, n), lambda i: (0, 0)),
        ],
        out_specs=pl.BlockSpec((ROW_TILE, n), lambda i: (i, 0)),
        out_shape=jax.ShapeDtypeStruct((T_ALL, n), F32),
        compiler_params=_cparams(("parallel",)),
        name="modlinear",
    )(x, mod, w_bf16)


def _filter_kernel(z_ref, w1_ref, b1_ref, w2_ref, b2_ref, w3_ref, fq_ref, dl_ref, hsum_ref, hdiff_ref, *, tile):
    z = z_ref[...]
    fq = fq_ref[...]
    h = jnp.sin(fq * (jnp.dot(z, w1_ref[...], preferred_element_type=F32, precision=HIGHEST) + b1_ref[...]))
    h = jnp.sin(fq * (jnp.dot(h, w2_ref[...], preferred_element_type=F32, precision=HIGHEST) + b2_ref[...]))
    h = jnp.dot(h, w3_ref[...], preferred_element_type=F32, precision=HIGHEST)
    decay = jnp.exp(-z[:, 0:1] * dl_ref[...])
    hf = h[:, :HY_CH] * decay
    hb = h[:, HY_CH:] * decay
    row = lax.broadcasted_iota(jnp.int32, (tile, 1), 0) + pl.program_id(0) * tile
    hsum_ref[...] = hf + hb
    hdiff_ref[...] = jnp.where(row == 0, hf + hb, hf - hb)


def hyena_filter_taps(seq_len, w1, b1, w2, b2, w3, freq):
    t = jnp.linspace(0.0, 1.0, seq_len, dtype=F32)[:, None]
    w = 2.0 * math.pi * jnp.arange(seq_len, dtype=F32)[:, None] / seq_len
    f = jnp.linspace(1e-4, HY_BANDS - 1, HY_BANDS, dtype=F32)[None, :]
    z = jnp.concatenate([t, jnp.cos(f * w), -jnp.sin(f * w)], -1)
    z = jnp.pad(z, ((0, 0), (0, 128 - HY_EMB)))
    pad_o = 128 - HY_ORDER
    w1p = jnp.pad(w1, ((0, 128 - HY_EMB), (0, pad_o)))
    w2p = jnp.pad(w2, ((0, pad_o), (0, pad_o)))
    w3p = jnp.pad(w3, ((0, pad_o), (0, 0)))
    b1p = jnp.pad(b1, (0, pad_o))[None]
    b2p = jnp.pad(b2, (0, pad_o))[None]
    fqp = jnp.pad(freq, (0, pad_o))[None]
    max_decay = math.log(1e-2) / 0.3
    min_decay = math.log(1e-2) / 1.5
    absdelta = jnp.abs(jnp.linspace(min_decay, max_decay, HY_CH, dtype=F32))[None]
    tile = 256
    full = lambda shp: pl.BlockSpec(shp, lambda i: (0, 0))
    return pl.pallas_call(
        functools.partial(_filter_kernel, tile=tile),
        grid=(seq_len // tile,),
        in_specs=[pl.BlockSpec((tile, 128), lambda i: (i, 0)), full((128, 128)), full((1, 128)), full((128, 128)),
                  full((1, 128)), full((128, 2 * HY_CH)), full((1, 128)), full((1, HY_CH))],
        out_specs=[pl.BlockSpec((tile, HY_CH), lambda i: (i, 0))] * 2,
        out_shape=[jax.ShapeDtypeStruct((seq_len, HY_CH), F32)] * 2,
        compiler_params=_cparams(("parallel",)),
        name="hyena_filter",
    )(z, w1p, b1p, w2p, b2p, w3p, fqp, absdelta)


def _dft_tables(seq_len):
    s = int(round(math.sqrt(seq_len)))
    k = jnp.arange(seq_len, dtype=jnp.int32)[:, None]
    j = jnp.arange(s, dtype=jnp.int32)[None, :]
    unit = math.pi / (2 * seq_len)

    def cos_sin(idx):
        ang = (idx % (4 * seq_len)).astype(F32) * unit
        return jnp.cos(ang), jnp.sin(ang)

    def expand(ca, sa, cb, sb):
        cos = ca[:, :, None] * cb[:, None, :] - sa[:, :, None] * sb[:, None, :]
        nsin = -(sa[:, :, None] * cb[:, None, :] + ca[:, :, None] * sb[:, None, :])
        return cos.reshape(seq_len, seq_len).astype(BF16), nsin.reshape(seq_len, seq_len).astype(BF16)

    cm, nsm = expand(*cos_sin((2 * k + 1) * (s * j)), *cos_sin((2 * k + 1) * j))
    cmt, nsmt = expand(*cos_sin((2 * s * j) * k), *cos_sin((2 * j + 1) * k))
    return cm, nsm, cmt, nsmt


def _hyena_pre_kernel(u0_ref, u1_ref, u2_ref, w0_ref, w1_ref, w2_ref, b0_ref, b1_ref, b2_ref,
                      vx_ref, vxb_ref, x0_ref, *, seq_len):
    row = lax.broadcasted_iota(jnp.int32, (seq_len, 1), 0)

    def short_conv(u_ref, w_ref, b_ref):
        a = u_ref[...]
        w = w_ref[...]
        prev = jnp.where(row == 0, 0.0, pltpu.roll(a, 1, 0))
        nxt = jnp.where(row == seq_len - 1, 0.0, pltpu.roll(a, seq_len - 1, 0))
        return prev * w[0:1] + a * w[1:2] + nxt * w[2:3] + b_ref[...]

    x0 = short_conv(u0_ref, w0_ref, b0_ref)
    x1 = short_conv(u1_ref, w1_ref, b1_ref)
    v = short_conv(u2_ref, w2_ref, b2_ref)
    vx = v * x1
    vx_ref[...] = vx
    vxb_ref[...] = vx.astype(BF16)
    x0_ref[...] = x0


def hyena_pre(u, conv_w, conv_b, *, n_seq, seq_len, row_off):
    cb = 128
    ncb = HY_CH // cb
    rb0 = row_off // seq_len
    uspec = lambda part: pl.BlockSpec((seq_len, cb), lambda b, j: (rb0 + b, part * ncb + j))
    wspec = lambda part: pl.BlockSpec((3, cb), lambda b, j: (0, part * ncb + j))
    bspec = lambda part: pl.BlockSpec((1, cb), lambda b, j: (0, part * ncb + j))
    ospec = pl.BlockSpec((seq_len, cb), lambda b, j: (b, j))
    rows = n_seq * seq_len
    return pl.pallas_call(
        functools.partial(_hyena_pre_kernel, seq_len=seq_len),
        grid=(n_seq, ncb),
        in_specs=[uspec(0), uspec(1), uspec(2), wspec(0), wspec(1), wspec(2), bspec(0), bspec(1), bspec(2)],
        out_specs=[ospec, ospec, ospec],
        out_shape=[jax.ShapeDtypeStruct((rows, HY_CH), F32), jax.ShapeDtypeStruct((rows, HY_CH), BF16),
                   jax.ShapeDtypeStruct((rows, HY_CH), F32)],
        compiler_params=_cparams(("parallel", "parallel")),
        name="hyena_pre",
    )(u, u, u, conv_w, conv_w, conv_w, conv_b, conv_b, conv_b)


def _dft_fwd_kernel(cm_ref, nsm_ref, v_ref, hr_ref, hi_ref, zr_ref, zi_ref, accr_ref, acci_ref, *, nk):
    k = pl.program_id(3)

    @pl.when(k == 0)
    def _():
        accr_ref[...] = jnp.zeros_like(accr_ref)
        acci_ref[...] = jnp.zeros_like(acci_ref)

    v = v_ref[...]
    accr_ref[...] += jnp.dot(cm_ref[...], v, preferred_element_type=F32)
    acci_ref[...] += jnp.dot(nsm_ref[...], v, preferred_element_type=F32)

    @pl.when(k == nk - 1)
    def _():
        xr, xi = accr_ref[...], acci_ref[...]
        hr, hi = hr_ref[...], hi_ref[...]
        zr_ref[...] = (xr * hr - xi * hi).astype(BF16)
        zi_ref[...] = (xr * hi + xi * hr).astype(BF16)


def dft_fwd(cm, nsm, vxb, hr, hi, *, n_seq, seq_len, tile):
    nk = seq_len // tile
    tn = HY_CH
    v3 = vxb.reshape(n_seq, seq_len, HY_CH)
    zspec = pl.BlockSpec((None, tile, tn), lambda s, i, j, k: (s, i, j))
    return pl.pallas_call(
        functools.partial(_dft_fwd_kernel, nk=nk),
        grid=(n_seq, seq_len // tile, HY_CH // tn, nk),
        in_specs=[
            pl.BlockSpec((tile, tile), lambda s, i, j, k: (i, k)),
            pl.BlockSpec((tile, tile), lambda s, i, j, k: (i, k)),
            pl.BlockSpec((None, tile, tn), lambda s, i, j, k: (s, k, j)),
            pl.BlockSpec((tile, tn), lambda s, i, j, k: (i, j)),
            pl.BlockSpec((tile, tn), lambda s, i, j, k: (i, j)),
        ],
        out_specs=[zspec, zspec],
        out_shape=[jax.ShapeDtypeStruct((n_seq, seq_len, HY_CH), BF16)] * 2,
        scratch_shapes=[pltpu.VMEM((tile, tn), F32), pltpu.VMEM((tile, tn), F32)],
        compiler_params=_cparams(("parallel", "parallel", "parallel", "arbitrary")),
        name="dft_fwd",
    )(cm, nsm, v3, hr, hi)


def _dft_inv_kernel(cmt_ref, nsmt_ref, zr_ref, zi_ref, vx_ref, x0_ref, bias_ref, prev_ref, o_ref, acc_ref, *,
                    nk, inv_len):
    del prev_ref
    k = pl.program_id(3)

    @pl.when(k == 0)
    def _():
        acc_ref[...] = jnp.zeros_like(acc_ref)

    acc_ref[...] += (jnp.dot(cmt_ref[...], zr_ref[...], preferred_element_type=F32)
                     + jnp.dot(nsmt_ref[...], zi_ref[...], preferred_element_type=F32))

    @pl.when(k == nk - 1)
    def _():
        conv = acc_ref[...] * inv_len
        o_ref[...] = (conv + vx_ref[...] * bias_ref[...]) * x0_ref[...]


def dft_inv(cmt, nsmt, zr, zi, vx, x0, bias, y_prev, *, n_seq, seq_len, row_off, tile):
    nk = seq_len // tile
    tn = HY_CH
    per_seq = seq_len // tile
    rb0 = row_off // tile
    zspec = pl.BlockSpec((None, tile, tn), lambda s, i, j, k: (s, k, j))
    espec = pl.BlockSpec((None, tile, tn), lambda s, i, j, k: (s, i, j))
    return pl.pallas_call(
        functools.partial(_dft_inv_kernel, nk=nk, inv_len=1.0 / seq_len),
        grid=(n_seq, per_seq, HY_CH // tn, nk),
        in_specs=[
            pl.BlockSpec((tile, tile), lambda s, i, j, k: (i, k)),
            pl.BlockSpec((tile, tile), lambda s, i, j, k: (i, k)),
            zspec, zspec, espec, espec,
            pl.BlockSpec((1, tn), lambda s, i, j, k: (0, j)),
            pl.BlockSpec(memory_space=pl.ANY),
        ],
        out_specs=pl.BlockSpec((tile, tn), lambda s, i, j, k: (rb0 + s * per_seq + i, j)),
        out_shape=jax.ShapeDtypeStruct((T_ALL, HY_CH), F32),
        scratch_shapes=[pltpu.VMEM((tile, tn), F32)],
        input_output_aliases={7: 0},
        compiler_params=_cparams(("parallel", "parallel", "parallel", "arbitrary")),
        name="dft_inv",
    )(cmt, nsmt, zr, zi, vx.reshape(n_seq, seq_len, HY_CH), x0.reshape(n_seq, seq_len, HY_CH), bias, y_prev)


def hyena_group(u, conv_w, conv_b, taps, bias, y_prev, *, n_seq, seq_len, row_off, tile):
    hsum, hdiff = taps
    cm, nsm, cmt, nsmt = _dft_tables(seq_len)
    hr = matmul(cm, hsum, tm=tile, tn=HY_CH, tk=tile)
    hi = matmul(nsm, hdiff, tm=tile, tn=HY_CH, tk=tile)
    vx, vxb, x0 = hyena_pre(u, conv_w, conv_b, n_seq=n_seq, seq_len=seq_len, row_off=row_off)
    zr, zi = dft_fwd(cm, nsm, vxb, hr, hi, n_seq=n_seq, seq_len=seq_len, tile=tile)
    return dft_inv(cmt, nsmt, zr, zi, vx, x0, bias, y_prev, n_seq=n_seq, seq_len=seq_len, row_off=row_off, tile=tile)


def _s5_tables(lam_re, lam_im, log_dt, b_re, b_im, c_re, c_im, reverse):
    dt = jnp.exp(log_dt)[:, None]
    a = lam_re * dt
    b = lam_im * dt
    mag = jnp.exp(a)
    lbr, lbi = mag * jnp.cos(b), mag * jnp.sin(b)
    den = lam_re * lam_re + lam_im * lam_im
    qr = ((lbr - 1.0) * lam_re + lbi * lam_im) / den
    qi = (lbi * lam_re - (lbr - 1.0) * lam_im) / den
    bbr = qr[..., None] * b_re - qi[..., None] * b_im
    bbi = qr[..., None] * b_im + qi[..., None] * b_re
    eye = jnp.eye(S5_GROUPS, dtype=F32)

    def in_block(m):
        return jnp.einsum("gnc,gh->gchn", m, eye).reshape(S5_CH, S5_NS)

    def out_block(m):
        return jnp.einsum("gcn,gh->gnhc", m, eye).reshape(S5_NS, S5_CH)

    b_blk = jnp.concatenate([in_block(bbr), in_block(bbi)], axis=1).astype(BF16)
    c_blk = jnp.concatenate([out_block(c_re), out_block(-c_im)], axis=0).astype(BF16)
    flat = lambda m: m.reshape(1, S5_NS)
    lam = jnp.concatenate([flat(lbr), flat(lbi)], axis=1)
    steps = jnp.arange(1, S5_SEG + 1, dtype=F32)
    if reverse:
        steps = steps[::-1]
    pa = a.reshape(1, S5_NS) * steps[:, None]
    pb = b.reshape(1, S5_NS) * steps[:, None]
    pw = jnp.concatenate([jnp.exp(pa) * jnp.cos(pb), jnp.exp(pa) * jnp.sin(pb)], axis=1)
    sa, sb = flat(a) * S5_SEG, flat(b) * S5_SEG
    lam_seg = jnp.concatenate([jnp.exp(sa) * jnp.cos(sb), jnp.exp(sa) * jnp.sin(sb)], axis=1)
    return b_blk, c_blk, lam, lam_seg, pw


N_S5_CTX_STEPS = T_CTX // S5_CHUNK
N_S5_LAT_CHUNKS = LAT_LEN // S5_CHUNK
N_S5_STEPS = T_ALL // S5_CHUNK


def _s5_step_info(i, reverse):
    k = jnp.maximum(i - N_S5_CTX_STEPS, 0)
    b, j = k // N_S5_LAT_CHUNKS, k % N_S5_LAT_CHUNKS
    chunk = (N_S5_LAT_CHUNKS - 1 - j) if reverse else j
    is_ctx = i < N_S5_CTX_STEPS
    row_blk = jnp.where(is_ctx, i, N_S5_CTX_STEPS + b * N_S5_LAT_CHUNKS + chunk)
    seq = jnp.where(is_ctx, i, N_CTX_SEQ + b)
    return row_blk, seq, is_ctx | (j == 0), is_ctx | (j == N_S5_LAT_CHUNKS - 1)


def _s5_kernel(u_ref, bblk_ref, cblk_ref, lam_ref, lseg_ref, pw_ref, h0_ref, y_ref, fin_ref,
               bu_ref, hend_ref, cin_ref, carry_ref, *, reverse):
    _, _, is_first, is_last = _s5_step_info(pl.program_id(0), reverse)
    ns = S5_NS

    @pl.when(is_first)
    def _():
        carry_ref[...] = h0_ref[...]

    bu = jnp.dot(u_ref[...].astype(BF16), bblk_ref[...], preferred_element_type=F32)
    n_blk = ns // 128
    for c in range(2 * n_blk):
        bu_ref[c] = bu[:, c * 128:(c + 1) * 128]

    per = S5_COLS // 128
    for cb in range(ns // S5_COLS):
        blocks = [cb * per + q for q in range(per)]
        lr = [jnp.broadcast_to(lam_ref[:, pl.ds(c * 128, 128)], (8, 128)) for c in blocks]
        li = [jnp.broadcast_to(lam_ref[:, pl.ds(ns + c * 128, 128)], (8, 128)) for c in blocks]

        def step(kk, carry, blocks=blocks, lr=lr, li=li):
            k = (S5_SEG - 1 - kk) if reverse else kk
            rows = pl.ds(k, 8, stride=S5_SEG)
            out = []
            for q, c in enumerate(blocks):
                hr, hi = carry[q]
                nr = lr[q] * hr - li[q] * hi + bu_ref[c, rows, :]
                ni = lr[q] * hi + li[q] * hr + bu_ref[n_blk + c, rows, :]
                bu_ref[c, rows, :] = nr
                bu_ref[n_blk + c, rows, :] = ni
                out.append((nr, ni))
            return tuple(out)

        zero = jnp.zeros((8, 128), F32)
        fin = lax.fori_loop(0, S5_SEG, step, tuple((zero, zero) for _ in blocks), unroll=4)
        for q, c in enumerate(blocks):
            hend_ref[:, pl.ds(c * 128, 128)] = fin[q][0]
            hend_ref[:, pl.ds(ns + c * 128, 128)] = fin[q][1]

    cr = carry_ref[:, :ns]
    ci = carry_ref[:, ns:]
    lsr = lseg_ref[:, :ns]
    lsi = lseg_ref[:, ns:]
    for p in range(8):
        s = 7 - p if reverse else p
        cin_ref[s:s + 1, :ns] = cr
        cin_ref[s:s + 1, ns:] = ci
        er = hend_ref[s:s + 1, :ns]
        ei = hend_ref[s:s + 1, ns:]
        cr, ci = er + lsr * cr - lsi * ci, ei + lsr * ci + lsi * cr
    carry_ref[:, :ns] = cr
    carry_ref[:, ns:] = ci

    def fix(c, _):
        re_cols = pl.ds(pl.multiple_of(c * 128, 128), 128)
        im_cols = pl.ds(pl.multiple_of(ns + c * 128, 128), 128)
        for s in range(8):
            cinr = cin_ref[s:s + 1, re_cols]
            cini = cin_ref[s:s + 1, im_cols]
            for k8 in range(0, S5_SEG, 8):
                rows = pl.ds(s * S5_SEG + k8, 8)
                pr = pw_ref[pl.ds(k8, 8), re_cols]
                pi = pw_ref[pl.ds(k8, 8), im_cols]
                bu_ref[c, rows, :] += pr * cinr - pi * cini
                bu_ref[n_blk + c, rows, :] += pr * cini + pi * cinr
        return 0

    lax.fori_loop(0, n_blk, fix, 0)

    h_all = jnp.concatenate([bu_ref[c].astype(BF16) for c in range(2 * n_blk)], axis=1)
    y_ref[...] = jnp.dot(h_all, cblk_ref[...], preferred_element_type=F32)

    @pl.when(is_last)
    def _():
        fin_ref[...] = carry_ref[...]


def s5_direction(u, tables, h0, *, reverse):
    b_blk, c_blk, lam, lam_seg, pw = tables
    s5_col = 3 * HY_CH // S5_CH
    n_seq = N_CTX_SEQ + N_LAT_SEQ
    row_blk = lambda i: _s5_step_info(i, reverse)[0]
    seq = lambda i: _s5_step_info(i, reverse)[1]
    const = lambda shp: pl.BlockSpec(shp, lambda i: (0, 0))
    return pl.pallas_call(
        functools.partial(_s5_kernel, reverse=reverse),
        grid=(N_S5_STEPS,),
        in_specs=[
            pl.BlockSpec((S5_CHUNK, S5_CH), lambda i: (row_blk(i), s5_col)),
            const((S5_CH, 2 * S5_NS)), const((2 * S5_NS, S5_CH)), const((1, 2 * S5_NS)), const((1, 2 * S5_NS)),
            const((S5_SEG, 2 * S5_NS)),
            pl.BlockSpec((None, 1, 2 * S5_NS), lambda i: (seq(i), 0, 0)),
        ],
        out_specs=[pl.BlockSpec((S5_CHUNK, S5_CH), lambda i: (row_blk(i), 0)),
                   pl.BlockSpec((None, 1, 2 * S5_NS), lambda i: (seq(i), 0, 0))],
        out_shape=[jax.ShapeDtypeStruct((T_ALL, S5_CH), F32),
                   jax.ShapeDtypeStruct((n_seq, 1, 2 * S5_NS), F32)],
        scratch_shapes=[pltpu.VMEM((2 * S5_NS // 128, S5_CHUNK, 128), F32), pltpu.VMEM((8, 2 * S5_NS), F32),
                        pltpu.VMEM((8, 2 * S5_NS), F32), pltpu.VMEM((1, 2 * S5_NS), F32)],
        compiler_params=_cparams(("arbitrary",)),
        name="s5_bwd" if reverse else "s5_fwd",
    )(u, b_blk, c_blk, lam, lam_seg, pw, h0)


def _s5_post_kernel(u_ref, yf_ref, yb_ref, d_ref, w_ref, b_ref, o_ref):
    y = d_ref[...] * u_ref[...] + yf_ref[...] + yb_ref[...]
    cdf = 0.5 * (1.0 + jnp.tanh(math.sqrt(2.0 / math.pi) * (y + 0.044715 * (y * y * y))))
    g = y * cdf
    z = jnp.dot(g.astype(BF16), w_ref[...], preferred_element_type=F32) + b_ref[...]
    o_ref[...] = z[:, :S5_CH] * jax.nn.sigmoid(z[:, S5_CH:])


def s5_post(u, yf, yb, d, glu_w, glu_b):
    row = pl.BlockSpec((ROW_TILE, S5_CH), lambda i: (i, 0))
    return pl.pallas_call(
        _s5_post_kernel,
        grid=(T_ALL // ROW_TILE,),
        in_specs=[pl.BlockSpec((ROW_TILE, S5_CH), lambda i: (i, 3 * HY_CH // S5_CH)), row, row,
                  pl.BlockSpec((1, S5_CH), lambda i: (0, 0)),
                  pl.BlockSpec((S5_CH, 2 * S5_CH), lambda i: (0, 0)),
                  pl.BlockSpec((1, 2 * S5_CH), lambda i: (0, 0))],
        out_specs=row,
        out_shape=jax.ShapeDtypeStruct((T_ALL, S5_CH), F32),
        compiler_params=_cparams(("parallel",)),
        name="s5_post",
    )(u, yf, yb, d, glu_w, glu_b)


def _head_mask(shape):
    return lax.broadcasted_iota(jnp.int32, shape, 1) < HEAD_DIM


def _ctx_attn_kernel(q_ref, k_ref, v_ref, prev_ref, o_ref):
    del prev_ref
    low = _head_mask((CTX_LEN, 128))
    for hp in range(N_HEADS // 2):
        cols = pl.ds(hp * 128, 128)
        a = _stack_heads(q_ref[:, cols] * (HEAD_DIM ** -0.5), low)
        k = k_ref[:, cols].astype(BF16)
        v = v_ref[:, cols].astype(BF16)
        s = lax.dot_general(a, k, (((1,), (1,)), ((), ())), preferred_element_type=F32)
        m = jnp.max(s, axis=-1, keepdims=True)
        p = jnp.exp(s - m)
        den = jnp.sum(p, axis=-1, keepdims=True)
        o = jnp.dot(p.astype(BF16), v, preferred_element_type=F32) / den
        o_ref[:, cols] = jnp.where(low, o[:CTX_LEN], o[CTX_LEN:])


def _stack_heads(q, low):
    return jnp.concatenate([jnp.where(low, q, 0.0), jnp.where(low, 0.0, q)], axis=0).astype(BF16)


def ctx_attention(qkv, attn_prev):
    blk = lambda part: pl.BlockSpec((CTX_LEN, D), lambda b: (b, part))
    return pl.pallas_call(
        _ctx_attn_kernel,
        grid=(N_CTX_SEQ,),
        in_specs=[blk(0), blk(1), blk(2), pl.BlockSpec(memory_space=pl.ANY)],
        out_specs=pl.BlockSpec((CTX_LEN, D), lambda b: (b, 0)),
        out_shape=jax.ShapeDtypeStruct((T_ALL, D), F32),
        input_output_aliases={3: 0},
        compiler_params=_cparams(("parallel",)),
        name="ctx_attention",
    )(qkv, qkv, qkv, attn_prev)


NA_ROWS_PER_STEP = 8


def _na_kernel(q_ref, k_ref, v_ref, kc_ref, vc_ref, bias_ref, prev_ref, o_ref):
    del prev_ref
    rb = pl.program_id(2)
    kc = kc_ref[...]
    vc = vc_ref[...]
    low = _head_mask((GRID_W, 128))
    n_rows = LAT_LEN // GRID_W
    for i in range(NA_ROWS_PER_STEP):
        r = rb * NA_ROWS_PER_STEP + i
        rs = jnp.clip(r - WIN_R // 2, 0, n_rows - WIN_R)
        dlt = r - rs
        rows = pl.ds(pl.multiple_of(rs * GRID_W, GRID_W), WIN_R * GRID_W)
        kw = k_ref[rows, :]
        vw = v_ref[rows, :]
        a = _stack_heads(q_ref[i * GRID_W:(i + 1) * GRID_W, :] * (HEAD_DIM ** -0.5), low)
        s = lax.dot_general(a, kw, (((1,), (1,)), ((), ())), preferred_element_type=F32)
        s = s + bias_ref[dlt].reshape(2 * GRID_W, WIN_R * GRID_W)
        sc = lax.dot_general(a, kc, (((1,), (1,)), ((), ())), preferred_element_type=F32)
        m = jnp.maximum(jnp.max(s, axis=-1, keepdims=True), jnp.max(sc, axis=-1, keepdims=True))
        p = jnp.exp(s - m)
        pc = jnp.exp(sc - m)
        den = jnp.sum(p, axis=-1, keepdims=True) + jnp.sum(pc, axis=-1, keepdims=True)
        o = (jnp.dot(p.astype(BF16), vw, preferred_element_type=F32)
             + jnp.dot(pc.astype(BF16), vc, preferred_element_type=F32)) / den
        o_ref[i * GRID_W:(i + 1) * GRID_W, :] = jnp.where(low, o[:GRID_W], o[GRID_W:])


def _na_bias_table(rpb):
    j = np.arange(GRID_W)
    cs = np.clip(j - WIN_C // 2, 0, GRID_W - WIN_C)
    c = np.arange(GRID_W)
    inside = (c[None, :] >= cs[:, None]) & (c[None, :] < cs[:, None] + WIN_C)
    col_rel = c[None, :] - j[:, None] + WIN_C - 1
    n_rel_c = 2 * WIN_C - 1
    onehot = ((np.arange(n_rel_c)[:, None, None] == col_rel[None]) & inside[None]).astype(np.float32)
    t = jnp.dot(rpb.reshape(N_HEADS * (2 * WIN_R - 1), n_rel_c), jnp.asarray(onehot.reshape(n_rel_c, -1)),
                precision=HIGHEST)
    t = t.reshape(N_HEADS, 2 * WIN_R - 1, GRID_W, GRID_W)
    t = t + jnp.asarray(np.where(inside, 0.0, NEG_BIG).astype(np.float32))
    t = jnp.stack([t[:, WIN_R - 1 - dl:2 * WIN_R - 1 - dl] for dl in range(WIN_R)], axis=0)
    t = t.transpose(0, 1, 3, 2, 4)
    return t.reshape(WIN_R, N_HEADS, GRID_W, WIN_R * GRID_W)


def na_attention(qkv, kvb, kcb, vcb, bias, attn_prev):
    hp = N_HEADS // 2
    q_rows = NA_ROWS_PER_STEP * GRID_W
    steps = LAT_LEN // q_rows
    q_off = T_CTX // q_rows
    return pl.pallas_call(
        _na_kernel,
        grid=(N_LAT_SEQ, hp, steps),
        in_specs=[
            pl.BlockSpec((q_rows, 128), lambda b, h, r: (q_off + b * steps + r, h)),
            pl.BlockSpec((LAT_LEN, 128), lambda b, h, r: (b, h)),
            pl.BlockSpec((LAT_LEN, 128), lambda b, h, r: (b, hp + h)),
            pl.BlockSpec((CTX_LEN, 128), lambda b, h, r: (b, h)),
            pl.BlockSpec((CTX_LEN, 128), lambda b, h, r: (b, h)),
            pl.BlockSpec((WIN_R, 2, GRID_W, WIN_R * GRID_W), lambda b, h, r: (0, h, 0, 0)),
            pl.BlockSpec(memory_space=pl.ANY),
        ],
        out_specs=pl.BlockSpec((q_rows, 128), lambda b, h, r: (q_off + b * steps + r, h)),
        out_shape=jax.ShapeDtypeStruct((T_ALL, D), F32),
        input_output_aliases={6: 0},
        compiler_params=_cparams(("parallel", "parallel", "arbitrary")),
        name="na_attention",
    )(qkv, kvb, kvb, kcb, vcb, bias, attn_prev)


def _layer_norm(r, g, b):
    mu = jnp.mean(r, axis=-1, keepdims=True)
    c = r - mu
    var = jnp.mean(c * c, axis=-1, keepdims=True)
    return c * lax.rsqrt(var + LN_EPS) * g + b


def _proj_kernel(*refs, n_in):
    a_refs = refs[:n_in]
    w_refs = refs[n_in:2 * n_in]
    (x_ref, mod_ref, g_ref, b_ref, rwh_ref, rwl_ref, rb_ref, tri_ref,
     x1_ref, h2_ref, exp_ref, pos_ref, gate_ref, cnt_ref, run_ref) = refs[2 * n_in:]

    @pl.when(pl.program_id(0) == 0)
    def _():
        run_ref[...] = jnp.zeros_like(run_ref)

    y = None
    for a_ref, w_ref in zip(a_refs, w_refs):
        part = jnp.dot(a_ref[...].astype(BF16), w_ref[...], preferred_element_type=F32)
        y = part if y is None else y + part
    m = mod_ref[...]
    x1 = _layer_norm(DN_ALPHA * x_ref[...] + m[2:3] * y, g_ref[...], b_ref[...])
    x1_ref[...] = x1
    h2 = x1 * (1.0 + m[4:5]) + m[3:4]
    h2_ref[...] = h2

    hh = h2.astype(BF16)
    hl = (h2 - hh.astype(F32)).astype(BF16)
    logits = (jnp.dot(hh, rwh_ref[...], preferred_element_type=F32)
              + jnp.dot(hh, rwl_ref[...], preferred_element_type=F32)
              + jnp.dot(hl, rwh_ref[...], preferred_element_type=F32)) + rb_ref[...]

    lane = lax.broadcasted_iota(jnp.int32, logits.shape, 1)
    work = logits
    vals, sels = [], []
    for k in range(TOP_K):
        mx = jnp.max(work, axis=-1, keepdims=True)
        idx = jnp.min(jnp.where(work == mx, lane, 128), axis=-1, keepdims=True)
        sel = lane == idx
        work = jnp.where(sel, -jnp.inf, work)
        exp_ref[:, k:k + 1] = idx
        vals.append(mx)
        sels.append(sel)
    exps = [jnp.exp(v - vals[0]) for v in vals]
    den = exps[0] + exps[1] + exps[2] + exps[3]
    for k in range(TOP_K):
        gate_ref[:, k:k + 1] = exps[k] / den

    picked = sels[0] | sels[1] | sels[2] | sels[3]
    onehot = jnp.where(picked, 1.0, 0.0)
    incl = jnp.dot(tri_ref[...], onehot.astype(BF16), preferred_element_type=F32)
    before = incl - onehot + run_ref[...]
    for k in range(TOP_K):
        pos = jnp.sum(jnp.where(sels[k], before, 0.0), axis=-1, keepdims=True)
        pos_ref[:, k:k + 1] = pos.astype(jnp.int32)
    run_ref[...] += jnp.sum(onehot, axis=0, keepdims=True)
    cnt_ref[...] = run_ref[...]


def proj_res_ln(acts, weights, x, mod, ln_g, ln_b, rw_hi, rw_lo, rb):
    n_in = len(acts)
    row = lambda c: pl.BlockSpec((ROW_TILE, c), lambda i: (i, 0))
    full = lambda shp: pl.BlockSpec(shp, lambda i: (0, 0))
    tri = jnp.asarray(np.tril(np.ones((ROW_TILE, ROW_TILE), np.float32)), dtype=BF16)
    in_specs = ([row(a.shape[1]) for a in acts] + [full(w.shape) for w in weights]
                + [row(D), pl.BlockSpec((None, 8, D), lambda i: (_mod_index(i, ROW_TILE), 0, 0)),
                   full((1, D)), full((1, D)), full((D, 128)), full((D, 128)), full((1, 128)),
                   full((ROW_TILE, ROW_TILE))])
    return pl.pallas_call(
        functools.partial(_proj_kernel, n_in=n_in),
        grid=(T_ALL // ROW_TILE,),
        in_specs=in_specs,
        out_specs=[row(D), row(D), row(TOP_K), row(TOP_K), row(TOP_K), full((1, 128))],
        out_shape=[jax.ShapeDtypeStruct((T_ALL, D), F32), jax.ShapeDtypeStruct((T_ALL, D), F32),
                   jax.ShapeDtypeStruct((T_ALL, TOP_K), jnp.int32), jax.ShapeDtypeStruct((T_ALL, TOP_K), jnp.int32),
                   jax.ShapeDtypeStruct((T_ALL, TOP_K), F32), jax.ShapeDtypeStruct((1, 128), F32)],
        scratch_shapes=[pltpu.VMEM((1, 128), F32)],
        compiler_params=_cparams(("arbitrary",)),
        name="proj_res_ln",
    )(*acts, *weights, x, mod, ln_g, ln_b, rw_hi, rw_lo, rb, tri)


def _dispatch_kernel(dest_ref, h_ref, xs_ref, sem):
    def copy(r, d):
        return pltpu.make_async_copy(h_ref.at[pl.ds(r, 1)], xs_ref.at[pl.ds(d, 1)], sem)

    def issue(r, _):
        for k in range(TOP_K):
            copy(r, dest_ref[r * TOP_K + k]).start(priority=k % 2)
        return 0

    lax.fori_loop(0, ROW_TILE, issue, 0, unroll=4)

    def drain(r, _):
        for k in range(TOP_K):
            copy(r, dest_ref[r * TOP_K + k]).wait()
        return 0

    lax.fori_loop(0, ROW_TILE, drain, 0, unroll=4)


def moe_dispatch(h2, dest_flat):
    return pl.pallas_call(
        _dispatch_kernel,
        grid=(T_ALL // ROW_TILE,),
        in_specs=[pl.BlockSpec((ROW_TILE * TOP_K,), lambda i: (i,), memory_space=pltpu.SMEM),
                  pl.BlockSpec((ROW_TILE, D), lambda i: (i, 0))],
        out_specs=pl.BlockSpec(memory_space=pl.ANY),
        out_shape=jax.ShapeDtypeStruct((N_ASSIGN, D), F32),
        scratch_shapes=[pltpu.SemaphoreType.DMA],
        compiler_params=_cparams(("arbitrary",)),
        name="moe_dispatch",
    )(dest_flat, h2)


W_CAST_ROWS = 128


def _moe_ffn_kernel(blk_ref, exp_ref, lo_ref, hi_ref, first_ref, newexp_ref,
                    x_ref, w1_ref, b1_ref, w2_ref, b2_ref, o_ref, w1b_ref, w2b_ref):
    i = pl.program_id(0)
    lo = lo_ref[i]
    hi = hi_ref[i]

    @pl.when(newexp_ref[i] == 1)
    def _():
        def cast(c, _):
            rows = pl.ds(pl.multiple_of(c * W_CAST_ROWS, W_CAST_ROWS), W_CAST_ROWS)
            w1b_ref[rows, :] = w1_ref[rows, :].astype(BF16)
            w2b_ref[rows, :] = w2_ref[rows, :].astype(BF16)
            return 0

        lax.fori_loop(0, D // W_CAST_ROWS, cast, 0)

    @pl.when(first_ref[i] == 1)
    def _():
        o_ref[...] = jnp.zeros_like(o_ref)

    @pl.when(hi > lo)
    def _():
        x = x_ref[...].astype(BF16)
        h = jnp.dot(x, w1b_ref[...], preferred_element_type=F32) + b1_ref[...]
        g = jnp.minimum(h[:, :D_FF], SWIGLU_LIMIT)
        u = jnp.clip(h[:, D_FF:], -SWIGLU_LIMIT, SWIGLU_LIMIT)
        a = g * jax.nn.sigmoid(SWIGLU_ALPHA * g) * (u + 1.0)
        y = jnp.dot(a.astype(BF16), w2b_ref[...], preferred_element_type=F32) + b2_ref[...]
        row = lax.broadcasted_iota(jnp.int32, (MOE_TILE, 1), 0)
        o_ref[...] += jnp.where((row >= lo) & (row < hi), y, 0.0)


def moe_ffn(items, xs, w1, b1, w2, b2):
    assert D == D_FF
    blk, exp, lo, hi, first, newexp = items
    grid_spec = pltpu.PrefetchScalarGridSpec(
        num_scalar_prefetch=6,
        grid=(N_MOE_ITEMS,),
        in_specs=[
            pl.BlockSpec((MOE_TILE, D), lambda i, blk, exp, *_: (blk[i], 0)),
            pl.BlockSpec((None, D, 2 * D_FF), lambda i, blk, exp, *_: (exp[i], 0, 0)),
            pl.BlockSpec((None, 1, 2 * D_FF), lambda i, blk, exp, *_: (exp[i], 0, 0)),
            pl.BlockSpec((None, D_FF, D), lambda i, blk, exp, *_: (exp[i], 0, 0)),
            pl.BlockSpec((None, 1, D), lambda i, blk, exp, *_: (exp[i], 0, 0)),
        ],
        out_specs=pl.BlockSpec((MOE_TILE, D), lambda i, blk, exp, *_: (blk[i], 0)),
        scratch_shapes=[pltpu.VMEM((D, 2 * D_FF), BF16), pltpu.VMEM((D_FF, D), BF16)],
    )
    return pl.pallas_call(
        _moe_ffn_kernel,
        grid_spec=grid_spec,
        out_shape=jax.ShapeDtypeStruct((N_ASSIGN, D), F32),
        compiler_params=_cparams(("arbitrary",)),
        name="moe_ffn",
    )(blk, exp, lo, hi, first, newexp, xs, w1, b1, w2, b2)


def _combine_kernel(dest_ref, ys_ref, gates_ref, x_ref, mod_ref, g_ref, b_ref, o_ref, buf_ref, sem):
    def copy(r, k, d):
        return pltpu.make_async_copy(ys_ref.at[pl.ds(d, 1)], buf_ref.at[k, pl.ds(r, 1)], sem)

    def issue(r, _):
        for k in range(TOP_K):
            copy(r, k, dest_ref[r * TOP_K + k]).start(priority=k % 2)
        return 0

    lax.fori_loop(0, COMB_TILE, issue, 0, unroll=4)

    def drain(r, _):
        for k in range(TOP_K):
            copy(r, k, dest_ref[r * TOP_K + k]).wait()
        return 0

    lax.fori_loop(0, COMB_TILE, drain, 0, unroll=4)

    gates = gates_ref[...]
    y = gates[:, 0:1] * buf_ref[0]
    for k in range(1, TOP_K):
        y = y + gates[:, k:k + 1] * buf_ref[k]
    m = mod_ref[...]
    o_ref[...] = _layer_norm(DN_ALPHA * x_ref[...] + m[5:6] * y, g_ref[...], b_ref[...])


def moe_combine(ys, dest_flat, gates, x1, mod, ln_g, ln_b):
    full = lambda shp: pl.BlockSpec(shp, lambda i: (0, 0))
    return pl.pallas_call(
        _combine_kernel,
        grid=(T_ALL // COMB_TILE,),
        in_specs=[pl.BlockSpec((COMB_TILE * TOP_K,), lambda i: (i,), memory_space=pltpu.SMEM),
                  pl.BlockSpec(memory_space=pl.ANY),
                  pl.BlockSpec((COMB_TILE, TOP_K), lambda i: (i, 0)),
                  pl.BlockSpec((COMB_TILE, D), lambda i: (i, 0)),
                  pl.BlockSpec((None, 8, D), lambda i: (_mod_index(i, COMB_TILE), 0, 0)),
                  full((1, D)), full((1, D))],
        out_specs=pl.BlockSpec((COMB_TILE, D), lambda i: (i, 0)),
        out_shape=jax.ShapeDtypeStruct((T_ALL, D), F32),
        scratch_shapes=[pltpu.VMEM((TOP_K, COMB_TILE, D), F32), pltpu.SemaphoreType.DMA],
        compiler_params=_cparams(("arbitrary",)),
        name="moe_combine",
    )(dest_flat, ys, gates, x1, mod, ln_g, ln_b)


def _routing_tables(experts, pos, counts):
    counts = counts[0, :N_EXPERTS].astype(jnp.int32)
    starts = jnp.cumsum(counts) - counts
    eids = jnp.arange(N_EXPERTS, dtype=jnp.int32)
    start_of_pick = jnp.sum(jnp.where(experts[..., None] == eids, starts, 0), axis=-1)
    dest = start_of_pick + pos
    bnd = jnp.sort(jnp.concatenate([jnp.arange(N_MOE_BLOCKS + 1, dtype=jnp.int32) * MOE_TILE, starts[1:]]))
    a, b = bnd[:-1], bnd[1:]
    blk = jnp.minimum(a // MOE_TILE, N_MOE_BLOCKS - 1)
    lo = a - blk * MOE_TILE
    hi = b - blk * MOE_TILE
    exp = jnp.clip(jnp.sum((starts[None, :] <= a[:, None]).astype(jnp.int32), axis=1) - 1, 0, N_EXPERTS - 1)
    one = jnp.ones((1,), jnp.int32)
    first = jnp.concatenate([one, (blk[1:] != blk[:-1]).astype(jnp.int32)])
    newexp = jnp.concatenate([one, (exp[1:] != exp[:-1]).astype(jnp.int32)])
    items = tuple(v.astype(jnp.int32) for v in (blk, exp, lo, hi, first, newexp))
    return dest.astype(jnp.int32).reshape(-1), items


def moe_layer(h2, experts, pos, gates, counts, x1, mod, ln_g, ln_b, w1, b1, w2, b2):
    dest_flat, items = _routing_tables(experts, pos, counts)
    xs = moe_dispatch(h2, dest_flat)
    ys = moe_ffn(items, xs, w1, b1, w2, b2)
    return moe_combine(ys, dest_flat, gates, x1, mod, ln_g, ln_b)


def _router_operands(router_w, router_b):
    w = jnp.pad(router_w, ((0, 0), (0, 128 - N_EXPERTS)))
    hi = w.astype(BF16)
    lo = (w - hi.astype(F32)).astype(BF16)
    b = jnp.concatenate([router_b, jnp.full((128 - N_EXPERTS,), -jnp.inf, F32)])[None]
    return hi, lo, b


def kernel(x_prompt, x_sample, state_s5, cache_na_k, cache_na_v, c, c_ctx, w_mod, b_mod, ln1_g, ln1_b, ln2_g, ln2_b, ev_w_in, ev_w_out, hy_conv_w, hy_conv_b, hy_f_w1, hy_f_b1, hy_f_w2, hy_f_b2, hy_f_w3, hy_f_freq, hy_bias, s5_lam_re, s5_lam_im, s5_log_dt, s5_b_re, s5_b_im, s5_c_re, s5_c_im, s5_d, s5_glu_w, s5_glu_b, od_w_in, od_w_out, na_rpb, router_w, router_b, moe_w1, moe_b1, moe_w2, moe_b2):
    x = jnp.concatenate([x_prompt.reshape(T_CTX, D), x_sample.reshape(T_LAT, D)], axis=0)
    cvec = jnp.pad(jnp.concatenate([c_ctx[None], c], axis=0), ((0, 5), (0, 0)))

    new_state = None
    new_k = new_v = None
    for l in range(DEPTH):
        i = l // 2
        mod = matmul(cvec, w_mod[l], b_mod[l][None], tm=8, tn=512, tk=D, precise=True, silu_a=True)
        mod = jnp.pad(mod.reshape(8, 6, D)[:3], ((0, 0), (0, 2), (0, 0)))

        if l % 2 == 0:
            u = modlinear(x, mod, ev_w_in[i].astype(BF16))
            y_hy = jnp.zeros((T_ALL, HY_CH), F32)
            for n_seq, seq_len, row_off, tile in ((N_CTX_SEQ, CTX_LEN, 0, CTX_LEN), (N_LAT_SEQ, LAT_LEN, T_CTX, 1024)):
                taps = hyena_filter_taps(seq_len, hy_f_w1[i], hy_f_b1[i], hy_f_w2[i], hy_f_b2[i], hy_f_w3[i],
                                         hy_f_freq[i])
                y_hy = hyena_group(u, hy_conv_w[i], hy_conv_b[i][None], taps, hy_bias[i][None], y_hy,
                                   n_seq=n_seq, seq_len=seq_len, row_off=row_off, tile=tile)

            h0_lat = state_s5[:, i]
            yf_parts, yb_parts, finals = [], [], []
            for r in range(2):
                tables = _s5_tables(s5_lam_re[i, r], s5_lam_im[i, r], s5_log_dt[i, r], s5_b_re[i, r], s5_b_im[i, r],
                                    s5_c_re[i, r], s5_c_im[i, r], reverse=(r == 1))
                h0_l = jnp.concatenate([h0_lat[:, r, :, :, 0].reshape(N_LAT_SEQ, 1, S5_NS),
                                        h0_lat[:, r, :, :, 1].reshape(N_LAT_SEQ, 1, S5_NS)], axis=-1)
                h0 = jnp.concatenate([jnp.zeros((N_CTX_SEQ, 1, 2 * S5_NS), F32), h0_l], axis=0)
                y_dir, fin_all = s5_direction(u, tables, h0, reverse=(r == 1))
                (yf_parts if r == 0 else yb_parts).append(y_dir)
                fin = fin_all[:N_CTX_SEQ].reshape(N_CTX_SEQ, 2, S5_GROUPS, S5_STATE)
                finals.append(jnp.stack([fin[:, 0], fin[:, 1]], axis=-1))
            new_state = jnp.stack(finals, axis=1)[:, None]
            y_s5 = s5_post(u, yf_parts[0], yb_parts[0], s5_d[i].reshape(1, S5_CH), s5_glu_w[i].astype(BF16),
                           s5_glu_b[i][None])
            acts = [y_hy, y_s5]
            w_out = ev_w_out[i].astype(BF16)
            weights = [w_out[:HY_CH], w_out[HY_CH:]]
        else:
            qkv = modlinear(x, mod, od_w_in[i].astype(BF16))
            new_k = qkv[:T_CTX, D:2 * D].reshape(N_CTX_SEQ, 1, CTX_LEN, N_HEADS, HEAD_DIM)
            new_v = qkv[:T_CTX, 2 * D:].reshape(N_CTX_SEQ, 1, CTX_LEN, N_HEADS, HEAD_DIM)
            attn = ctx_attention(qkv, jnp.zeros((T_ALL, D), F32))
            kvb = qkv[T_CTX:, D:].astype(BF16)
            kcb = cache_na_k[:, i].reshape(N_LAT_SEQ * CTX_LEN, D).astype(BF16)
            vcb = cache_na_v[:, i].reshape(N_LAT_SEQ * CTX_LEN, D).astype(BF16)
            attn = na_attention(qkv, kvb, kcb, vcb, _na_bias_table(na_rpb[i]), attn)
            acts = [attn]
            weights = [od_w_out[i].astype(BF16)]

        rw_hi, rw_lo, rb = _router_operands(router_w[l], router_b[l])
        x1, h2, experts, pos, gates, counts = proj_res_ln(acts, weights, x, mod, ln1_g[l][None], ln1_b[l][None],
                                                          rw_hi, rw_lo, rb)
        x = moe_layer(h2, experts, pos, gates, counts, x1, mod, ln2_g[l][None], ln2_b[l][None],
                      moe_w1[l], moe_b1[l][:, None, :], moe_w2[l], moe_b2[l][:, None, :])

    y_prompt = x[:T_CTX].reshape(N_CTX_SEQ, CTX_LEN, D)
    y_sample = x[T_CTX:].reshape(N_LAT_SEQ, LAT_LEN, D)
    return (y_prompt, y_sample, new_state, new_k, new_v)
```

```python
import functools
import math

import jax
import jax.numpy as jnp
import numpy as np
from jax import lax
from jax.experimental import pallas as pl
from jax.experimental.pallas import tpu as pltpu

F32 = jnp.float32
BF16 = jnp.bfloat16
HIGHEST = lax.Precision.HIGHEST

D = 1024
N_CTX_SEQ, CTX_LEN = 32, 256
N_LAT_SEQ, LAT_LEN = 2, 4096
T_CTX = N_CTX_SEQ * CTX_LEN
T_LAT = N_LAT_SEQ * LAT_LEN
T_ALL = T_CTX + T_LAT
DEPTH = 2
HY_CH = 512
S5_CH = 512
S5_GROUPS, S5_GROUP, S5_STATE = 32, 16, 64
S5_NS = S5_GROUPS * S5_STATE
HY_EMB, HY_BANDS, HY_ORDER = 33, 16, 64
N_HEADS, HEAD_DIM = 16, 64
GRID_W, WIN_R, WIN_C = 64, 8, 16
N_EXPERTS, TOP_K, D_FF = 32, 4, 1024
SWIGLU_LIMIT, SWIGLU_ALPHA = 7.0, 1.702
LN_EPS = 1e-5
DN_ALPHA = (2 * DEPTH) ** 0.25
NEG_BIG = -1e30

ROW_TILE = 256
MOE_TILE = 256
N_ASSIGN = T_ALL * TOP_K
N_MOE_BLOCKS = N_ASSIGN // MOE_TILE
N_MOE_ITEMS = N_MOE_BLOCKS + N_EXPERTS - 1
COMB_TILE = 128
S5_CHUNK = 256
S5_SEG = S5_CHUNK // 8
S5_COLS = 512
VMEM_LIMIT = 56 * 1024 * 1024


def _cparams(sem, vmem=None):
    return pltpu.CompilerParams(dimension_semantics=sem, vmem_limit_bytes=vmem or VMEM_LIMIT)


def _mod_index(i, rows_per_tile):
    n_ctx = T_CTX // rows_per_tile
    per_lat = LAT_LEN // rows_per_tile
    return jnp.where(i < n_ctx, 0, 1 + (i - n_ctx) // per_lat)


def _mm_kernel(a_ref, b_ref, bias_ref, o_ref, acc_ref, *, nk, precise, silu_a):
    k = pl.program_id(2)

    @pl.when(k == 0)
    def _():
        acc_ref[...] = jnp.zeros_like(acc_ref)

    a = a_ref[...]
    if silu_a:
        a = a * jax.nn.sigmoid(a)
    if precise:
        acc_ref[...] += jnp.dot(a, b_ref[...], preferred_element_type=F32, precision=HIGHEST)
    else:
        acc_ref[...] += jnp.dot(a.astype(BF16), b_ref[...].astype(BF16), preferred_element_type=F32)

    @pl.when(k == nk - 1)
    def _():
        o_ref[...] = acc_ref[...] + bias_ref[...]


def matmul(a, b, bias=None, *, tm, tn, tk, b_layer=None, precise=False, silu_a=False):
    kdim, n = b.shape[-2:]
    m = a.shape[0]
    if bias is None:
        bias = jnp.zeros((1, n), F32)
    nk = kdim // tk
    if b.ndim == 3:
        b_spec = pl.BlockSpec((None, tk, tn), lambda i, j, k: (b_layer, k, j))
    else:
        b_spec = pl.BlockSpec((tk, tn), lambda i, j, k: (k, j))
    return pl.pallas_call(
        functools.partial(_mm_kernel, nk=nk, precise=precise, silu_a=silu_a),
        grid=(m // tm, n // tn, nk),
        in_specs=[pl.BlockSpec((tm, tk), lambda i, j, k: (i, k)), b_spec,
                  pl.BlockSpec((1, tn), lambda i, j, k: (0, j))],
        out_specs=pl.BlockSpec((tm, tn), lambda i, j, k: (i, j)),
        out_shape=jax.ShapeDtypeStruct((m, n), F32),
        scratch_shapes=[pltpu.VMEM((tm, tn), F32)],
        compiler_params=_cparams(("parallel", "parallel", "arbitrary")),
        name="matmul",
    )(a, b, bias)


def _modlinear_kernel(x_ref, mod_ref, w_ref, o_ref):
    m = mod_ref[...]
    h = x_ref[...] * (1.0 + m[1:2]) + m[0:1]
    o_ref[...] = jnp.dot(h.astype(BF16), w_ref[...], preferred_element_type=F32)


def modlinear(x, mod, w_bf16):
    n = w_bf16.shape[1]
    return pl.pallas_call(
        _modlinear_kernel,
        grid=(T_ALL // ROW_TILE,),
        in_specs=[
            pl.BlockSpec((ROW_TILE, D), lambda i: (i, 0)),
            pl.BlockSpec((None, 8, D), lambda i: (_mod_index(i, ROW_TILE), 0, 0)),
            pl.BlockSpec((D, n), lambda i: (0, 0)),
        ],
        out_specs=pl.BlockSpec((ROW_TILE, n), lambda i: (i, 0)),
        out_shape=jax.ShapeDtypeStruct((T_ALL, n), F32),
        compiler_params=_cparams(("parallel",)),
        name="modlinear",
    )(x, mod, w_bf16)


def _filter_kernel(z_ref, w1_ref, b1_ref, w2_ref, b2_ref, w3_ref, fq_ref, dl_ref, hsum_ref, hdiff_ref, *, tile):
    z = z_ref[...]
    fq = fq_ref[...]
    h = jnp.sin(fq * (jnp.dot(z, w1_ref[...], preferred_element_type=F32, precision=HIGHEST) + b1_ref[...]))
    h = jnp.sin(fq * (jnp.dot(h, w2_ref[...], preferred_element_type=F32, precision=HIGHEST) + b2_ref[...]))
    h = jnp.dot(h, w3_ref[...], preferred_element_type=F32, precision=HIGHEST)
    decay = jnp.exp(-z[:, 0:1] * dl_ref[...])
    hf = h[:, :HY_CH] * decay
    hb = h[:, HY_CH:] * decay
    row = lax.broadcasted_iota(jnp.int32, (tile, 1), 0) + pl.program_id(0) * tile
    hsum_ref[...] = hf + hb
    hdiff_ref[...] = jnp.where(row == 0, hf + hb, hf - hb)


def hyena_filter_taps(seq_len, w1, b1, w2, b2, w3, freq):
    t = jnp.linspace(0.0, 1.0, seq_len, dtype=F32)[:, None]
    w = 2.0 * math.pi * jnp.arange(seq_len, dtype=F32)[:, None] / seq_len
    f = jnp.linspace(1e-4, HY_BANDS - 1, HY_BANDS, dtype=F32)[None, :]
    z = jnp.concatenate([t, jnp.cos(f * w), -jnp.sin(f * w)], -1)
    z = jnp.pad(z, ((0, 0), (0, 128 - HY_EMB)))
    pad_o = 128 - HY_ORDER
    w1p = jnp.pad(w1, ((0, 128 - HY_EMB), (0, pad_o)))
    w2p = jnp.pad(w2, ((0, pad_o), (0, pad_o)))
    w3p = jnp.pad(w3, ((0, pad_o), (0, 0)))
    b1p = jnp.pad(b1, (0, pad_o))[None]
    b2p = jnp.pad(b2, (0, pad_o))[None]
    fqp = jnp.pad(freq, (0, pad_o))[None]
    max_decay = math.log(1e-2) / 0.3
    min_decay = math.log(1e-2) / 1.5
    absdelta = jnp.abs(jnp.linspace(min_decay, max_decay, HY_CH, dtype=F32))[None]
    tile = 256
    full = lambda shp: pl.BlockSpec(shp, lambda i: (0, 0))
    return pl.pallas_call(
        functools.partial(_filter_kernel, tile=tile),
        grid=(seq_len // tile,),
        in_specs=[pl.BlockSpec((tile, 128), lambda i: (i, 0)), full((128, 128)), full((1, 128)), full((128, 128)),
                  full((1, 128)), full((128, 2 * HY_CH)), full((1, 128)), full((1, HY_CH))],
        out_specs=[pl.BlockSpec((tile, HY_CH), lambda i: (i, 0))] * 2,
        out_shape=[jax.ShapeDtypeStruct((seq_len, HY_CH), F32)] * 2,
        compiler_params=_cparams(("parallel",)),
        name="hyena_filter",
    )(z, w1p, b1p, w2p, b2p, w3p, fqp, absdelta)


def _dft_tables(seq_len):
    s = int(round(math.sqrt(seq_len)))
    j = jnp.arange(s, dtype=jnp.int32)[:, None]
    c = jnp.arange(seq_len, dtype=jnp.int32)[None, :]
    unit = math.pi / (2 * seq_len)

    def cos_sin(idx):
        ang = (idx % (4 * seq_len)).astype(F32) * unit
        return jnp.cos(ang), jnp.sin(ang)

    def expand(ca, sa, cb, sb):
        cos = ca[:, None, :] * cb[None, :, :] - sa[:, None, :] * sb[None, :, :]
        nsin = -(sa[:, None, :] * cb[None, :, :] + ca[:, None, :] * sb[None, :, :])
        return cos.reshape(seq_len, seq_len).astype(BF16), nsin.reshape(seq_len, seq_len).astype(BF16)

    cm, nsm = expand(*cos_sin((2 * s * j) * c), *cos_sin((2 * j + 1) * c))
    cmt, nsmt = expand(*cos_sin((2 * c + 1) * (s * j)), *cos_sin((2 * c + 1) * j))
    return cm, nsm, cmt, nsmt


def _hyena_pre_kernel(u0_ref, u1_ref, u2_ref, w0_ref, w1_ref, w2_ref, b0_ref, b1_ref, b2_ref,
                      vx_ref, vxb_ref, x0_ref, *, seq_len):
    row = lax.broadcasted_iota(jnp.int32, (seq_len, 1), 0)

    def short_conv(u_ref, w_ref, b_ref):
        a = u_ref[...]
        w = w_ref[...]
        prev = jnp.where(row == 0, 0.0, pltpu.roll(a, 1, 0))
        nxt = jnp.where(row == seq_len - 1, 0.0, pltpu.roll(a, seq_len - 1, 0))
        return prev * w[0:1] + a * w[1:2] + nxt * w[2:3] + b_ref[...]

    x0 = short_conv(u0_ref, w0_ref, b0_ref)
    x1 = short_conv(u1_ref, w1_ref, b1_ref)
    v = short_conv(u2_ref, w2_ref, b2_ref)
    vx = v * x1
    vx_ref[...] = vx
    vxb_ref[...] = vx.astype(BF16)
    x0_ref[...] = x0


def hyena_pre(u, conv_w, conv_b, *, n_seq, seq_len, row_off):
    cb = 128
    ncb = HY_CH // cb
    rb0 = row_off // seq_len
    uspec = lambda part: pl.BlockSpec((seq_len, cb), lambda b, j: (rb0 + b, part * ncb + j))
    wspec = lambda part: pl.BlockSpec((3, cb), lambda b, j: (0, part * ncb + j))
    bspec = lambda part: pl.BlockSpec((1, cb), lambda b, j: (0, part * ncb + j))
    ospec = pl.BlockSpec((seq_len, cb), lambda b, j: (b, j))
    rows = n_seq * seq_len
    return pl.pallas_call(
        functools.partial(_hyena_pre_kernel, seq_len=seq_len),
        grid=(n_seq, ncb),
        in_specs=[uspec(0), uspec(1), uspec(2), wspec(0), wspec(1), wspec(2), bspec(0), bspec(1), bspec(2)],
        out_specs=[ospec, ospec, ospec],
        out_shape=[jax.ShapeDtypeStruct((rows, HY_CH), F32), jax.ShapeDtypeStruct((rows, HY_CH), BF16),
                   jax.ShapeDtypeStruct((rows, HY_CH), F32)],
        compiler_params=_cparams(("parallel", "parallel")),
        name="hyena_pre",
    )(u, u, u, conv_w, conv_w, conv_w, conv_b, conv_b, conv_b)


def _dft_fwd_kernel(cm_ref, nsm_ref, v_ref, hr_ref, hi_ref, zr_ref, zi_ref, accr_ref, acci_ref, *, nk):
    k = pl.program_id(3)

    @pl.when(k == 0)
    def _():
        accr_ref[...] = jnp.zeros_like(accr_ref)
        acci_ref[...] = jnp.zeros_like(acci_ref)

    v = v_ref[...]
    accr_ref[...] += jnp.dot(cm_ref[...], v, preferred_element_type=F32)
    acci_ref[...] += jnp.dot(nsm_ref[...], v, preferred_element_type=F32)

    @pl.when(k == nk - 1)
    def _():
        xr, xi = accr_ref[...], acci_ref[...]
        hr, hi = hr_ref[...], hi_ref[...]
        zr_ref[...] = (xr * hr - xi * hi).astype(BF16)
        zi_ref[...] = (xr * hi + xi * hr).astype(BF16)


def dft_fwd(cm, nsm, vxb, hr, hi, *, n_seq, seq_len, tile):
    nk = seq_len // tile
    tn = HY_CH
    v3 = vxb.reshape(n_seq, seq_len, HY_CH)
    zspec = pl.BlockSpec((None, tile, tn), lambda s, i, j, k: (s, i, j))
    return pl.pallas_call(
        functools.partial(_dft_fwd_kernel, nk=nk),
        grid=(n_seq, seq_len // tile, HY_CH // tn, nk),
        in_specs=[
            pl.BlockSpec((tile, tile), lambda s, i, j, k: (i, k)),
            pl.BlockSpec((tile, tile), lambda s, i, j, k: (i, k)),
            pl.BlockSpec((None, tile, tn), lambda s, i, j, k: (s, k, j)),
            pl.BlockSpec((tile, tn), lambda s, i, j, k: (i, j)),
            pl.BlockSpec((tile, tn), lambda s, i, j, k: (i, j)),
        ],
        out_specs=[zspec, zspec],
        out_shape=[jax.ShapeDtypeStruct((n_seq, seq_len, HY_CH), BF16)] * 2,
        scratch_shapes=[pltpu.VMEM((tile, tn), F32), pltpu.VMEM((tile, tn), F32)],
        compiler_params=_cparams(("parallel", "parallel", "parallel", "arbitrary")),
        name="dft_fwd",
    )(cm, nsm, v3, hr, hi)


def _dft_inv_kernel(cmt_ref, nsmt_ref, zr_ref, zi_ref, vx_ref, x0_ref, bias_ref, prev_ref, o_ref, acc_ref, *,
                    nk, inv_len):
    del prev_ref
    k = pl.program_id(3)

    @pl.when(k == 0)
    def _():
        acc_ref[...] = jnp.zeros_like(acc_ref)

    acc_ref[...] += (jnp.dot(cmt_ref[...], zr_ref[...], preferred_element_type=F32)
                     + jnp.dot(nsmt_ref[...], zi_ref[...], preferred_element_type=F32))

    @pl.when(k == nk - 1)
    def _():
        conv = acc_ref[...] * inv_len
        o_ref[...] = (conv + vx_ref[...] * bias_ref[...]) * x0_ref[...]


def dft_inv(cmt, nsmt, zr, zi, vx, x0, bias, y_prev, *, n_seq, seq_len, row_off, tile):
    nk = seq_len // tile
    tn = HY_CH
    per_seq = seq_len // tile
    rb0 = row_off // tile
    zspec = pl.BlockSpec((None, tile, tn), lambda s, i, j, k: (s, k, j))
    espec = pl.BlockSpec((None, tile, tn), lambda s, i, j, k: (s, i, j))
    return pl.pallas_call(
        functools.partial(_dft_inv_kernel, nk=nk, inv_len=1.0 / seq_len),
        grid=(n_seq, per_seq, HY_CH // tn, nk),
        in_specs=[
            pl.BlockSpec((tile, tile), lambda s, i, j, k: (i, k)),
            pl.BlockSpec((tile, tile), lambda s, i, j, k: (i, k)),
            zspec, zspec, espec, espec,
            pl.BlockSpec((1, tn), lambda s, i, j, k: (0, j)),
            pl.BlockSpec(memory_space=pl.ANY),
        ],
        out_specs=pl.BlockSpec((tile, tn), lambda s, i, j, k: (rb0 + s * per_seq + i, j)),
        out_shape=jax.ShapeDtypeStruct((T_ALL, HY_CH), F32),
        scratch_shapes=[pltpu.VMEM((tile, tn), F32)],
        input_output_aliases={7: 0},
        compiler_params=_cparams(("parallel", "parallel", "parallel", "arbitrary")),
        name="dft_inv",
    )(cmt, nsmt, zr, zi, vx.reshape(n_seq, seq_len, HY_CH), x0.reshape(n_seq, seq_len, HY_CH), bias, y_prev)


def hyena_group(u, conv_w, conv_b, taps, bias, y_prev, *, n_seq, seq_len, row_off, tile):
    hsum, hdiff = taps
    cm, nsm, cmt, nsmt = _dft_tables(seq_len)
    hr = matmul(cm, hsum, tm=tile, tn=HY_CH, tk=tile)
    hi = matmul(nsm, hdiff, tm=tile, tn=HY_CH, tk=tile)
    vx, vxb, x0 = hyena_pre(u, conv_w, conv_b, n_seq=n_seq, seq_len=seq_len, row_off=row_off)
    zr, zi = dft_fwd(cm, nsm, vxb, hr, hi, n_seq=n_seq, seq_len=seq_len, tile=tile)
    return dft_inv(cmt, nsmt, zr, zi, vx, x0, bias, y_prev, n_seq=n_seq, seq_len=seq_len, row_off=row_off, tile=tile)


def _s5_tables(lam_re, lam_im, log_dt, b_re, b_im, c_re, c_im, reverse):
    dt = jnp.exp(log_dt)[:, None]
    a = lam_re * dt
    b = lam_im * dt
    mag = jnp.exp(a)
    lbr, lbi = mag * jnp.cos(b), mag * jnp.sin(b)
    den = lam_re * lam_re + lam_im * lam_im
    qr = ((lbr - 1.0) * lam_re + lbi * lam_im) / den
    qi = (lbi * lam_re - (lbr - 1.0) * lam_im) / den
    bbr = qr[..., None] * b_re - qi[..., None] * b_im
    bbi = qr[..., None] * b_im + qi[..., None] * b_re
    eye = jnp.eye(S5_GROUPS, dtype=F32)

    def in_block(m):
        return jnp.einsum("gnc,gh->gchn", m, eye).reshape(S5_CH, S5_NS)

    def out_block(m):
        return jnp.einsum("gcn,gh->gnhc", m, eye).reshape(S5_NS, S5_CH)

    b_blk = jnp.concatenate([in_block(bbr), in_block(bbi)], axis=1).astype(BF16)
    c_blk = jnp.concatenate([out_block(c_re), out_block(-c_im)], axis=0).astype(BF16)
    flat = lambda m: m.reshape(1, S5_NS)
    lam = jnp.concatenate([flat(lbr), flat(lbi)], axis=1)
    steps = jnp.arange(1, S5_SEG + 1, dtype=F32)
    if reverse:
        steps = steps[::-1]
    pa = a.reshape(1, S5_NS) * steps[:, None]
    pb = b.reshape(1, S5_NS) * steps[:, None]
    pw = jnp.concatenate([jnp.exp(pa) * jnp.cos(pb), jnp.exp(pa) * jnp.sin(pb)], axis=1)
    sa, sb = flat(a) * S5_SEG, flat(b) * S5_SEG
    lam_seg = jnp.concatenate([jnp.exp(sa) * jnp.cos(sb), jnp.exp(sa) * jnp.sin(sb)], axis=1)
    return b_blk, c_blk, lam, lam_seg, pw


N_S5_CTX_STEPS = T_CTX // S5_CHUNK
N_S5_LAT_CHUNKS = LAT_LEN // S5_CHUNK
N_S5_STEPS = T_ALL // S5_CHUNK


def _s5_step_info(i, reverse):
    k = jnp.maximum(i - N_S5_CTX_STEPS, 0)
    b, j = k // N_S5_LAT_CHUNKS, k % N_S5_LAT_CHUNKS
    chunk = (N_S5_LAT_CHUNKS - 1 - j) if reverse else j
    is_ctx = i < N_S5_CTX_STEPS
    row_blk = jnp.where(is_ctx, i, N_S5_CTX_STEPS + b * N_S5_LAT_CHUNKS + chunk)
    seq = jnp.where(is_ctx, i, N_CTX_SEQ + b)
    return row_blk, seq, is_ctx | (j == 0), is_ctx | (j == N_S5_LAT_CHUNKS - 1)


def _s5_kernel(u_ref, perm_ref, permt_ref, bblk_ref, cblk_ref, lam_ref, lseg_ref, pw_ref, h0_ref, y_ref, fin_ref,
               bu_ref, hend_ref, cin_ref, carry_ref, *, reverse):
    _, _, is_first, is_last = _s5_step_info(pl.program_id(0), reverse)
    ns = S5_NS

    @pl.when(is_first)
    def _():
        carry_ref[...] = h0_ref[...]

    u_seg = jnp.dot(perm_ref[...], u_ref[...].astype(BF16), preferred_element_type=F32).astype(BF16)
    bu_ref[...] = jnp.dot(u_seg, bblk_ref[...], preferred_element_type=F32)

    for cb in range(ns // S5_COLS):
        re_cols = pl.ds(cb * S5_COLS, S5_COLS)
        im_cols = pl.ds(ns + cb * S5_COLS, S5_COLS)
        lr = jnp.broadcast_to(lam_ref[:, re_cols], (8, S5_COLS))
        li = jnp.broadcast_to(lam_ref[:, im_cols], (8, S5_COLS))

        def step(kk, carry, re_cols=re_cols, im_cols=im_cols, lr=lr, li=li):
            hr, hi = carry
            k = (S5_SEG - 1 - kk) if reverse else kk
            rows = pl.ds(pl.multiple_of(k * 8, 8), 8)
            nr = lr * hr - li * hi + bu_ref[rows, re_cols]
            ni = lr * hi + li * hr + bu_ref[rows, im_cols]
            bu_ref[rows, re_cols] = nr
            bu_ref[rows, im_cols] = ni
            return nr, ni

        zero = jnp.zeros((8, S5_COLS), F32)
        hr, hi = lax.fori_loop(0, S5_SEG, step, (zero, zero), unroll=4)
        hend_ref[:, re_cols] = hr
        hend_ref[:, im_cols] = hi

    cr = carry_ref[:, :ns]
    ci = carry_ref[:, ns:]
    lsr = lseg_ref[:, :ns]
    lsi = lseg_ref[:, ns:]
    for p in range(8):
        s = 7 - p if reverse else p
        cin_ref[s:s + 1, :ns] = cr
        cin_ref[s:s + 1, ns:] = ci
        er = hend_ref[s:s + 1, :ns]
        ei = hend_ref[s:s + 1, ns:]
        cr, ci = er + lsr * cr - lsi * ci, ei + lsr * ci + lsi * cr
    carry_ref[:, :ns] = cr
    carry_ref[:, ns:] = ci

    for cb in range(ns // S5_COLS):
        re_cols = pl.ds(cb * S5_COLS, S5_COLS)
        im_cols = pl.ds(ns + cb * S5_COLS, S5_COLS)
        cinr = cin_ref[:, re_cols]
        cini = cin_ref[:, im_cols]

        def fix(k, _, re_cols=re_cols, im_cols=im_cols, cinr=cinr, cini=cini):
            rows = pl.ds(pl.multiple_of(k * 8, 8), 8)
            pr = pw_ref[pl.ds(k, 1), re_cols]
            pi = pw_ref[pl.ds(k, 1), im_cols]
            bu_ref[rows, re_cols] += pr * cinr - pi * cini
            bu_ref[rows, im_cols] += pr * cini + pi * cinr
            return 0

        lax.fori_loop(0, S5_SEG, fix, 0, unroll=4)

    y_seg = jnp.dot(bu_ref[...].astype(BF16), cblk_ref[...], preferred_element_type=F32)
    y_hi = y_seg.astype(BF16)
    y_lo = (y_seg - y_hi.astype(F32)).astype(BF16)
    y_ref[...] = (jnp.dot(permt_ref[...], y_hi, preferred_element_type=F32)
                  + jnp.dot(permt_ref[...], y_lo, preferred_element_type=F32))

    @pl.when(is_last)
    def _():
        fin_ref[...] = carry_ref[...]


def s5_direction(u, tables, h0, *, reverse):
    b_blk, c_blk, lam, lam_seg, pw = tables
    s5_col = 3 * HY_CH // S5_CH
    n_seq = N_CTX_SEQ + N_LAT_SEQ
    row_blk = lambda i: _s5_step_info(i, reverse)[0]
    seq = lambda i: _s5_step_info(i, reverse)[1]
    const = lambda shp: pl.BlockSpec(shp, lambda i: (0, 0))
    t = np.arange(S5_CHUNK)
    perm_np = np.zeros((S5_CHUNK, S5_CHUNK), np.float32)
    perm_np[(t % S5_SEG) * 8 + t // S5_SEG, t] = 1.0
    perm = jnp.asarray(perm_np, dtype=BF16)
    perm_t = jnp.asarray(perm_np.T, dtype=BF16)
    return pl.pallas_call(
        functools.partial(_s5_kernel, reverse=reverse),
        grid=(N_S5_STEPS,),
        in_specs=[
            pl.BlockSpec((S5_CHUNK, S5_CH), lambda i: (row_blk(i), s5_col)),
            const((S5_CHUNK, S5_CHUNK)), const((S5_CHUNK, S5_CHUNK)),
            const((S5_CH, 2 * S5_NS)), const((2 * S5_NS, S5_CH)), const((1, 2 * S5_NS)), const((1, 2 * S5_NS)),
            const((S5_SEG, 2 * S5_NS)),
            pl.BlockSpec((None, 1, 2 * S5_NS), lambda i: (seq(i), 0, 0)),
        ],
        out_specs=[pl.BlockSpec((S5_CHUNK, S5_CH), lambda i: (row_blk(i), 0)),
                   pl.BlockSpec((None, 1, 2 * S5_NS), lambda i: (seq(i), 0, 0))],
        out_shape=[jax.ShapeDtypeStruct((T_ALL, S5_CH), F32),
                   jax.ShapeDtypeStruct((n_seq, 1, 2 * S5_NS), F32)],
        scratch_shapes=[pltpu.VMEM((S5_CHUNK, 2 * S5_NS), F32), pltpu.VMEM((8, 2 * S5_NS), F32),
                        pltpu.VMEM((8, 2 * S5_NS), F32), pltpu.VMEM((1, 2 * S5_NS), F32)],
        compiler_params=_cparams(("arbitrary",)),
        name="s5_bwd" if reverse else "s5_fwd",
    )(u, perm, perm_t, b_blk, c_blk, lam, lam_seg, pw, h0)


def _s5_post_kernel(u_ref, yf_ref, yb_ref, d_ref, w_ref, b_ref, o_ref):
    y = d_ref[...] * u_ref[...] + yf_ref[...] + yb_ref[...]
    cdf = 0.5 * (1.0 + jnp.tanh(math.sqrt(2.0 / math.pi) * (y + 0.044715 * (y * y * y))))
    g = y * cdf
    z = jnp.dot(g.astype(BF16), w_ref[...], preferred_element_type=F32) + b_ref[...]
    o_ref[...] = z[:, :S5_CH] * jax.nn.sigmoid(z[:, S5_CH:])


def s5_post(u, yf, yb, d, glu_w, glu_b):
    row = pl.BlockSpec((ROW_TILE, S5_CH), lambda i: (i, 0))
    return pl.pallas_call(
        _s5_post_kernel,
        grid=(T_ALL // ROW_TILE,),
        in_specs=[pl.BlockSpec((ROW_TILE, S5_CH), lambda i: (i, 3 * HY_CH // S5_CH)), row, row,
                  pl.BlockSpec((1, S5_CH), lambda i: (0, 0)),
                  pl.BlockSpec((S5_CH, 2 * S5_CH), lambda i: (0, 0)),
                  pl.BlockSpec((1, 2 * S5_CH), lambda i: (0, 0))],
        out_specs=row,
        out_shape=jax.ShapeDtypeStruct((T_ALL, S5_CH), F32),
        compiler_params=_cparams(("parallel",)),
        name="s5_post",
    )(u, yf, yb, d, glu_w, glu_b)


def _head_mask(shape):
    return lax.broadcasted_iota(jnp.int32, shape, 1) < HEAD_DIM


def _ctx_attn_kernel(q_ref, k_ref, v_ref, prev_ref, o_ref):
    del prev_ref
    low = _head_mask((CTX_LEN, 128))
    for hp in range(N_HEADS // 2):
        cols = pl.ds(hp * 128, 128)
        a = _stack_heads(q_ref[:, cols] * (HEAD_DIM ** -0.5), low)
        k = k_ref[:, cols].astype(BF16)
        v = v_ref[:, cols].astype(BF16)
        s = lax.dot_general(a, k, (((1,), (1,)), ((), ())), preferred_element_type=F32)
        m = jnp.max(s, axis=-1, keepdims=True)
        p = jnp.exp(s - m)
        den = jnp.sum(p, axis=-1, keepdims=True)
        o = jnp.dot(p.astype(BF16), v, preferred_element_type=F32) / den
        o_ref[:, cols] = jnp.where(low, o[:CTX_LEN], o[CTX_LEN:])


def _stack_heads(q, low):
    return jnp.concatenate([jnp.where(low, q, 0.0), jnp.where(low, 0.0, q)], axis=0).astype(BF16)


def ctx_attention(qkv, attn_prev):
    blk = lambda part: pl.BlockSpec((CTX_LEN, D), lambda b: (b, part))
    return pl.pallas_call(
        _ctx_attn_kernel,
        grid=(N_CTX_SEQ,),
        in_specs=[blk(0), blk(1), blk(2), pl.BlockSpec(memory_space=pl.ANY)],
        out_specs=pl.BlockSpec((CTX_LEN, D), lambda b: (b, 0)),
        out_shape=jax.ShapeDtypeStruct((T_ALL, D), F32),
        input_output_aliases={3: 0},
        compiler_params=_cparams(("parallel",)),
        name="ctx_attention",
    )(qkv, qkv, qkv, attn_prev)


NA_ROWS_PER_STEP = 8


def _na_kernel(q_ref, k_ref, v_ref, kc_ref, vc_ref, bias_ref, prev_ref, o_ref):
    del prev_ref
    rb = pl.program_id(2)
    kc = kc_ref[...]
    vc = vc_ref[...]
    low = _head_mask((GRID_W, 128))
    n_rows = LAT_LEN // GRID_W
    for i in range(NA_ROWS_PER_STEP):
        r = rb * NA_ROWS_PER_STEP + i
        rs = jnp.clip(r - WIN_R // 2, 0, n_rows - WIN_R)
        dlt = r - rs
        rows = pl.ds(pl.multiple_of(rs * GRID_W, GRID_W), WIN_R * GRID_W)
        kw = k_ref[rows, :].astype(BF16)
        vw = v_ref[rows, :].astype(BF16)
        a = _stack_heads(q_ref[i * GRID_W:(i + 1) * GRID_W, :] * (HEAD_DIM ** -0.5), low)
        s = lax.dot_general(a, kw, (((1,), (1,)), ((), ())), preferred_element_type=F32)
        s = s + bias_ref[dlt].reshape(2 * GRID_W, WIN_R * GRID_W)
        sc = lax.dot_general(a, kc, (((1,), (1,)), ((), ())), preferred_element_type=F32)
        m = jnp.maximum(jnp.max(s, axis=-1, keepdims=True), jnp.max(sc, axis=-1, keepdims=True))
        p = jnp.exp(s - m)
        pc = jnp.exp(sc - m)
        den = jnp.sum(p, axis=-1, keepdims=True) + jnp.sum(pc, axis=-1, keepdims=True)
        o = (jnp.dot(p.astype(BF16), vw, preferred_element_type=F32)
             + jnp.dot(pc.astype(BF16), vc, preferred_element_type=F32)) / den
        o_ref[i * GRID_W:(i + 1) * GRID_W, :] = jnp.where(low, o[:GRID_W], o[GRID_W:])


def _na_bias_table(rpb):
    j = np.arange(GRID_W)
    cs = np.clip(j - WIN_C // 2, 0, GRID_W - WIN_C)
    c = np.arange(GRID_W)
    inside = (c[None, :] >= cs[:, None]) & (c[None, :] < cs[:, None] + WIN_C)
    col_rel = c[None, :] - j[:, None] + WIN_C - 1
    n_rel_c = 2 * WIN_C - 1
    onehot = ((np.arange(n_rel_c)[:, None, None] == col_rel[None]) & inside[None]).astype(np.float32)
    t = jnp.dot(rpb.reshape(N_HEADS * (2 * WIN_R - 1), n_rel_c), jnp.asarray(onehot.reshape(n_rel_c, -1)),
                precision=HIGHEST)
    t = t.reshape(N_HEADS, 2 * WIN_R - 1, GRID_W, GRID_W)
    t = t + jnp.asarray(np.where(inside, 0.0, NEG_BIG).astype(np.float32))
    t = jnp.stack([t[:, WIN_R - 1 - dl:2 * WIN_R - 1 - dl] for dl in range(WIN_R)], axis=0)
    t = t.transpose(0, 1, 3, 2, 4)
    return t.reshape(WIN_R, N_HEADS, GRID_W, WIN_R * GRID_W)


def na_attention(qkv, kcb, vcb, bias, attn_prev):
    hp = N_HEADS // 2
    q_rows = NA_ROWS_PER_STEP * GRID_W
    steps = LAT_LEN // q_rows
    q_off = T_CTX // q_rows
    seq_off = T_CTX // LAT_LEN
    return pl.pallas_call(
        _na_kernel,
        grid=(N_LAT_SEQ, hp, steps),
        in_specs=[
            pl.BlockSpec((q_rows, 128), lambda b, h, r: (q_off + b * steps + r, h)),
            pl.BlockSpec((LAT_LEN, 128), lambda b, h, r: (seq_off + b, hp + h)),
            pl.BlockSpec((LAT_LEN, 128), lambda b, h, r: (seq_off + b, 2 * hp + h)),
            pl.BlockSpec((CTX_LEN, 128), lambda b, h, r: (b, h)),
            pl.BlockSpec((CTX_LEN, 128), lambda b, h, r: (b, h)),
            pl.BlockSpec((WIN_R, 2, GRID_W, WIN_R * GRID_W), lambda b, h, r: (0, h, 0, 0)),
            pl.BlockSpec(memory_space=pl.ANY),
        ],
        out_specs=pl.BlockSpec((q_rows, 128), lambda b, h, r: (q_off + b * steps + r, h)),
        out_shape=jax.ShapeDtypeStruct((T_ALL, D), F32),
        input_output_aliases={6: 0},
        compiler_params=_cparams(("parallel", "parallel", "arbitrary")),
        name="na_attention",
    )(qkv, qkv, qkv, kcb, vcb, bias, attn_prev)


def _layer_norm(r, g, b):
    mu = jnp.mean(r, axis=-1, keepdims=True)
    c = r - mu
    var = jnp.mean(c * c, axis=-1, keepdims=True)
    return c * lax.rsqrt(var + LN_EPS) * g + b


def _proj_kernel(*refs, n_in):
    a_refs = refs[:n_in]
    w_refs = refs[n_in:2 * n_in]
    (x_ref, mod_ref, g_ref, b_ref, rwh_ref, rwl_ref, rb_ref, tri_ref,
     x1_ref, h2_ref, exp_ref, pos_ref, gate_ref, cnt_ref, run_ref) = refs[2 * n_in:]

    @pl.when(pl.program_id(0) == 0)
    def _():
        run_ref[...] = jnp.zeros_like(run_ref)

    y = None
    for a_ref, w_ref in zip(a_refs, w_refs):
        part = jnp.dot(a_ref[...].astype(BF16), w_ref[...], preferred_element_type=F32)
        y = part if y is None else y + part
    m = mod_ref[...]
    x1 = _layer_norm(DN_ALPHA * x_ref[...] + m[2:3] * y, g_ref[...], b_ref[...])
    x1_ref[...] = x1
    h2 = x1 * (1.0 + m[4:5]) + m[3:4]
    h2_ref[...] = h2

    hh = h2.astype(BF16)
    hl = (h2 - hh.astype(F32)).astype(BF16)
    logits = (jnp.dot(hh, rwh_ref[...], preferred_element_type=F32)
              + jnp.dot(hh, rwl_ref[...], preferred_element_type=F32)
              + jnp.dot(hl, rwh_ref[...], preferred_element_type=F32)) + rb_ref[...]

    lane = lax.broadcasted_iota(jnp.int32, logits.shape, 1)
    work = logits
    vals, sels = [], []
    for k in range(TOP_K):
        mx = jnp.max(work, axis=-1, keepdims=True)
        idx = jnp.min(jnp.where(work == mx, lane, 128), axis=-1, keepdims=True)
        sel = lane == idx
        work = jnp.where(sel, -jnp.inf, work)
        exp_ref[:, k:k + 1] = idx
        vals.append(mx)
        sels.append(sel)
    exps = [jnp.exp(v - vals[0]) for v in vals]
    den = exps[0] + exps[1] + exps[2] + exps[3]
    for k in range(TOP_K):
        gate_ref[:, k:k + 1] = exps[k] / den

    picked = sels[0] | sels[1] | sels[2] | sels[3]
    onehot = jnp.where(picked, 1.0, 0.0)
    incl = jnp.dot(tri_ref[...], onehot.astype(BF16), preferred_element_type=F32)
    before = incl - onehot + run_ref[...]
    for k in range(TOP_K):
        pos = jnp.sum(jnp.where(sels[k], before, 0.0), axis=-1, keepdims=True)
        pos_ref[:, k:k + 1] = pos.astype(jnp.int32)
    run_ref[...] += jnp.sum(onehot, axis=0, keepdims=True)
    cnt_ref[...] = run_ref[...]


def proj_res_ln(acts, weights, x, mod, ln_g, ln_b, rw_hi, rw_lo, rb):
    n_in = len(acts)
    row = lambda c: pl.BlockSpec((ROW_TILE, c), lambda i: (i, 0))
    full = lambda shp: pl.BlockSpec(shp, lambda i: (0, 0))
    tri = jnp.asarray(np.tril(np.ones((ROW_TILE, ROW_TILE), np.float32)), dtype=BF16)
    in_specs = ([row(a.shape[1]) for a in acts] + [full(w.shape) for w in weights]
                + [row(D), pl.BlockSpec((None, 8, D), lambda i: (_mod_index(i, ROW_TILE), 0, 0)),
                   full((1, D)), full((1, D)), full((D, 128)), full((D, 128)), full((1, 128)),
                   full((ROW_TILE, ROW_TILE))])
    return pl.pallas_call(
        functools.partial(_proj_kernel, n_in=n_in),
        grid=(T_ALL // ROW_TILE,),
        in_specs=in_specs,
        out_specs=[row(D), row(D), row(TOP_K), row(TOP_K), row(TOP_K), full((1, 128))],
        out_shape=[jax.ShapeDtypeStruct((T_ALL, D), F32), jax.ShapeDtypeStruct((T_ALL, D), F32),
                   jax.ShapeDtypeStruct((T_ALL, TOP_K), jnp.int32), jax.ShapeDtypeStruct((T_ALL, TOP_K), jnp.int32),
                   jax.ShapeDtypeStruct((T_ALL, TOP_K), F32), jax.ShapeDtypeStruct((1, 128), F32)],
        scratch_shapes=[pltpu.VMEM((1, 128), F32)],
        compiler_params=_cparams(("arbitrary",)),
        name="proj_res_ln",
    )(*acts, *weights, x, mod, ln_g, ln_b, rw_hi, rw_lo, rb, tri)


def _dispatch_kernel(dest_ref, h_ref, xs_ref, sem):
    def copy(r, d):
        return pltpu.make_async_copy(h_ref.at[pl.ds(r, 1)], xs_ref.at[pl.ds(d, 1)], sem)

    def issue(r, _):
        for k in range(TOP_K):
            copy(r, dest_ref[r * TOP_K + k]).start()
        return 0

    lax.fori_loop(0, ROW_TILE, issue, 0, unroll=4)

    def drain(r, _):
        for k in range(TOP_K):
            copy(r, dest_ref[r * TOP_K + k]).wait()
        return 0

    lax.fori_loop(0, ROW_TILE, drain, 0, unroll=4)


def moe_dispatch(h2, dest_flat):
    return pl.pallas_call(
        _dispatch_kernel,
        grid=(T_ALL // ROW_TILE,),
        in_specs=[pl.BlockSpec((ROW_TILE * TOP_K,), lambda i: (i,), memory_space=pltpu.SMEM),
                  pl.BlockSpec((ROW_TILE, D), lambda i: (i, 0))],
        out_specs=pl.BlockSpec(memory_space=pl.ANY),
        out_shape=jax.ShapeDtypeStruct((N_ASSIGN, D), F32),
        scratch_shapes=[pltpu.SemaphoreType.DMA],
        compiler_params=_cparams(("arbitrary",)),
        name="moe_dispatch",
    )(dest_flat, h2)


W_CAST_ROWS = 128


def _moe_ffn_kernel(blk_ref, exp_ref, lo_ref, hi_ref, first_ref, newexp_ref,
                    x_ref, w1_ref, b1_ref, w2_ref, b2_ref, o_ref, w1b_ref, w2b_ref):
    i = pl.program_id(0)
    lo = lo_ref[i]
    hi = hi_ref[i]

    @pl.when(newexp_ref[i] == 1)
    def _():
        def cast(c, _):
            rows = pl.ds(pl.multiple_of(c * W_CAST_ROWS, W_CAST_ROWS), W_CAST_ROWS)
            w1b_ref[rows, :] = w1_ref[rows, :].astype(BF16)
            w2b_ref[rows, :] = w2_ref[rows, :].astype(BF16)
            return 0

        lax.fori_loop(0, D // W_CAST_ROWS, cast, 0)

    @pl.when(first_ref[i] == 1)
    def _():
        o_ref[...] = jnp.zeros_like(o_ref)

    @pl.when(hi > lo)
    def _():
        x = x_ref[...].astype(BF16)
        h = jnp.dot(x, w1b_ref[...], preferred_element_type=F32) + b1_ref[...]
        g = jnp.minimum(h[:, :D_FF], SWIGLU_LIMIT)
        u = jnp.clip(h[:, D_FF:], -SWIGLU_LIMIT, SWIGLU_LIMIT)
        a = g * jax.nn.sigmoid(SWIGLU_ALPHA * g) * (u + 1.0)
        y = jnp.dot(a.astype(BF16), w2b_ref[...], preferred_element_type=F32) + b2_ref[...]
        row = lax.broadcasted_iota(jnp.int32, (MOE_TILE, 1), 0)
        o_ref[...] += jnp.where((row >= lo) & (row < hi), y, 0.0)


def moe_ffn(items, xs, w1, b1, w2, b2, layer):
    assert D == D_FF
    blk, exp, lo, hi, first, newexp = items
    grid_spec = pltpu.PrefetchScalarGridSpec(
        num_scalar_prefetch=6,
        grid=(N_MOE_ITEMS,),
        in_specs=[
            pl.BlockSpec((MOE_TILE, D), lambda i, blk, exp, *_: (blk[i], 0)),
            pl.BlockSpec((None, None, D, 2 * D_FF), lambda i, blk, exp, *_: (layer, exp[i], 0, 0)),
            pl.BlockSpec((None, None, 1, 2 * D_FF), lambda i, blk, exp, *_: (layer, exp[i], 0, 0)),
            pl.BlockSpec((None, None, D_FF, D), lambda i, blk, exp, *_: (layer, exp[i], 0, 0)),
            pl.BlockSpec((None, None, 1, D), lambda i, blk, exp, *_: (layer, exp[i], 0, 0)),
        ],
        out_specs=pl.BlockSpec((MOE_TILE, D), lambda i, blk, exp, *_: (blk[i], 0)),
        scratch_shapes=[pltpu.VMEM((D, 2 * D_FF), BF16), pltpu.VMEM((D_FF, D), BF16)],
    )
    return pl.pallas_call(
        _moe_ffn_kernel,
        grid_spec=grid_spec,
        out_shape=jax.ShapeDtypeStruct((N_ASSIGN, D), F32),
        compiler_params=_cparams(("arbitrary",)),
        name="moe_ffn",
    )(blk, exp, lo, hi, first, newexp, xs, w1, b1, w2, b2)


N_COMB_STEPS = T_ALL // COMB_TILE


def _combine_kernel(dest_ref, dest_next_ref, ys_ref, gates_ref, x_ref, mod_ref, g_ref, b_ref, o_ref, buf_ref, sems):
    i = pl.program_id(0)
    slot = i % 2

    def copy(idx_ref, s, r, k):
        d = idx_ref[r * TOP_K + k]
        return pltpu.make_async_copy(ys_ref.at[pl.ds(d, 1)], buf_ref.at[s, k, pl.ds(r, 1)], sems.at[s])

    def gather(idx_ref, s):
        def issue(r, _):
            for k in range(TOP_K):
                copy(idx_ref, s, r, k).start()
            return 0

        lax.fori_loop(0, COMB_TILE, issue, 0, unroll=4)

    @pl.when(i == 0)
    def _():
        gather(dest_ref, 0)

    @pl.when(i + 1 < N_COMB_STEPS)
    def _():
        gather(dest_next_ref, 1 - slot)

    def drain(r, _):
        for k in range(TOP_K):
            copy(dest_ref, slot, r, k).wait()
        return 0

    lax.fori_loop(0, COMB_TILE, drain, 0, unroll=4)

    gates = gates_ref[...]
    y = gates[:, 0:1] * buf_ref[slot, 0]
    for k in range(1, TOP_K):
        y = y + gates[:, k:k + 1] * buf_ref[slot, k]
    m = mod_ref[...]
    o_ref[...] = _layer_norm(DN_ALPHA * x_ref[...] + m[5:6] * y, g_ref[...], b_ref[...])


def moe_combine(ys, dest_flat, gates, x1, mod, ln_g, ln_b):
    full = lambda shp: pl.BlockSpec(shp, lambda i: (0, 0))
    idx = lambda fn: pl.BlockSpec((COMB_TILE * TOP_K,), fn, memory_space=pltpu.SMEM)
    return pl.pallas_call(
        _combine_kernel,
        grid=(N_COMB_STEPS,),
        in_specs=[idx(lambda i: (i,)), idx(lambda i: (jnp.minimum(i + 1, N_COMB_STEPS - 1),)),
                  pl.BlockSpec(memory_space=pl.ANY),
                  pl.BlockSpec((COMB_TILE, TOP_K), lambda i: (i, 0)),
                  pl.BlockSpec((COMB_TILE, D), lambda i: (i, 0)),
                  pl.BlockSpec((None, 8, D), lambda i: (_mod_index(i, COMB_TILE), 0, 0)),
                  full((1, D)), full((1, D))],
        out_specs=pl.BlockSpec((COMB_TILE, D), lambda i: (i, 0)),
        out_shape=jax.ShapeDtypeStruct((T_ALL, D), F32),
        scratch_shapes=[pltpu.VMEM((2, TOP_K, COMB_TILE, D), F32), pltpu.SemaphoreType.DMA((2,))],
        compiler_params=_cparams(("arbitrary",)),
        name="moe_combine",
    )(dest_flat, dest_flat, ys, gates, x1, mod, ln_g, ln_b)


def _routing_tables(experts, pos, counts):
    counts = counts[0, :N_EXPERTS].astype(jnp.int32)
    starts = jnp.cumsum(counts) - counts
    eids = jnp.arange(N_EXPERTS, dtype=jnp.int32)
    start_of_pick = jnp.sum(jnp.where(experts[..., None] == eids, starts, 0), axis=-1)
    dest = start_of_pick + pos
    bnd = jnp.sort(jnp.concatenate([jnp.arange(N_MOE_BLOCKS + 1, dtype=jnp.int32) * MOE_TILE, starts[1:]]))
    a, b = bnd[:-1], bnd[1:]
    blk = jnp.minimum(a // MOE_TILE, N_MOE_BLOCKS - 1)
    lo = a - blk * MOE_TILE
    hi = b - blk * MOE_TILE
    exp = jnp.clip(jnp.sum((starts[None, :] <= a[:, None]).astype(jnp.int32), axis=1) - 1, 0, N_EXPERTS - 1)
    one = jnp.ones((1,), jnp.int32)
    first = jnp.concatenate([one, (blk[1:] != blk[:-1]).astype(jnp.int32)])
    newexp = jnp.concatenate([one, (exp[1:] != exp[:-1]).astype(jnp.int32)])
    items = tuple(v.astype(jnp.int32) for v in (blk, exp, lo, hi, first, newexp))
    return dest.astype(jnp.int32).reshape(-1), items


def moe_layer(h2, experts, pos, gates, counts, x1, mod, ln_g, ln_b, w1, b1, w2, b2, layer):
    dest_flat, items = _routing_tables(experts, pos, counts)
    xs = moe_dispatch(h2, dest_flat)
    ys = moe_ffn(items, xs, w1, b1, w2, b2, layer)
    return moe_combine(ys, dest_flat, gates, x1, mod, ln_g, ln_b)


def _router_operands(router_w, router_b):
    w = jnp.pad(router_w, ((0, 0), (0, 128 - N_EXPERTS)))
    hi = w.astype(BF16)
    lo = (w - hi.astype(F32)).astype(BF16)
    b = jnp.concatenate([router_b, jnp.full((128 - N_EXPERTS,), -jnp.inf, F32)])[None]
    return hi, lo, b


def kernel(x_prompt, x_sample, state_s5, cache_na_k, cache_na_v, c, c_ctx, w_mod, b_mod, ln1_g, ln1_b, ln2_g, ln2_b, ev_w_in, ev_w_out, hy_conv_w, hy_conv_b, hy_f_w1, hy_f_b1, hy_f_w2, hy_f_b2, hy_f_w3, hy_f_freq, hy_bias, s5_lam_re, s5_lam_im, s5_log_dt, s5_b_re, s5_b_im, s5_c_re, s5_c_im, s5_d, s5_glu_w, s5_glu_b, od_w_in, od_w_out, na_rpb, router_w, router_b, moe_w1, moe_b1, moe_w2, moe_b2):
    x = jnp.concatenate([x_prompt.reshape(T_CTX, D), x_sample.reshape(T_LAT, D)], axis=0)
    cvec = jnp.pad(jnp.concatenate([c_ctx[None], c], axis=0), ((0, 5), (0, 0)))

    new_state = None
    new_k = new_v = None
    for l in range(DEPTH):
        i = l // 2
        mod = matmul(cvec, w_mod, b_mod[l][None], b_layer=l, tm=8, tn=512, tk=D, precise=True, silu_a=True)
        mod = jnp.pad(mod.reshape(8, 6, D)[:3], ((0, 0), (0, 2), (0, 0)))

        if l % 2 == 0:
            u = modlinear(x, mod, ev_w_in[i].astype(BF16))
            y_hy = jnp.zeros((T_ALL, HY_CH), F32)
            for n_seq, seq_len, row_off, tile in ((N_CTX_SEQ, CTX_LEN, 0, CTX_LEN), (N_LAT_SEQ, LAT_LEN, T_CTX, 1024)):
                taps = hyena_filter_taps(seq_len, hy_f_w1[i], hy_f_b1[i], hy_f_w2[i], hy_f_b2[i], hy_f_w3[i],
                                         hy_f_freq[i])
                y_hy = hyena_group(u, hy_conv_w[i], hy_conv_b[i][None], taps, hy_bias[i][None], y_hy,
                                   n_seq=n_seq, seq_len=seq_len, row_off=row_off, tile=tile)

            h0_lat = state_s5[:, i]
            yf_parts, yb_parts, finals = [], [], []
            for r in range(2):
                tables = _s5_tables(s5_lam_re[i, r], s5_lam_im[i, r], s5_log_dt[i, r], s5_b_re[i, r], s5_b_im[i, r],
                                    s5_c_re[i, r], s5_c_im[i, r], reverse=(r == 1))
                h0_l = jnp.concatenate([h0_lat[:, r, :, :, 0].reshape(N_LAT_SEQ, 1, S5_NS),
                                        h0_lat[:, r, :, :, 1].reshape(N_LAT_SEQ, 1, S5_NS)], axis=-1)
                h0 = jnp.concatenate([jnp.zeros((N_CTX_SEQ, 1, 2 * S5_NS), F32), h0_l], axis=0)
                y_dir, fin_all = s5_direction(u, tables, h0, reverse=(r == 1))
                (yf_parts if r == 0 else yb_parts).append(y_dir)
                fin = fin_all[:N_CTX_SEQ].reshape(N_CTX_SEQ, 2, S5_GROUPS, S5_STATE)
                finals.append(jnp.stack([fin[:, 0], fin[:, 1]], axis=-1))
            new_state = jnp.stack(finals, axis=1)[:, None]
            y_s5 = s5_post(u, yf_parts[0], yb_parts[0], s5_d[i].reshape(1, S5_CH), s5_glu_w[i].astype(BF16),
                           s5_glu_b[i][None])
            acts = [y_hy, y_s5]
            w_out = ev_w_out[i].astype(BF16)
            weights = [w_out[:HY_CH], w_out[HY_CH:]]
        else:
            qkv = modlinear(x, mod, od_w_in[i].astype(BF16))
            new_k = qkv[:T_CTX, D:2 * D].reshape(N_CTX_SEQ, 1, CTX_LEN, N_HEADS, HEAD_DIM)
            new_v = qkv[:T_CTX, 2 * D:].reshape(N_CTX_SEQ, 1, CTX_LEN, N_HEADS, HEAD_DIM)
            attn = ctx_attention(qkv, jnp.zeros((T_ALL, D), F32))
            kcb = cache_na_k[:, i].reshape(N_LAT_SEQ * CTX_LEN, D).astype(BF16)
            vcb = cache_na_v[:, i].reshape(N_LAT_SEQ * CTX_LEN, D).astype(BF16)
            attn = na_attention(qkv, kcb, vcb, _na_bias_table(na_rpb[i]), attn)
            acts = [attn]
            weights = [od_w_out[i].astype(BF16)]

        rw_hi, rw_lo, rb = _router_operands(router_w[l], router_b[l])
        x1, h2, experts, pos, gates, counts = proj_res_ln(acts, weights, x, mod, ln1_g[l][None], ln1_b[l][None],
                                                          rw_hi, rw_lo, rb)
        x = moe_layer(h2, experts, pos, gates, counts, x1, mod, ln2_g[l][None], ln2_b[l][None],
                      moe_w1, moe_b1[:, :, None, :], moe_w2, moe_b2[:, :, None, :], l)

    y_prompt = x[:T_CTX].reshape(N_CTX_SEQ, CTX_LEN, D)
    y_sample = x[T_CTX:].reshape(N_LAT_SEQ, LAT_LEN, D)
    return (y_prompt, y_sample, new_state, new_k, new_v)
```

```python
import functools
import math

import jax
import jax.numpy as jnp
import numpy as np
from jax import lax
from jax.experimental import pallas as pl
from jax.experimental.pallas import tpu as pltpu

F32 = jnp.float32
BF16 = jnp.bfloat16
HIGHEST = lax.Precision.HIGHEST

D = 1024
N_CTX_SEQ, CTX_LEN = 32, 256
N_LAT_SEQ, LAT_LEN = 2, 4096
T_CTX = N_CTX_SEQ * CTX_LEN
T_LAT = N_LAT_SEQ * LAT_LEN
T_ALL = T_CTX + T_LAT
DEPTH = 2
HY_CH = 512
S5_CH = 512
S5_GROUPS, S5_GROUP, S5_STATE = 32, 16, 64
S5_NS = S5_GROUPS * S5_STATE
HY_EMB, HY_BANDS, HY_ORDER = 33, 16, 64
N_HEADS, HEAD_DIM = 16, 64
GRID_W, WIN_R, WIN_C = 64, 8, 16
N_EXPERTS, TOP_K, D_FF = 32, 4, 1024
SWIGLU_LIMIT, SWIGLU_ALPHA = 7.0, 1.702
LN_EPS = 1e-5
DN_ALPHA = (2 * DEPTH) ** 0.25
NEG_BIG = -1e30

ROW_TILE = 256
MOE_TILE = 256
N_ASSIGN = T_ALL * TOP_K
N_MOE_BLOCKS = N_ASSIGN // MOE_TILE
N_MOE_ITEMS = N_MOE_BLOCKS + N_EXPERTS - 1
COMB_TILE = 128
ROW_AS_TILE = (8, D // 8)
S5_CHUNK = 256
S5_SEG = S5_CHUNK // 8
S5_COLS = 512
VMEM_LIMIT = 56 * 1024 * 1024


def _cparams(sem, vmem=None):
    return pltpu.CompilerParams(dimension_semantics=sem, vmem_limit_bytes=vmem or VMEM_LIMIT)


def _mod_index(i, rows_per_tile):
    n_ctx = T_CTX // rows_per_tile
    per_lat = LAT_LEN // rows_per_tile
    return jnp.where(i < n_ctx, 0, 1 + (i - n_ctx) // per_lat)


def _mm_kernel(a_ref, b_ref, bias_ref, o_ref, acc_ref, *, nk, precise, silu_a):
    k = pl.program_id(2)

    @pl.when(k == 0)
    def _():
        acc_ref[...] = jnp.zeros_like(acc_ref)

    a = a_ref[...]
    if silu_a:
        a = a * jax.nn.sigmoid(a)
    if precise:
        acc_ref[...] += jnp.dot(a, b_ref[...], preferred_element_type=F32, precision=HIGHEST)
    else:
        acc_ref[...] += jnp.dot(a.astype(BF16), b_ref[...].astype(BF16), preferred_element_type=F32)

    @pl.when(k == nk - 1)
    def _():
        o_ref[...] = acc_ref[...] + bias_ref[...]


def matmul(a, b, bias=None, *, tm, tn, tk, b_layer=None, precise=False, silu_a=False):
    kdim, n = b.shape[-2:]
    m = a.shape[0]
    if bias is None:
        bias = jnp.zeros((1, n), F32)
    nk = kdim // tk
    if b.ndim == 3:
        b_spec = pl.BlockSpec((None, tk, tn), lambda i, j, k: (b_layer, k, j))
    else:
        b_spec = pl.BlockSpec((tk, tn), lambda i, j, k: (k, j))
    return pl.pallas_call(
        functools.partial(_mm_kernel, nk=nk, precise=precise, silu_a=silu_a),
        grid=(m // tm, n // tn, nk),
        in_specs=[pl.BlockSpec((tm, tk), lambda i, j, k: (i, k)), b_spec,
                  pl.BlockSpec((1, tn), lambda i, j, k: (0, j))],
        out_specs=pl.BlockSpec((tm, tn), lambda i, j, k: (i, j)),
        out_shape=jax.ShapeDtypeStruct((m, n), F32),
        scratch_shapes=[pltpu.VMEM((tm, tn), F32)],
        compiler_params=_cparams(("parallel", "parallel", "arbitrary")),
        name="matmul",
    )(a, b, bias)


def _modlinear_kernel(x_ref, mod_ref, w_ref, o_ref):
    m = mod_ref[...]
    h = x_ref[...] * (1.0 + m[1:2]) + m[0:1]
    o_ref[...] = jnp.dot(h.astype(BF16), w_ref[...], preferred_element_type=F32)


def modlinear(x, mod, w_bf16):
    n = w_bf16.shape[1]
    return pl.pallas_call(
        _modlinear_kernel,
        grid=(T_ALL // ROW_TILE,),
        in_specs=[
            pl.BlockSpec((ROW_TILE, D), lambda i: (i, 0)),
            pl.BlockSpec((None, 8, D), lambda i: (_mod_index(i, ROW_TILE), 0, 0)),
            pl.BlockSpec((D, n), lambda i: (0, 0)),
        ],
        out_specs=pl.BlockSpec((ROW_TILE, n), lambda i: (i, 0)),
        out_shape=jax.ShapeDtypeStruct((T_ALL, n), F32),
        compiler_params=_cparams(("parallel",)),
        name="modlinear",
    )(x, mod, w_bf16)


def _filter_kernel(z_ref, w1_ref, b1_ref, w2_ref, b2_ref, w3_ref, fq_ref, dl_ref, hsum_ref, hdiff_ref, *, tile):
    z = z_ref[...]
    fq = fq_ref[...]
    h = jnp.sin(fq * (jnp.dot(z, w1_ref[...], preferred_element_type=F32, precision=HIGHEST) + b1_ref[...]))
    h = jnp.sin(fq * (jnp.dot(h, w2_ref[...], preferred_element_type=F32, precision=HIGHEST) + b2_ref[...]))
    h = jnp.dot(h, w3_ref[...], preferred_element_type=F32, precision=HIGHEST)
    decay = jnp.exp(-z[:, 0:1] * dl_ref[...])
    hf = h[:, :HY_CH] * decay
    hb = h[:, HY_CH:] * decay
    row = lax.broadcasted_iota(jnp.int32, (tile, 1), 0) + pl.program_id(0) * tile
    hsum_ref[...] = hf + hb
    hdiff_ref[...] = jnp.where(row == 0, hf + hb, hf - hb)


def hyena_filter_taps(seq_len, w1, b1, w2, b2, w3, freq):
    t = jnp.linspace(0.0, 1.0, seq_len, dtype=F32)[:, None]
    w = 2.0 * math.pi * jnp.arange(seq_len, dtype=F32)[:, None] / seq_len
    f = jnp.linspace(1e-4, HY_BANDS - 1, HY_BANDS, dtype=F32)[None, :]
    z = jnp.concatenate([t, jnp.cos(f * w), -jnp.sin(f * w)], -1)
    z = jnp.pad(z, ((0, 0), (0, 128 - HY_EMB)))
    pad_o = 128 - HY_ORDER
    w1p = jnp.pad(w1, ((0, 128 - HY_EMB), (0, pad_o)))
    w2p = jnp.pad(w2, ((0, pad_o), (0, pad_o)))
    w3p = jnp.pad(w3, ((0, pad_o), (0, 0)))
    b1p = jnp.pad(b1, (0, pad_o))[None]
    b2p = jnp.pad(b2, (0, pad_o))[None]
    fqp = jnp.pad(freq, (0, pad_o))[None]
    max_decay = math.log(1e-2) / 0.3
    min_decay = math.log(1e-2) / 1.5
    absdelta = jnp.abs(jnp.linspace(min_decay, max_decay, HY_CH, dtype=F32))[None]
    tile = 256
    full = lambda shp: pl.BlockSpec(shp, lambda i: (0, 0))
    return pl.pallas_call(
        functools.partial(_filter_kernel, tile=tile),
        grid=(seq_len // tile,),
        in_specs=[pl.BlockSpec((tile, 128), lambda i: (i, 0)), full((128, 128)), full((1, 128)), full((128, 128)),
                  full((1, 128)), full((128, 2 * HY_CH)), full((1, 128)), full((1, HY_CH))],
        out_specs=[pl.BlockSpec((tile, HY_CH), lambda i: (i, 0))] * 2,
        out_shape=[jax.ShapeDtypeStruct((seq_len, HY_CH), F32)] * 2,
        compiler_params=_cparams(("parallel",)),
        name="hyena_filter",
    )(z, w1p, b1p, w2p, b2p, w3p, fqp, absdelta)


def _dft_tables(seq_len):
    s = int(round(math.sqrt(seq_len)))
    j = jnp.arange(s, dtype=jnp.int32)[:, None]
    c = jnp.arange(seq_len, dtype=jnp.int32)[None, :]
    unit = math.pi / (2 * seq_len)

    def cos_sin(idx):
        ang = (idx % (4 * seq_len)).astype(F32) * unit
        return jnp.cos(ang), jnp.sin(ang)

    def expand(ca, sa, cb, sb):
        cos = ca[:, None, :] * cb[None, :, :] - sa[:, None, :] * sb[None, :, :]
        nsin = -(sa[:, None, :] * cb[None, :, :] + ca[:, None, :] * sb[None, :, :])
        return cos.reshape(seq_len, seq_len).astype(BF16), nsin.reshape(seq_len, seq_len).astype(BF16)

    cm, nsm = expand(*cos_sin((2 * s * j) * c), *cos_sin((2 * j + 1) * c))
    cmt, nsmt = expand(*cos_sin((2 * c + 1) * (s * j)), *cos_sin((2 * c + 1) * j))
    return cm, nsm, cmt, nsmt


def _hyena_pre_kernel(u0_ref, u1_ref, u2_ref, w0_ref, w1_ref, w2_ref, b0_ref, b1_ref, b2_ref,
                      vx_ref, vxb_ref, x0_ref, *, seq_len):
    row = lax.broadcasted_iota(jnp.int32, (seq_len, 1), 0)

    def short_conv(u_ref, w_ref, b_ref):
        a = u_ref[...]
        w = w_ref[...]
        prev = jnp.where(row == 0, 0.0, pltpu.roll(a, 1, 0))
        nxt = jnp.where(row == seq_len - 1, 0.0, pltpu.roll(a, seq_len - 1, 0))
        return prev * w[0:1] + a * w[1:2] + nxt * w[2:3] + b_ref[...]

    x0 = short_conv(u0_ref, w0_ref, b0_ref)
    x1 = short_conv(u1_ref, w1_ref, b1_ref)
    v = short_conv(u2_ref, w2_ref, b2_ref)
    vx = v * x1
    vx_ref[...] = vx
    vxb_ref[...] = vx.astype(BF16)
    x0_ref[...] = x0


def hyena_pre(u, conv_w, conv_b, *, n_seq, seq_len, row_off):
    cb = min(HY_CH, 128 * LAT_LEN // seq_len)
    ncb = HY_CH // cb
    rb0 = row_off // seq_len
    uspec = lambda part: pl.BlockSpec((seq_len, cb), lambda b, j: (rb0 + b, part * ncb + j))
    wspec = lambda part: pl.BlockSpec((3, cb), lambda b, j: (0, part * ncb + j))
    bspec = lambda part: pl.BlockSpec((1, cb), lambda b, j: (0, part * ncb + j))
    ospec = pl.BlockSpec((seq_len, cb), lambda b, j: (b, j))
    rows = n_seq * seq_len
    return pl.pallas_call(
        functools.partial(_hyena_pre_kernel, seq_len=seq_len),
        grid=(n_seq, ncb),
        in_specs=[uspec(0), uspec(1), uspec(2), wspec(0), wspec(1), wspec(2), bspec(0), bspec(1), bspec(2)],
        out_specs=[ospec, ospec, ospec],
        out_shape=[jax.ShapeDtypeStruct((rows, HY_CH), F32), jax.ShapeDtypeStruct((rows, HY_CH), BF16),
                   jax.ShapeDtypeStruct((rows, HY_CH), F32)],
        compiler_params=_cparams(("parallel", "parallel")),
        name="hyena_pre",
    )(u, u, u, conv_w, conv_w, conv_w, conv_b, conv_b, conv_b)


def _dft_fwd_kernel(cm_ref, nsm_ref, v_ref, hr_ref, hi_ref, zr_ref, zi_ref, accr_ref, acci_ref, *, nk):
    k = pl.program_id(3)

    @pl.when(k == 0)
    def _():
        accr_ref[...] = jnp.zeros_like(accr_ref)
        acci_ref[...] = jnp.zeros_like(acci_ref)

    v = v_ref[...]
    accr_ref[...] += jnp.dot(cm_ref[...], v, preferred_element_type=F32)
    acci_ref[...] += jnp.dot(nsm_ref[...], v, preferred_element_type=F32)

    @pl.when(k == nk - 1)
    def _():
        xr, xi = accr_ref[...], acci_ref[...]
        hr, hi = hr_ref[...], hi_ref[...]
        zr_ref[...] = (xr * hr - xi * hi).astype(BF16)
        zi_ref[...] = (xr * hi + xi * hr).astype(BF16)


def dft_fwd(cm, nsm, vxb, hr, hi, *, n_seq, seq_len, tile):
    nk = seq_len // tile
    tn = HY_CH
    v3 = vxb.reshape(n_seq, seq_len, HY_CH)
    zspec = pl.BlockSpec((None, tile, tn), lambda s, i, j, k: (s, i, j))
    return pl.pallas_call(
        functools.partial(_dft_fwd_kernel, nk=nk),
        grid=(n_seq, seq_len // tile, HY_CH // tn, nk),
        in_specs=[
            pl.BlockSpec((tile, tile), lambda s, i, j, k: (i, k)),
            pl.BlockSpec((tile, tile), lambda s, i, j, k: (i, k)),
            pl.BlockSpec((None, tile, tn), lambda s, i, j, k: (s, k, j)),
            pl.BlockSpec((tile, tn), lambda s, i, j, k: (i, j)),
            pl.BlockSpec((tile, tn), lambda s, i, j, k: (i, j)),
        ],
        out_specs=[zspec, zspec],
        out_shape=[jax.ShapeDtypeStruct((n_seq, seq_len, HY_CH), BF16)] * 2,
        scratch_shapes=[pltpu.VMEM((tile, tn), F32), pltpu.VMEM((tile, tn), F32)],
        compiler_params=_cparams(("parallel", "parallel", "parallel", "arbitrary")),
        name="dft_fwd",
    )(cm, nsm, v3, hr, hi)


def _dft_inv_kernel(cmt_ref, nsmt_ref, zr_ref, zi_ref, vx_ref, x0_ref, bias_ref, prev_ref, o_ref, acc_ref, *,
                    nk, inv_len):
    del prev_ref
    k = pl.program_id(3)

    @pl.when(k == 0)
    def _():
        acc_ref[...] = jnp.zeros_like(acc_ref)

    acc_ref[...] += (jnp.dot(cmt_ref[...], zr_ref[...], preferred_element_type=F32)
                     + jnp.dot(nsmt_ref[...], zi_ref[...], preferred_element_type=F32))

    @pl.when(k == nk - 1)
    def _():
        conv = acc_ref[...] * inv_len
        o_ref[...] = (conv + vx_ref[...] * bias_ref[...]) * x0_ref[...]


def dft_inv(cmt, nsmt, zr, zi, vx, x0, bias, y_prev, *, n_seq, seq_len, row_off, tile):
    nk = seq_len // tile
    tn = HY_CH
    per_seq = seq_len // tile
    rb0 = row_off // tile
    zspec = pl.BlockSpec((None, tile, tn), lambda s, i, j, k: (s, k, j))
    espec = pl.BlockSpec((None, tile, tn), lambda s, i, j, k: (s, i, j))
    return pl.pallas_call(
        functools.partial(_dft_inv_kernel, nk=nk, inv_len=1.0 / seq_len),
        grid=(n_seq, per_seq, HY_CH // tn, nk),
        in_specs=[
            pl.BlockSpec((tile, tile), lambda s, i, j, k: (i, k)),
            pl.BlockSpec((tile, tile), lambda s, i, j, k: (i, k)),
            zspec, zspec, espec, espec,
            pl.BlockSpec((1, tn), lambda s, i, j, k: (0, j)),
            pl.BlockSpec(memory_space=pl.ANY),
        ],
        out_specs=pl.BlockSpec((tile, tn), lambda s, i, j, k: (rb0 + s * per_seq + i, j)),
        out_shape=jax.ShapeDtypeStruct((T_ALL, HY_CH), F32),
        scratch_shapes=[pltpu.VMEM((tile, tn), F32)],
        input_output_aliases={7: 0},
        compiler_params=_cparams(("parallel", "parallel", "parallel", "arbitrary")),
        name="dft_inv",
    )(cmt, nsmt, zr, zi, vx.reshape(n_seq, seq_len, HY_CH), x0.reshape(n_seq, seq_len, HY_CH), bias, y_prev)


def hyena_group(u, conv_w, conv_b, taps, bias, y_prev, *, n_seq, seq_len, row_off, tile):
    hsum, hdiff = taps
    cm, nsm, cmt, nsmt = _dft_tables(seq_len)
    hr = matmul(cm, hsum, tm=tile, tn=HY_CH, tk=tile)
    hi = matmul(nsm, hdiff, tm=tile, tn=HY_CH, tk=tile)
    vx, vxb, x0 = hyena_pre(u, conv_w, conv_b, n_seq=n_seq, seq_len=seq_len, row_off=row_off)
    zr, zi = dft_fwd(cm, nsm, vxb, hr, hi, n_seq=n_seq, seq_len=seq_len, tile=tile)
    return dft_inv(cmt, nsmt, zr, zi, vx, x0, bias, y_prev, n_seq=n_seq, seq_len=seq_len, row_off=row_off, tile=tile)


def _s5_tables(lam_re, lam_im, log_dt, b_re, b_im, c_re, c_im, reverse):
    dt = jnp.exp(log_dt)[:, None]
    a = lam_re * dt
    b = lam_im * dt
    mag = jnp.exp(a)
    lbr, lbi = mag * jnp.cos(b), mag * jnp.sin(b)
    den = lam_re * lam_re + lam_im * lam_im
    qr = ((lbr - 1.0) * lam_re + lbi * lam_im) / den
    qi = (lbi * lam_re - (lbr - 1.0) * lam_im) / den
    bbr = qr[..., None] * b_re - qi[..., None] * b_im
    bbi = qr[..., None] * b_im + qi[..., None] * b_re
    eye = jnp.eye(S5_GROUPS, dtype=F32)

    def in_block(m):
        return jnp.einsum("gnc,gh->gchn", m, eye).reshape(S5_CH, S5_NS)

    def out_block(m):
        return jnp.einsum("gcn,gh->gnhc", m, eye).reshape(S5_NS, S5_CH)

    b_blk = jnp.concatenate([in_block(bbr), in_block(bbi)], axis=1).astype(BF16)
    c_blk = jnp.concatenate([out_block(c_re), out_block(-c_im)], axis=0).astype(BF16)
    flat = lambda m: m.reshape(1, S5_NS)
    lam = jnp.concatenate([flat(lbr), flat(lbi)], axis=1)
    steps = jnp.arange(1, S5_SEG + 1, dtype=F32)
    if reverse:
        steps = steps[::-1]
    pa = a.reshape(1, S5_NS) * steps[:, None]
    pb = b.reshape(1, S5_NS) * steps[:, None]
    pw = jnp.concatenate([jnp.exp(pa) * jnp.cos(pb), jnp.exp(pa) * jnp.sin(pb)], axis=1)
    sa, sb = flat(a) * S5_SEG, flat(b) * S5_SEG
    lam_seg = jnp.concatenate([jnp.exp(sa) * jnp.cos(sb), jnp.exp(sa) * jnp.sin(sb)], axis=1)
    return b_blk, c_blk, lam, lam_seg, pw


N_S5_CTX_STEPS = T_CTX // S5_CHUNK
N_S5_LAT_CHUNKS = LAT_LEN // S5_CHUNK
N_S5_STEPS = T_ALL // S5_CHUNK


def _s5_step_info(i, reverse):
    k = jnp.maximum(i - N_S5_CTX_STEPS, 0)
    b, j = k // N_S5_LAT_CHUNKS, k % N_S5_LAT_CHUNKS
    chunk = (N_S5_LAT_CHUNKS - 1 - j) if reverse else j
    is_ctx = i < N_S5_CTX_STEPS
    row_blk = jnp.where(is_ctx, i, N_S5_CTX_STEPS + b * N_S5_LAT_CHUNKS + chunk)
    seq = jnp.where(is_ctx, i, N_CTX_SEQ + b)
    return row_blk, seq, is_ctx | (j == 0), is_ctx | (j == N_S5_LAT_CHUNKS - 1)


def _s5_kernel(u_ref, perm_ref, permt_ref, bblk_ref, cblk_ref, lam_ref, lseg_ref, pw_ref, h0_ref, y_ref, fin_ref,
               bu_ref, hend_ref, cin_ref, carry_ref, *, reverse):
    _, _, is_first, is_last = _s5_step_info(pl.program_id(0), reverse)
    ns = S5_NS

    @pl.when(is_first)
    def _():
        carry_ref[...] = h0_ref[...]

    u_seg = jnp.dot(perm_ref[...], u_ref[...].astype(BF16), preferred_element_type=F32).astype(BF16)
    bu_ref[...] = jnp.dot(u_seg, bblk_ref[...], preferred_element_type=F32)

    for cb in range(ns // S5_COLS):
        re_cols = pl.ds(cb * S5_COLS, S5_COLS)
        im_cols = pl.ds(ns + cb * S5_COLS, S5_COLS)
        lr = jnp.broadcast_to(lam_ref[:, re_cols], (8, S5_COLS))
        li = jnp.broadcast_to(lam_ref[:, im_cols], (8, S5_COLS))

        def step(kk, carry, re_cols=re_cols, im_cols=im_cols, lr=lr, li=li):
            hr, hi = carry
            k = (S5_SEG - 1 - kk) if reverse else kk
            rows = pl.ds(pl.multiple_of(k * 8, 8), 8)
            nr = lr * hr - li * hi + bu_ref[rows, re_cols]
            ni = lr * hi + li * hr + bu_ref[rows, im_cols]
            bu_ref[rows, re_cols] = nr
            bu_ref[rows, im_cols] = ni
            return nr, ni

        zero = jnp.zeros((8, S5_COLS), F32)
        hr, hi = lax.fori_loop(0, S5_SEG, step, (zero, zero), unroll=4)
        hend_ref[:, re_cols] = hr
        hend_ref[:, im_cols] = hi

    cr = carry_ref[:, :ns]
    ci = carry_ref[:, ns:]
    lsr = lseg_ref[:, :ns]
    lsi = lseg_ref[:, ns:]
    for p in range(8):
        s = 7 - p if reverse else p
        cin_ref[s:s + 1, :ns] = cr
        cin_ref[s:s + 1, ns:] = ci
        er = hend_ref[s:s + 1, :ns]
        ei = hend_ref[s:s + 1, ns:]
        cr, ci = er + lsr * cr - lsi * ci, ei + lsr * ci + lsi * cr
    carry_ref[:, :ns] = cr
    carry_ref[:, ns:] = ci

    for cb in range(ns // S5_COLS):
        re_cols = pl.ds(cb * S5_COLS, S5_COLS)
        im_cols = pl.ds(ns + cb * S5_COLS, S5_COLS)
        cinr = cin_ref[:, re_cols]
        cini = cin_ref[:, im_cols]

        def fix(k, _, re_cols=re_cols, im_cols=im_cols, cinr=cinr, cini=cini):
            rows = pl.ds(pl.multiple_of(k * 8, 8), 8)
            pr = pw_ref[pl.ds(k, 1), re_cols]
            pi = pw_ref[pl.ds(k, 1), im_cols]
            bu_ref[rows, re_cols] += pr * cinr - pi * cini
            bu_ref[rows, im_cols] += pr * cini + pi * cinr
            return 0

        lax.fori_loop(0, S5_SEG, fix, 0, unroll=4)

    y_seg = jnp.dot(bu_ref[...].astype(BF16), cblk_ref[...], preferred_element_type=F32)
    y_hi = y_seg.astype(BF16)
    y_lo = (y_seg - y_hi.astype(F32)).astype(BF16)
    y_ref[...] = (jnp.dot(permt_ref[...], y_hi, preferred_element_type=F32)
                  + jnp.dot(permt_ref[...], y_lo, preferred_element_type=F32))

    @pl.when(is_last)
    def _():
        fin_ref[...] = carry_ref[...]


def s5_direction(u, tables, h0, *, reverse):
    b_blk, c_blk, lam, lam_seg, pw = tables
    s5_col = 3 * HY_CH // S5_CH
    n_seq = N_CTX_SEQ + N_LAT_SEQ
    row_blk = lambda i: _s5_step_info(i, reverse)[0]
    seq = lambda i: _s5_step_info(i, reverse)[1]
    const = lambda shp: pl.BlockSpec(shp, lambda i: (0, 0))
    t = np.arange(S5_CHUNK)
    perm_np = np.zeros((S5_CHUNK, S5_CHUNK), np.float32)
    perm_np[(t % S5_SEG) * 8 + t // S5_SEG, t] = 1.0
    perm = jnp.asarray(perm_np, dtype=BF16)
    perm_t = jnp.asarray(perm_np.T, dtype=BF16)
    return pl.pallas_call(
        functools.partial(_s5_kernel, reverse=reverse),
        grid=(N_S5_STEPS,),
        in_specs=[
            pl.BlockSpec((S5_CHUNK, S5_CH), lambda i: (row_blk(i), s5_col)),
            const((S5_CHUNK, S5_CHUNK)), const((S5_CHUNK, S5_CHUNK)),
            const((S5_CH, 2 * S5_NS)), const((2 * S5_NS, S5_CH)), const((1, 2 * S5_NS)), const((1, 2 * S5_NS)),
            const((S5_SEG, 2 * S5_NS)),
            pl.BlockSpec((None, 1, 2 * S5_NS), lambda i: (seq(i), 0, 0)),
        ],
        out_specs=[pl.BlockSpec((S5_CHUNK, S5_CH), lambda i: (row_blk(i), 0)),
                   pl.BlockSpec((None, 1, 2 * S5_NS), lambda i: (seq(i), 0, 0))],
        out_shape=[jax.ShapeDtypeStruct((T_ALL, S5_CH), F32),
                   jax.ShapeDtypeStruct((n_seq, 1, 2 * S5_NS), F32)],
        scratch_shapes=[pltpu.VMEM((S5_CHUNK, 2 * S5_NS), F32), pltpu.VMEM((8, 2 * S5_NS), F32),
                        pltpu.VMEM((8, 2 * S5_NS), F32), pltpu.VMEM((1, 2 * S5_NS), F32)],
        compiler_params=_cparams(("arbitrary",)),
        name="s5_bwd" if reverse else "s5_fwd",
    )(u, perm, perm_t, b_blk, c_blk, lam, lam_seg, pw, h0)


def _s5_post_kernel(u_ref, yf_ref, yb_ref, d_ref, w_ref, b_ref, o_ref):
    y = d_ref[...] * u_ref[...] + yf_ref[...] + yb_ref[...]
    cdf = 0.5 * (1.0 + jnp.tanh(math.sqrt(2.0 / math.pi) * (y + 0.044715 * (y * y * y))))
    g = y * cdf
    z = jnp.dot(g.astype(BF16), w_ref[...], preferred_element_type=F32) + b_ref[...]
    o_ref[...] = z[:, :S5_CH] * jax.nn.sigmoid(z[:, S5_CH:])


def s5_post(u, yf, yb, d, glu_w, glu_b):
    row = pl.BlockSpec((ROW_TILE, S5_CH), lambda i: (i, 0))
    return pl.pallas_call(
        _s5_post_kernel,
        grid=(T_ALL // ROW_TILE,),
        in_specs=[pl.BlockSpec((ROW_TILE, S5_CH), lambda i: (i, 3 * HY_CH // S5_CH)), row, row,
                  pl.BlockSpec((1, S5_CH), lambda i: (0, 0)),
                  pl.BlockSpec((S5_CH, 2 * S5_CH), lambda i: (0, 0)),
                  pl.BlockSpec((1, 2 * S5_CH), lambda i: (0, 0))],
        out_specs=row,
        out_shape=jax.ShapeDtypeStruct((T_ALL, S5_CH), F32),
        compiler_params=_cparams(("parallel",)),
        name="s5_post",
    )(u, yf, yb, d, glu_w, glu_b)


def _head_mask(shape):
    return lax.broadcasted_iota(jnp.int32, shape, 1) < HEAD_DIM


def _ctx_attn_kernel(q_ref, k_ref, v_ref, prev_ref, o_ref):
    del prev_ref
    low = _head_mask((CTX_LEN, 128))
    for hp in range(N_HEADS // 2):
        cols = pl.ds(hp * 128, 128)
        a = _stack_heads(q_ref[:, cols] * (HEAD_DIM ** -0.5), low)
        k = k_ref[:, cols].astype(BF16)
        v = v_ref[:, cols].astype(BF16)
        s = lax.dot_general(a, k, (((1,), (1,)), ((), ())), preferred_element_type=F32)
        m = jnp.max(s, axis=-1, keepdims=True)
        p = jnp.exp(s - m)
        den = jnp.sum(p, axis=-1, keepdims=True)
        o = jnp.dot(p.astype(BF16), v, preferred_element_type=F32) / den
        o_ref[:, cols] = jnp.where(low, o[:CTX_LEN], o[CTX_LEN:])


def _stack_heads(q, low):
    return jnp.concatenate([jnp.where(low, q, 0.0), jnp.where(low, 0.0, q)], axis=0).astype(BF16)


def ctx_attention(qkv, attn_prev):
    blk = lambda part: pl.BlockSpec((CTX_LEN, D), lambda b: (b, part))
    return pl.pallas_call(
        _ctx_attn_kernel,
        grid=(N_CTX_SEQ,),
        in_specs=[blk(0), blk(1), blk(2), pl.BlockSpec(memory_space=pl.ANY)],
        out_specs=pl.BlockSpec((CTX_LEN, D), lambda b: (b, 0)),
        out_shape=jax.ShapeDtypeStruct((T_ALL, D), F32),
        input_output_aliases={3: 0},
        compiler_params=_cparams(("parallel",)),
        name="ctx_attention",
    )(qkv, qkv, qkv, attn_prev)


NA_ROWS_PER_STEP = 8


def _na_kernel(q_ref, k_ref, v_ref, kc_ref, vc_ref, bias_ref, prev_ref, o_ref):
    del prev_ref
    rb = pl.program_id(2)
    kc = kc_ref[...]
    vc = vc_ref[...]
    low = _head_mask((GRID_W, 128))
    n_rows = LAT_LEN // GRID_W
    for i in range(NA_ROWS_PER_STEP):
        r = rb * NA_ROWS_PER_STEP + i
        rs = jnp.clip(r - WIN_R // 2, 0, n_rows - WIN_R)
        dlt = r - rs
        rows = pl.ds(pl.multiple_of(rs * GRID_W, GRID_W), WIN_R * GRID_W)
        kw = k_ref[rows, :].astype(BF16)
        vw = v_ref[rows, :].astype(BF16)
        a = _stack_heads(q_ref[i * GRID_W:(i + 1) * GRID_W, :] * (HEAD_DIM ** -0.5), low)
        s = lax.dot_general(a, kw, (((1,), (1,)), ((), ())), preferred_element_type=F32)
        s = s + bias_ref[dlt].reshape(2 * GRID_W, WIN_R * GRID_W)
        sc = lax.dot_general(a, kc, (((1,), (1,)), ((), ())), preferred_element_type=F32)
        m = jnp.maximum(jnp.max(s, axis=-1, keepdims=True), jnp.max(sc, axis=-1, keepdims=True))
        p = jnp.exp(s - m)
        pc = jnp.exp(sc - m)
        den = jnp.sum(p, axis=-1, keepdims=True) + jnp.sum(pc, axis=-1, keepdims=True)
        o = (jnp.dot(p.astype(BF16), vw, preferred_element_type=F32)
             + jnp.dot(pc.astype(BF16), vc, preferred_element_type=F32)) / den
        o_ref[i * GRID_W:(i + 1) * GRID_W, :] = jnp.where(low, o[:GRID_W], o[GRID_W:])


def _na_bias_table(rpb):
    j = np.arange(GRID_W)
    cs = np.clip(j - WIN_C // 2, 0, GRID_W - WIN_C)
    c = np.arange(GRID_W)
    inside = (c[None, :] >= cs[:, None]) & (c[None, :] < cs[:, None] + WIN_C)
    col_rel = c[None, :] - j[:, None] + WIN_C - 1
    n_rel_c = 2 * WIN_C - 1
    onehot = ((np.arange(n_rel_c)[:, None, None] == col_rel[None]) & inside[None]).astype(np.float32)
    t = jnp.dot(rpb.reshape(N_HEADS * (2 * WIN_R - 1), n_rel_c), jnp.asarray(onehot.reshape(n_rel_c, -1)),
                precision=HIGHEST)
    t = t.reshape(N_HEADS, 2 * WIN_R - 1, GRID_W, GRID_W)
    t = t + jnp.asarray(np.where(inside, 0.0, NEG_BIG).astype(np.float32))
    t = jnp.stack([t[:, WIN_R - 1 - dl:2 * WIN_R - 1 - dl] for dl in range(WIN_R)], axis=0)
    t = t.transpose(0, 1, 3, 2, 4)
    return t.reshape(WIN_R, N_HEADS, GRID_W, WIN_R * GRID_W)


def na_attention(qkv, kcb, vcb, bias, attn_prev):
    hp = N_HEADS // 2
    q_rows = NA_ROWS_PER_STEP * GRID_W
    steps = LAT_LEN // q_rows
    q_off = T_CTX // q_rows
    seq_off = T_CTX // LAT_LEN
    return pl.pallas_call(
        _na_kernel,
        grid=(N_LAT_SEQ, hp, steps),
        in_specs=[
            pl.BlockSpec((q_rows, 128), lambda b, h, r: (q_off + b * steps + r, h)),
            pl.BlockSpec((LAT_LEN, 128), lambda b, h, r: (seq_off + b, hp + h)),
            pl.BlockSpec((LAT_LEN, 128), lambda b, h, r: (seq_off + b, 2 * hp + h)),
            pl.BlockSpec((CTX_LEN, 128), lambda b, h, r: (b, h)),
            pl.BlockSpec((CTX_LEN, 128), lambda b, h, r: (b, h)),
            pl.BlockSpec((WIN_R, 2, GRID_W, WIN_R * GRID_W), lambda b, h, r: (0, h, 0, 0)),
            pl.BlockSpec(memory_space=pl.ANY),
        ],
        out_specs=pl.BlockSpec((q_rows, 128), lambda b, h, r: (q_off + b * steps + r, h)),
        out_shape=jax.ShapeDtypeStruct((T_ALL, D), F32),
        input_output_aliases={6: 0},
        compiler_params=_cparams(("parallel", "parallel", "arbitrary")),
        name="na_attention",
    )(qkv, qkv, qkv, kcb, vcb, bias, attn_prev)


def _layer_norm(r, g, b):
    mu = jnp.mean(r, axis=-1, keepdims=True)
    c = r - mu
    var = jnp.mean(c * c, axis=-1, keepdims=True)
    return c * lax.rsqrt(var + LN_EPS) * g + b


def _proj_kernel(*refs, n_in):
    a_refs = refs[:n_in]
    w_refs = refs[n_in:2 * n_in]
    (x_ref, mod_ref, g_ref, b_ref, rwh_ref, rwl_ref, rb_ref, tri_ref,
     x1_ref, h2_ref, exp_ref, pos_ref, gate_ref, cnt_ref, run_ref) = refs[2 * n_in:]

    @pl.when(pl.program_id(0) == 0)
    def _():
        run_ref[...] = jnp.zeros_like(run_ref)

    y = None
    for a_ref, w_ref in zip(a_refs, w_refs):
        part = jnp.dot(a_ref[...].astype(BF16), w_ref[...], preferred_element_type=F32)
        y = part if y is None else y + part
    m = mod_ref[...]
    x1 = _layer_norm(DN_ALPHA * x_ref[...] + m[2:3] * y, g_ref[...], b_ref[...])
    x1_ref[...] = x1
    h2 = x1 * (1.0 + m[4:5]) + m[3:4]
    h2_ref[...] = h2.reshape(ROW_TILE, *ROW_AS_TILE)

    hh = h2.astype(BF16)
    hl = (h2 - hh.astype(F32)).astype(BF16)
    logits = (jnp.dot(hh, rwh_ref[...], preferred_element_type=F32)
              + jnp.dot(hh, rwl_ref[...], preferred_element_type=F32)
              + jnp.dot(hl, rwh_ref[...], preferred_element_type=F32)) + rb_ref[...]

    lane = lax.broadcasted_iota(jnp.int32, logits.shape, 1)
    work = logits
    vals, sels = [], []
    for k in range(TOP_K):
        mx = jnp.max(work, axis=-1, keepdims=True)
        idx = jnp.min(jnp.where(work == mx, lane, 128), axis=-1, keepdims=True)
        sel = lane == idx
        work = jnp.where(sel, -jnp.inf, work)
        exp_ref[:, k:k + 1] = idx
        vals.append(mx)
        sels.append(sel)
    exps = [jnp.exp(v - vals[0]) for v in vals]
    den = exps[0] + exps[1] + exps[2] + exps[3]
    for k in range(TOP_K):
        gate_ref[:, k:k + 1] = exps[k] / den

    picked = sels[0] | sels[1] | sels[2] | sels[3]
    onehot = jnp.where(picked, 1.0, 0.0)
    incl = jnp.dot(tri_ref[...], onehot.astype(BF16), preferred_element_type=F32)
    before = incl - onehot + run_ref[...]
    for k in range(TOP_K):
        pos = jnp.sum(jnp.where(sels[k], before, 0.0), axis=-1, keepdims=True)
        pos_ref[:, k:k + 1] = pos.astype(jnp.int32)
    run_ref[...] += jnp.sum(onehot, axis=0, keepdims=True)
    cnt_ref[...] = run_ref[...]


def proj_res_ln(acts, weights, x, mod, ln_g, ln_b, rw_hi, rw_lo, rb):
    n_in = len(acts)
    row = lambda c: pl.BlockSpec((ROW_TILE, c), lambda i: (i, 0))
    full = lambda shp: pl.BlockSpec(shp, lambda i: (0, 0))
    tri = jnp.asarray(np.tril(np.ones((ROW_TILE, ROW_TILE), np.float32)), dtype=BF16)
    in_specs = ([row(a.shape[1]) for a in acts] + [full(w.shape) for w in weights]
                + [row(D), pl.BlockSpec((None, 8, D), lambda i: (_mod_index(i, ROW_TILE), 0, 0)),
                   full((1, D)), full((1, D)), full((D, 128)), full((D, 128)), full((1, 128)),
                   full((ROW_TILE, ROW_TILE))])
    return pl.pallas_call(
        functools.partial(_proj_kernel, n_in=n_in),
        grid=(T_ALL // ROW_TILE,),
        in_specs=in_specs,
        out_specs=[row(D), pl.BlockSpec((ROW_TILE, *ROW_AS_TILE), lambda i: (i, 0, 0)),
                   row(TOP_K), row(TOP_K), row(TOP_K), full((1, 128))],
        out_shape=[jax.ShapeDtypeStruct((T_ALL, D), F32), jax.ShapeDtypeStruct((T_ALL, *ROW_AS_TILE), F32),
                   jax.ShapeDtypeStruct((T_ALL, TOP_K), jnp.int32), jax.ShapeDtypeStruct((T_ALL, TOP_K), jnp.int32),
                   jax.ShapeDtypeStruct((T_ALL, TOP_K), F32), jax.ShapeDtypeStruct((1, 128), F32)],
        scratch_shapes=[pltpu.VMEM((1, 128), F32)],
        compiler_params=_cparams(("arbitrary",)),
        name="proj_res_ln",
    )(*acts, *weights, x, mod, ln_g, ln_b, rw_hi, rw_lo, rb, tri)


def _dispatch_kernel(dest_ref, h_ref, xs_ref, sem):
    def copy(r, d):
        return pltpu.make_async_copy(h_ref.at[r], xs_ref.at[d], sem)

    def issue(r, _):
        for k in range(TOP_K):
            copy(r, dest_ref[r * TOP_K + k]).start()
        return 0

    lax.fori_loop(0, ROW_TILE, issue, 0, unroll=4)

    def drain(r, _):
        for k in range(TOP_K):
            copy(r, dest_ref[r * TOP_K + k]).wait()
        return 0

    lax.fori_loop(0, ROW_TILE, drain, 0, unroll=4)


def moe_dispatch(h2, dest_flat):
    return pl.pallas_call(
        _dispatch_kernel,
        grid=(T_ALL // ROW_TILE,),
        in_specs=[pl.BlockSpec((ROW_TILE * TOP_K,), lambda i: (i,), memory_space=pltpu.SMEM),
                  pl.BlockSpec((ROW_TILE, *ROW_AS_TILE), lambda i: (i, 0, 0))],
        out_specs=pl.BlockSpec(memory_space=pl.ANY),
        out_shape=jax.ShapeDtypeStruct((N_ASSIGN, *ROW_AS_TILE), F32),
        scratch_shapes=[pltpu.SemaphoreType.DMA],
        compiler_params=_cparams(("arbitrary",)),
        name="moe_dispatch",
    )(dest_flat, h2)


W_CAST_ROWS = 128


def _moe_ffn_kernel(blk_ref, exp_ref, lo_ref, hi_ref, first_ref, newexp_ref,
                    x_ref, w1_ref, b1_ref, w2_ref, b2_ref, o_ref, w1b_ref, w2b_ref):
    i = pl.program_id(0)
    lo = lo_ref[i]
    hi = hi_ref[i]

    @pl.when(newexp_ref[i] == 1)
    def _():
        def cast(c, _):
            rows = pl.ds(pl.multiple_of(c * W_CAST_ROWS, W_CAST_ROWS), W_CAST_ROWS)
            w1b_ref[rows, :] = w1_ref[rows, :].astype(BF16)
            w2b_ref[rows, :] = w2_ref[rows, :].astype(BF16)
            return 0

        lax.fori_loop(0, D // W_CAST_ROWS, cast, 0)

    @pl.when(first_ref[i] == 1)
    def _():
        o_ref[...] = jnp.zeros_like(o_ref)

    @pl.when(hi > lo)
    def _():
        x = x_ref[...].reshape(MOE_TILE, D).astype(BF16)
        h = jnp.dot(x, w1b_ref[...], preferred_element_type=F32) + b1_ref[...]
        g = jnp.minimum(h[:, :D_FF], SWIGLU_LIMIT)
        u = jnp.clip(h[:, D_FF:], -SWIGLU_LIMIT, SWIGLU_LIMIT)
        a = g * jax.nn.sigmoid(SWIGLU_ALPHA * g) * (u + 1.0)
        y = jnp.dot(a.astype(BF16), w2b_ref[...], preferred_element_type=F32) + b2_ref[...]
        row = lax.broadcasted_iota(jnp.int32, (MOE_TILE, 1), 0)
        o_ref[...] += jnp.where((row >= lo) & (row < hi), y, 0.0).reshape(MOE_TILE, *ROW_AS_TILE)


def moe_ffn(items, xs, w1, b1, w2, b2, layer):
    assert D == D_FF
    blk, exp, lo, hi, first, newexp = items
    grid_spec = pltpu.PrefetchScalarGridSpec(
        num_scalar_prefetch=6,
        grid=(N_MOE_ITEMS,),
        in_specs=[
            pl.BlockSpec((MOE_TILE, *ROW_AS_TILE), lambda i, blk, exp, *_: (blk[i], 0, 0)),
            pl.BlockSpec((None, None, D, 2 * D_FF), lambda i, blk, exp, *_: (layer, exp[i], 0, 0)),
            pl.BlockSpec((None, None, 1, 2 * D_FF), lambda i, blk, exp, *_: (layer, exp[i], 0, 0)),
            pl.BlockSpec((None, None, D_FF, D), lambda i, blk, exp, *_: (layer, exp[i], 0, 0)),
            pl.BlockSpec((None, None, 1, D), lambda i, blk, exp, *_: (layer, exp[i], 0, 0)),
        ],
        out_specs=pl.BlockSpec((MOE_TILE, *ROW_AS_TILE), lambda i, blk, exp, *_: (blk[i], 0, 0)),
        scratch_shapes=[pltpu.VMEM((D, 2 * D_FF), BF16), pltpu.VMEM((D_FF, D), BF16)],
    )
    return pl.pallas_call(
        _moe_ffn_kernel,
        grid_spec=grid_spec,
        out_shape=jax.ShapeDtypeStruct((N_ASSIGN, *ROW_AS_TILE), F32),
        compiler_params=_cparams(("arbitrary",)),
        name="moe_ffn",
    )(blk, exp, lo, hi, first, newexp, xs, w1, b1, w2, b2)


N_COMB_STEPS = T_ALL // COMB_TILE


def _combine_kernel(dest_ref, dest_next_ref, ys_ref, gates_ref, x_ref, mod_ref, g_ref, b_ref, o_ref, buf_ref, sems):
    i = pl.program_id(0)
    slot = i % 2

    def copy(idx_ref, s, r, k):
        d = idx_ref[r * TOP_K + k]
        return pltpu.make_async_copy(ys_ref.at[d], buf_ref.at[s, k, r], sems.at[s])

    def gather(idx_ref, s):
        def issue(r, _):
            for k in range(TOP_K):
                copy(idx_ref, s, r, k).start()
            return 0

        lax.fori_loop(0, COMB_TILE, issue, 0, unroll=4)

    @pl.when(i == 0)
    def _():
        gather(dest_ref, 0)

    @pl.when(i + 1 < N_COMB_STEPS)
    def _():
        gather(dest_next_ref, 1 - slot)

    def drain(r, _):
        for k in range(TOP_K):
            copy(dest_ref, slot, r, k).wait()
        return 0

    lax.fori_loop(0, COMB_TILE, drain, 0, unroll=4)

    gates = gates_ref[...]
    y = gates[:, 0:1] * buf_ref[slot, 0].reshape(COMB_TILE, D)
    for k in range(1, TOP_K):
        y = y + gates[:, k:k + 1] * buf_ref[slot, k].reshape(COMB_TILE, D)
    m = mod_ref[...]
    o_ref[...] = _layer_norm(DN_ALPHA * x_ref[...] + m[5:6] * y, g_ref[...], b_ref[...])


def moe_combine(ys, dest_flat, gates, x1, mod, ln_g, ln_b):
    full = lambda shp: pl.BlockSpec(shp, lambda i: (0, 0))
    idx = lambda fn: pl.BlockSpec((COMB_TILE * TOP_K,), fn, memory_space=pltpu.SMEM)
    return pl.pallas_call(
        _combine_kernel,
        grid=(N_COMB_STEPS,),
        in_specs=[idx(lambda i: (i,)), idx(lambda i: (jnp.minimum(i + 1, N_COMB_STEPS - 1),)),
                  pl.BlockSpec(memory_space=pl.ANY),
                  pl.BlockSpec((COMB_TILE, TOP_K), lambda i: (i, 0)),
                  pl.BlockSpec((COMB_TILE, D), lambda i: (i, 0)),
                  pl.BlockSpec((None, 8, D), lambda i: (_mod_index(i, COMB_TILE), 0, 0)),
                  full((1, D)), full((1, D))],
        out_specs=pl.BlockSpec((COMB_TILE, D), lambda i: (i, 0)),
        out_shape=jax.ShapeDtypeStruct((T_ALL, D), F32),
        scratch_shapes=[pltpu.VMEM((2, TOP_K, COMB_TILE, *ROW_AS_TILE), F32), pltpu.SemaphoreType.DMA((2,))],
        compiler_params=_cparams(("arbitrary",)),
        name="moe_combine",
    )(dest_flat, dest_flat, ys, gates, x1, mod, ln_g, ln_b)


def _routing_tables(experts, pos, counts):
    counts = counts[0, :N_EXPERTS].astype(jnp.int32)
    starts = jnp.cumsum(counts) - counts
    eids = jnp.arange(N_EXPERTS, dtype=jnp.int32)
    start_of_pick = jnp.sum(jnp.where(experts[..., None] == eids, starts, 0), axis=-1)
    dest = start_of_pick + pos
    bnd = jnp.sort(jnp.concatenate([jnp.arange(N_MOE_BLOCKS + 1, dtype=jnp.int32) * MOE_TILE, starts[1:]]))
    a, b = bnd[:-1], bnd[1:]
    blk = jnp.minimum(a // MOE_TILE, N_MOE_BLOCKS - 1)
    lo = a - blk * MOE_TILE
    hi = b - blk * MOE_TILE
    exp = jnp.clip(jnp.sum((starts[None, :] <= a[:, None]).astype(jnp.int32), axis=1) - 1, 0, N_EXPERTS - 1)
    one = jnp.ones((1,), jnp.int32)
    first = jnp.concatenate([one, (blk[1:] != blk[:-1]).astype(jnp.int32)])
    newexp = jnp.concatenate([one, (exp[1:] != exp[:-1]).astype(jnp.int32)])
    items = tuple(v.astype(jnp.int32) for v in (blk, exp, lo, hi, first, newexp))
    return dest.astype(jnp.int32).reshape(-1), items


def moe_layer(h2, experts, pos, gates, counts, x1, mod, ln_g, ln_b, w1, b1, w2, b2, layer):
    dest_flat, items = _routing_tables(experts, pos, counts)
    xs = moe_dispatch(h2, dest_flat)
    ys = moe_ffn(items, xs, w1, b1, w2, b2, layer)
    return moe_combine(ys, dest_flat, gates, x1, mod, ln_g, ln_b)


def _router_operands(router_w, router_b):
    w = jnp.pad(router_w, ((0, 0), (0, 128 - N_EXPERTS)))
    hi = w.astype(BF16)
    lo = (w - hi.astype(F32)).astype(BF16)
    b = jnp.concatenate([router_b, jnp.full((128 - N_EXPERTS,), -jnp.inf, F32)])[None]
    return hi, lo, b


def kernel(x_prompt, x_sample, state_s5, cache_na_k, cache_na_v, c, c_ctx, w_mod, b_mod, ln1_g, ln1_b, ln2_g, ln2_b, ev_w_in, ev_w_out, hy_conv_w, hy_conv_b, hy_f_w1, hy_f_b1, hy_f_w2, hy_f_b2, hy_f_w3, hy_f_freq, hy_bias, s5_lam_re, s5_lam_im, s5_log_dt, s5_b_re, s5_b_im, s5_c_re, s5_c_im, s5_d, s5_glu_w, s5_glu_b, od_w_in, od_w_out, na_rpb, router_w, router_b, moe_w1, moe_b1, moe_w2, moe_b2):
    x = jnp.concatenate([x_prompt.reshape(T_CTX, D), x_sample.reshape(T_LAT, D)], axis=0)
    cvec = jnp.pad(jnp.concatenate([c_ctx[None], c], axis=0), ((0, 5), (0, 0)))

    new_state = None
    new_k = new_v = None
    for l in range(DEPTH):
        i = l // 2
        mod = matmul(cvec, w_mod, b_mod[l][None], b_layer=l, tm=8, tn=512, tk=D, precise=True, silu_a=True)
        mod = jnp.pad(mod.reshape(8, 6, D)[:3], ((0, 0), (0, 2), (0, 0)))

        if l % 2 == 0:
            u = modlinear(x, mod, ev_w_in[i].astype(BF16))
            y_hy = jnp.zeros((T_ALL, HY_CH), F32)
            for n_seq, seq_len, row_off, tile in ((N_CTX_SEQ, CTX_LEN, 0, CTX_LEN), (N_LAT_SEQ, LAT_LEN, T_CTX, 1024)):
                taps = hyena_filter_taps(seq_len, hy_f_w1[i], hy_f_b1[i], hy_f_w2[i], hy_f_b2[i], hy_f_w3[i],
                                         hy_f_freq[i])
                y_hy = hyena_group(u, hy_conv_w[i], hy_conv_b[i][None], taps, hy_bias[i][None], y_hy,
                                   n_seq=n_seq, seq_len=seq_len, row_off=row_off, tile=tile)

            h0_lat = state_s5[:, i]
            yf_parts, yb_parts, finals = [], [], []
            for r in range(2):
                tables = _s5_tables(s5_lam_re[i, r], s5_lam_im[i, r], s5_log_dt[i, r], s5_b_re[i, r], s5_b_im[i, r],
                                    s5_c_re[i, r], s5_c_im[i, r], reverse=(r == 1))
                h0_l = jnp.concatenate([h0_lat[:, r, :, :, 0].reshape(N_LAT_SEQ, 1, S5_NS),
                                        h0_lat[:, r, :, :, 1].reshape(N_LAT_SEQ, 1, S5_NS)], axis=-1)
                h0 = jnp.concatenate([jnp.zeros((N_CTX_SEQ, 1, 2 * S5_NS), F32), h0_l], axis=0)
                y_dir, fin_all = s5_direction(u, tables, h0, reverse=(r == 1))
                (yf_parts if r == 0 else yb_parts).append(y_dir)
                fin = fin_all[:N_CTX_SEQ].reshape(N_CTX_SEQ, 2, S5_GROUPS, S5_STATE)
                finals.append(jnp.stack([fin[:, 0], fin[:, 1]], axis=-1))
            new_state = jnp.stack(finals, axis=1)[:, None]
            y_s5 = s5_post(u, yf_parts[0], yb_parts[0], s5_d[i].reshape(1, S5_CH), s5_glu_w[i].astype(BF16),
                           s5_glu_b[i][None])
            acts = [y_hy, y_s5]
            w_out = ev_w_out[i].astype(BF16)
            weights = [w_out[:HY_CH], w_out[HY_CH:]]
        else:
            qkv = modlinear(x, mod, od_w_in[i].astype(BF16))
            new_k = qkv[:T_CTX, D:2 * D].reshape(N_CTX_SEQ, 1, CTX_LEN, N_HEADS, HEAD_DIM)
            new_v = qkv[:T_CTX, 2 * D:].reshape(N_CTX_SEQ, 1, CTX_LEN, N_HEADS, HEAD_DIM)
            attn = ctx_attention(qkv, jnp.zeros((T_ALL, D), F32))
            kcb = cache_na_k[:, i].reshape(N_LAT_SEQ * CTX_LEN, D).astype(BF16)
            vcb = cache_na_v[:, i].reshape(N_LAT_SEQ * CTX_LEN, D).astype(BF16)
            attn = na_attention(qkv, kcb, vcb, _na_bias_table(na_rpb[i]), attn)
            acts = [attn]
            weights = [od_w_out[i].astype(BF16)]

        rw_hi, rw_lo, rb = _router_operands(router_w[l], router_b[l])
        x1, h2, experts, pos, gates, counts = proj_res_ln(acts, weights, x, mod, ln1_g[l][None], ln1_b[l][None],
                                                          rw_hi, rw_lo, rb)
        x = moe_layer(h2, experts, pos, gates, counts, x1, mod, ln2_g[l][None], ln2_b[l][None],
                      moe_w1, moe_b1[:, :, None, :], moe_w2, moe_b2[:, :, None, :], l)

    y_prompt = x[:T_CTX].reshape(N_CTX_SEQ, CTX_LEN, D)
    y_sample = x[T_CTX:].reshape(N_LAT_SEQ, LAT_LEN, D)
    return (y_prompt, y_sample, new_state, new_k, new_v)
```

```python
import functools
import math

import jax
import jax.numpy as jnp
import numpy as np
from jax import lax
from jax.experimental import pallas as pl
from jax.experimental.pallas import tpu as pltpu

F32 = jnp.float32
BF16 = jnp.bfloat16
HIGHEST = lax.Precision.HIGHEST

D = 1024
N_CTX_SEQ, CTX_LEN = 32, 256
N_LAT_SEQ, LAT_LEN = 2, 4096
T_CTX = N_CTX_SEQ * CTX_LEN
T_LAT = N_LAT_SEQ * LAT_LEN
T_ALL = T_CTX + T_LAT
DEPTH = 2
HY_CH = 512
S5_CH = 512
S5_GROUPS, S5_GROUP, S5_STATE = 32, 16, 64
S5_NS = S5_GROUPS * S5_STATE
HY_EMB, HY_BANDS, HY_ORDER = 33, 16, 64
N_HEADS, HEAD_DIM = 16, 64
GRID_W, WIN_R, WIN_C = 64, 8, 16
N_EXPERTS, TOP_K, D_FF = 32, 4, 1024
SWIGLU_LIMIT, SWIGLU_ALPHA = 7.0, 1.702
LN_EPS = 1e-5
DN_ALPHA = (2 * DEPTH) ** 0.25
NEG_BIG = -1e30

ROW_TILE = 256
MOE_TILE = 256
N_ASSIGN = T_ALL * TOP_K
N_MOE_BLOCKS = N_ASSIGN // MOE_TILE
N_MOE_ITEMS = N_MOE_BLOCKS + N_EXPERTS - 1
COMB_TILE = 128
ROW_AS_TILE = (8, D // 8)
S5_CHUNK = 256
S5_SEG = S5_CHUNK // 8
S5_COLS = 512
VMEM_LIMIT = 56 * 1024 * 1024


def _cparams(sem, vmem=None):
    return pltpu.CompilerParams(dimension_semantics=sem, vmem_limit_bytes=vmem or VMEM_LIMIT)


def _mod_index(i, rows_per_tile):
    n_ctx = T_CTX // rows_per_tile
    per_lat = LAT_LEN // rows_per_tile
    return jnp.where(i < n_ctx, 0, 1 + (i - n_ctx) // per_lat)


def _mm_kernel(a_ref, b_ref, bias_ref, o_ref, acc_ref, *, nk, precise, silu_a):
    k = pl.program_id(2)

    @pl.when(k == 0)
    def _():
        acc_ref[...] = jnp.zeros_like(acc_ref)

    a = a_ref[...]
    if silu_a:
        a = a * jax.nn.sigmoid(a)
    if precise:
        acc_ref[...] += jnp.dot(a, b_ref[...], preferred_element_type=F32, precision=HIGHEST)
    else:
        acc_ref[...] += jnp.dot(a.astype(BF16), b_ref[...].astype(BF16), preferred_element_type=F32)

    @pl.when(k == nk - 1)
    def _():
        o_ref[...] = acc_ref[...] + bias_ref[...]


def matmul(a, b, bias=None, *, tm, tn, tk, b_layer=None, precise=False, silu_a=False):
    kdim, n = b.shape[-2:]
    m = a.shape[0]
    if bias is None:
        bias = jnp.zeros((1, n), F32)
    nk = kdim // tk
    if b.ndim == 3:
        b_spec = pl.BlockSpec((None, tk, tn), lambda i, j, k: (b_layer, k, j))
    else:
        b_spec = pl.BlockSpec((tk, tn), lambda i, j, k: (k, j))
    return pl.pallas_call(
        functools.partial(_mm_kernel, nk=nk, precise=precise, silu_a=silu_a),
        grid=(m // tm, n // tn, nk),
        in_specs=[pl.BlockSpec((tm, tk), lambda i, j, k: (i, k)), b_spec,
                  pl.BlockSpec((1, tn), lambda i, j, k: (0, j))],
        out_specs=pl.BlockSpec((tm, tn), lambda i, j, k: (i, j)),
        out_shape=jax.ShapeDtypeStruct((m, n), F32),
        scratch_shapes=[pltpu.VMEM((tm, tn), F32)],
        compiler_params=_cparams(("parallel", "parallel", "arbitrary")),
        name="matmul",
    )(a, b, bias)


def _modlinear_kernel(x_ref, mod_ref, w_ref, o_ref):
    m = mod_ref[...]
    h = x_ref[...] * (1.0 + m[1:2]) + m[0:1]
    o_ref[...] = jnp.dot(h.astype(BF16), w_ref[...], preferred_element_type=F32)


def modlinear(x, mod, w_bf16):
    n = w_bf16.shape[1]
    return pl.pallas_call(
        _modlinear_kernel,
        grid=(T_ALL // ROW_TILE,),
        in_specs=[
            pl.BlockSpec((ROW_TILE, D), lambda i: (i, 0)),
            pl.BlockSpec((None, 8, D), lambda i: (_mod_index(i, ROW_TILE), 0, 0)),
            pl.BlockSpec((D, n), lambda i: (0, 0)),
        ],
        out_specs=pl.BlockSpec((ROW_TILE, n), lambda i: (i, 0)),
        out_shape=jax.ShapeDtypeStruct((T_ALL, n), F32),
        compiler_params=_cparams(("parallel",)),
        name="modlinear",
    )(x, mod, w_bf16)


def _filter_kernel(z_ref, w1_ref, b1_ref, w2_ref, b2_ref, w3_ref, fq_ref, dl_ref, hsum_ref, hdiff_ref, *, tile):
    z = z_ref[...]
    fq = fq_ref[...]
    h = jnp.sin(fq * (jnp.dot(z, w1_ref[...], preferred_element_type=F32, precision=HIGHEST) + b1_ref[...]))
    h = jnp.sin(fq * (jnp.dot(h, w2_ref[...], preferred_element_type=F32, precision=HIGHEST) + b2_ref[...]))
    h = jnp.dot(h, w3_ref[...], preferred_element_type=F32, precision=HIGHEST)
    decay = jnp.exp(-z[:, 0:1] * dl_ref[...])
    hf = h[:, :HY_CH] * decay
    hb = h[:, HY_CH:] * decay
    row = lax.broadcasted_iota(jnp.int32, (tile, 1), 0) + pl.program_id(0) * tile
    hsum_ref[...] = hf + hb
    hdiff_ref[...] = jnp.where(row == 0, hf + hb, hf - hb)


def hyena_filter_taps(seq_len, w1, b1, w2, b2, w3, freq):
    t = jnp.linspace(0.0, 1.0, seq_len, dtype=F32)[:, None]
    w = 2.0 * math.pi * jnp.arange(seq_len, dtype=F32)[:, None] / seq_len
    f = jnp.linspace(1e-4, HY_BANDS - 1, HY_BANDS, dtype=F32)[None, :]
    z = jnp.concatenate([t, jnp.cos(f * w), -jnp.sin(f * w)], -1)
    z = jnp.pad(z, ((0, 0), (0, 128 - HY_EMB)))
    pad_o = 128 - HY_ORDER
    w1p = jnp.pad(w1, ((0, 128 - HY_EMB), (0, pad_o)))
    w2p = jnp.pad(w2, ((0, pad_o), (0, pad_o)))
    w3p = jnp.pad(w3, ((0, pad_o), (0, 0)))
    b1p = jnp.pad(b1, (0, pad_o))[None]
    b2p = jnp.pad(b2, (0, pad_o))[None]
    fqp = jnp.pad(freq, (0, pad_o))[None]
    max_decay = math.log(1e-2) / 0.3
    min_decay = math.log(1e-2) / 1.5
    absdelta = jnp.abs(jnp.linspace(min_decay, max_decay, HY_CH, dtype=F32))[None]
    tile = 256
    full = lambda shp: pl.BlockSpec(shp, lambda i: (0, 0))
    return pl.pallas_call(
        functools.partial(_filter_kernel, tile=tile),
        grid=(seq_len // tile,),
        in_specs=[pl.BlockSpec((tile, 128), lambda i: (i, 0)), full((128, 128)), full((1, 128)), full((128, 128)),
                  full((1, 128)), full((128, 2 * HY_CH)), full((1, 128)), full((1, HY_CH))],
        out_specs=[pl.BlockSpec((tile, HY_CH), lambda i: (i, 0))] * 2,
        out_shape=[jax.ShapeDtypeStruct((seq_len, HY_CH), F32)] * 2,
        compiler_params=_cparams(("parallel",)),
        name="hyena_filter",
    )(z, w1p, b1p, w2p, b2p, w3p, fqp, absdelta)


def _dft_tables(seq_len):
    s = int(round(math.sqrt(seq_len)))
    j = jnp.arange(s, dtype=jnp.int32)[:, None]
    c = jnp.arange(seq_len, dtype=jnp.int32)[None, :]
    unit = math.pi / (2 * seq_len)

    def cos_sin(idx):
        ang = (idx % (4 * seq_len)).astype(F32) * unit
        return jnp.cos(ang), jnp.sin(ang)

    def expand(ca, sa, cb, sb):
        cos = ca[:, None, :] * cb[None, :, :] - sa[:, None, :] * sb[None, :, :]
        nsin = -(sa[:, None, :] * cb[None, :, :] + ca[:, None, :] * sb[None, :, :])
        return cos.reshape(seq_len, seq_len).astype(BF16), nsin.reshape(seq_len, seq_len).astype(BF16)

    cm, nsm = expand(*cos_sin((2 * s * j) * c), *cos_sin((2 * j + 1) * c))
    cmt, nsmt = expand(*cos_sin((2 * c + 1) * (s * j)), *cos_sin((2 * c + 1) * j))
    return cm, nsm, cmt, nsmt


def _hyena_pre_kernel(u0_ref, u1_ref, u2_ref, w0_ref, w1_ref, w2_ref, b0_ref, b1_ref, b2_ref,
                      vx_ref, vxb_ref, x0_ref, *, seq_len):
    row = lax.broadcasted_iota(jnp.int32, (seq_len, 1), 0)

    def short_conv(u_ref, w_ref, b_ref):
        a = u_ref[...]
        w = w_ref[...]
        prev = jnp.where(row == 0, 0.0, pltpu.roll(a, 1, 0))
        nxt = jnp.where(row == seq_len - 1, 0.0, pltpu.roll(a, seq_len - 1, 0))
        return prev * w[0:1] + a * w[1:2] + nxt * w[2:3] + b_ref[...]

    x0 = short_conv(u0_ref, w0_ref, b0_ref)
    x1 = short_conv(u1_ref, w1_ref, b1_ref)
    v = short_conv(u2_ref, w2_ref, b2_ref)
    vx = v * x1
    vx_ref[...] = vx
    vxb_ref[...] = vx.astype(BF16)
    x0_ref[...] = x0


def hyena_pre(u, conv_w, conv_b, *, n_seq, seq_len, row_off):
    cb = min(HY_CH, 128 * LAT_LEN // seq_len)
    ncb = HY_CH // cb
    rb0 = row_off // seq_len
    uspec = lambda part: pl.BlockSpec((seq_len, cb), lambda b, j: (rb0 + b, part * ncb + j))
    wspec = lambda part: pl.BlockSpec((3, cb), lambda b, j: (0, part * ncb + j))
    bspec = lambda part: pl.BlockSpec((1, cb), lambda b, j: (0, part * ncb + j))
    ospec = pl.BlockSpec((seq_len, cb), lambda b, j: (b, j))
    rows = n_seq * seq_len
    return pl.pallas_call(
        functools.partial(_hyena_pre_kernel, seq_len=seq_len),
        grid=(n_seq, ncb),
        in_specs=[uspec(0), uspec(1), uspec(2), wspec(0), wspec(1), wspec(2), bspec(0), bspec(1), bspec(2)],
        out_specs=[ospec, ospec, ospec],
        out_shape=[jax.ShapeDtypeStruct((rows, HY_CH), F32), jax.ShapeDtypeStruct((rows, HY_CH), BF16),
                   jax.ShapeDtypeStruct((rows, HY_CH), F32)],
        compiler_params=_cparams(("parallel", "parallel")),
        name="hyena_pre",
    )(u, u, u, conv_w, conv_w, conv_w, conv_b, conv_b, conv_b)


def _dft_fwd_kernel(cm_ref, nsm_ref, v_ref, hr_ref, hi_ref, zr_ref, zi_ref, accr_ref, acci_ref, *, nk):
    k = pl.program_id(3)

    @pl.when(k == 0)
    def _():
        accr_ref[...] = jnp.zeros_like(accr_ref)
        acci_ref[...] = jnp.zeros_like(acci_ref)

    v = v_ref[...]
    accr_ref[...] += jnp.dot(cm_ref[...], v, preferred_element_type=F32)
    acci_ref[...] += jnp.dot(nsm_ref[...], v, preferred_element_type=F32)

    @pl.when(k == nk - 1)
    def _():
        xr, xi = accr_ref[...], acci_ref[...]
        hr, hi = hr_ref[...], hi_ref[...]
        zr_ref[...] = (xr * hr - xi * hi).astype(BF16)
        zi_ref[...] = (xr * hi + xi * hr).astype(BF16)


def dft_fwd(cm, nsm, vxb, hr, hi, *, n_seq, seq_len, tile):
    nk = seq_len // tile
    tn = HY_CH
    v3 = vxb.reshape(n_seq, seq_len, HY_CH)
    zspec = pl.BlockSpec((None, tile, tn), lambda s, i, j, k: (s, i, j))
    return pl.pallas_call(
        functools.partial(_dft_fwd_kernel, nk=nk),
        grid=(n_seq, seq_len // tile, HY_CH // tn, nk),
        in_specs=[
            pl.BlockSpec((tile, tile), lambda s, i, j, k: (i, k)),
            pl.BlockSpec((tile, tile), lambda s, i, j, k: (i, k)),
            pl.BlockSpec((None, tile, tn), lambda s, i, j, k: (s, k, j)),
            pl.BlockSpec((tile, tn), lambda s, i, j, k: (i, j)),
            pl.BlockSpec((tile, tn), lambda s, i, j, k: (i, j)),
        ],
        out_specs=[zspec, zspec],
        out_shape=[jax.ShapeDtypeStruct((n_seq, seq_len, HY_CH), BF16)] * 2,
        scratch_shapes=[pltpu.VMEM((tile, tn), F32), pltpu.VMEM((tile, tn), F32)],
        compiler_params=_cparams(("parallel", "parallel", "parallel", "arbitrary")),
        name="dft_fwd",
    )(cm, nsm, v3, hr, hi)


def _dft_inv_kernel(cmt_ref, nsmt_ref, zr_ref, zi_ref, vx_ref, x0_ref, bias_ref, prev_ref, o_ref, acc_ref, *,
                    nk, inv_len):
    del prev_ref
    k = pl.program_id(3)

    @pl.when(k == 0)
    def _():
        acc_ref[...] = jnp.zeros_like(acc_ref)

    acc_ref[...] += (jnp.dot(cmt_ref[...], zr_ref[...], preferred_element_type=F32)
                     + jnp.dot(nsmt_ref[...], zi_ref[...], preferred_element_type=F32))

    @pl.when(k == nk - 1)
    def _():
        conv = acc_ref[...] * inv_len
        o_ref[...] = (conv + vx_ref[...] * bias_ref[...]) * x0_ref[...]


def dft_inv(cmt, nsmt, zr, zi, vx, x0, bias, y_prev, *, n_seq, seq_len, row_off, tile):
    nk = seq_len // tile
    tn = HY_CH
    per_seq = seq_len // tile
    rb0 = row_off // tile
    zspec = pl.BlockSpec((None, tile, tn), lambda s, i, j, k: (s, k, j))
    espec = pl.BlockSpec((None, tile, tn), lambda s, i, j, k: (s, i, j))
    return pl.pallas_call(
        functools.partial(_dft_inv_kernel, nk=nk, inv_len=1.0 / seq_len),
        grid=(n_seq, per_seq, HY_CH // tn, nk),
        in_specs=[
            pl.BlockSpec((tile, tile), lambda s, i, j, k: (i, k)),
            pl.BlockSpec((tile, tile), lambda s, i, j, k: (i, k)),
            zspec, zspec, espec, espec,
            pl.BlockSpec((1, tn), lambda s, i, j, k: (0, j)),
            pl.BlockSpec(memory_space=pl.ANY),
        ],
        out_specs=pl.BlockSpec((tile, tn), lambda s, i, j, k: (rb0 + s * per_seq + i, j)),
        out_shape=jax.ShapeDtypeStruct((T_ALL, HY_CH), F32),
        scratch_shapes=[pltpu.VMEM((tile, tn), F32)],
        input_output_aliases={7: 0},
        compiler_params=_cparams(("parallel", "parallel", "parallel", "arbitrary")),
        name="dft_inv",
    )(cmt, nsmt, zr, zi, vx.reshape(n_seq, seq_len, HY_CH), x0.reshape(n_seq, seq_len, HY_CH), bias, y_prev)


def hyena_group(u, conv_w, conv_b, taps, bias, y_prev, *, n_seq, seq_len, row_off, tile):
    hsum, hdiff = taps
    cm, nsm, cmt, nsmt = _dft_tables(seq_len)
    hr = matmul(cm, hsum, tm=tile, tn=HY_CH, tk=tile)
    hi = matmul(nsm, hdiff, tm=tile, tn=HY_CH, tk=tile)
    vx, vxb, x0 = hyena_pre(u, conv_w, conv_b, n_seq=n_seq, seq_len=seq_len, row_off=row_off)
    zr, zi = dft_fwd(cm, nsm, vxb, hr, hi, n_seq=n_seq, seq_len=seq_len, tile=tile)
    return dft_inv(cmt, nsmt, zr, zi, vx, x0, bias, y_prev, n_seq=n_seq, seq_len=seq_len, row_off=row_off, tile=tile)


def _s5_tables(lam_re, lam_im, log_dt, b_re, b_im, c_re, c_im, reverse):
    dt = jnp.exp(log_dt)[:, None]
    a = lam_re * dt
    b = lam_im * dt
    mag = jnp.exp(a)
    lbr, lbi = mag * jnp.cos(b), mag * jnp.sin(b)
    den = lam_re * lam_re + lam_im * lam_im
    qr = ((lbr - 1.0) * lam_re + lbi * lam_im) / den
    qi = (lbi * lam_re - (lbr - 1.0) * lam_im) / den
    bbr = qr[..., None] * b_re - qi[..., None] * b_im
    bbi = qr[..., None] * b_im + qi[..., None] * b_re
    eye = jnp.eye(S5_GROUPS, dtype=F32)

    def in_block(m):
        return jnp.einsum("gnc,gh->gchn", m, eye).reshape(S5_CH, S5_NS)

    def out_block(m):
        return jnp.einsum("gcn,gh->gnhc", m, eye).reshape(S5_NS, S5_CH)

    b_blk = jnp.concatenate([in_block(bbr), in_block(bbi)], axis=1).astype(BF16)
    c_blk = jnp.concatenate([out_block(c_re), out_block(-c_im)], axis=0).astype(BF16)
    flat = lambda m: m.reshape(1, S5_NS)
    lam = jnp.concatenate([flat(lbr), flat(lbi)], axis=1)
    steps = jnp.arange(1, S5_SEG + 1, dtype=F32)
    if reverse:
        steps = steps[::-1]
    pa = a.reshape(1, S5_NS) * steps[:, None]
    pb = b.reshape(1, S5_NS) * steps[:, None]
    pw = jnp.concatenate([jnp.exp(pa) * jnp.cos(pb), jnp.exp(pa) * jnp.sin(pb)], axis=1)
    sa, sb = flat(a) * S5_SEG, flat(b) * S5_SEG
    lam_seg = jnp.concatenate([jnp.exp(sa) * jnp.cos(sb), jnp.exp(sa) * jnp.sin(sb)], axis=1)
    return b_blk, c_blk, lam, lam_seg, pw


N_S5_CTX_STEPS = T_CTX // S5_CHUNK
N_S5_LAT_CHUNKS = LAT_LEN // S5_CHUNK
N_S5_STEPS = T_ALL // S5_CHUNK


def _s5_step_info(i, reverse):
    k = jnp.maximum(i - N_S5_CTX_STEPS, 0)
    b, j = k // N_S5_LAT_CHUNKS, k % N_S5_LAT_CHUNKS
    chunk = (N_S5_LAT_CHUNKS - 1 - j) if reverse else j
    is_ctx = i < N_S5_CTX_STEPS
    row_blk = jnp.where(is_ctx, i, N_S5_CTX_STEPS + b * N_S5_LAT_CHUNKS + chunk)
    seq = jnp.where(is_ctx, i, N_CTX_SEQ + b)
    return row_blk, seq, is_ctx | (j == 0), is_ctx | (j == N_S5_LAT_CHUNKS - 1)


def _s5_kernel(u_ref, perm_ref, permt_ref, bblk_ref, cblk_ref, lam_ref, lseg_ref, pw_ref, h0_ref, y_ref, fin_ref,
               bu_ref, hend_ref, cin_ref, carry_ref, *, reverse):
    _, _, is_first, is_last = _s5_step_info(pl.program_id(0), reverse)
    ns = S5_NS

    @pl.when(is_first)
    def _():
        carry_ref[...] = h0_ref[...]

    u_seg = jnp.dot(perm_ref[...], u_ref[...].astype(BF16), preferred_element_type=F32).astype(BF16)
    n_lane_blk = S5_CH // 128
    st_blk = ns // n_lane_blk
    for q in range(n_lane_blk):
        u_q = u_seg[:, q * 128:(q + 1) * 128]
        for part in range(2):
            cols = pl.ds(part * ns + q * st_blk, st_blk)
            bu_ref[:, cols] = jnp.dot(u_q, bblk_ref[q * 128:(q + 1) * 128, cols], preferred_element_type=F32)

    for cb in range(ns // S5_COLS):
        re_cols = pl.ds(cb * S5_COLS, S5_COLS)
        im_cols = pl.ds(ns + cb * S5_COLS, S5_COLS)
        lr = jnp.broadcast_to(lam_ref[:, re_cols], (8, S5_COLS))
        li = jnp.broadcast_to(lam_ref[:, im_cols], (8, S5_COLS))

        def step(kk, carry, re_cols=re_cols, im_cols=im_cols, lr=lr, li=li):
            hr, hi = carry
            k = (S5_SEG - 1 - kk) if reverse else kk
            rows = pl.ds(pl.multiple_of(k * 8, 8), 8)
            nr = lr * hr - li * hi + bu_ref[rows, re_cols]
            ni = lr * hi + li * hr + bu_ref[rows, im_cols]
            bu_ref[rows, re_cols] = nr
            bu_ref[rows, im_cols] = ni
            return nr, ni

        zero = jnp.zeros((8, S5_COLS), F32)
        hr, hi = lax.fori_loop(0, S5_SEG, step, (zero, zero), unroll=4)
        hend_ref[:, re_cols] = hr
        hend_ref[:, im_cols] = hi

    cr = carry_ref[:, :ns]
    ci = carry_ref[:, ns:]
    lsr = lseg_ref[:, :ns]
    lsi = lseg_ref[:, ns:]
    for p in range(8):
        s = 7 - p if reverse else p
        cin_ref[s:s + 1, :ns] = cr
        cin_ref[s:s + 1, ns:] = ci
        er = hend_ref[s:s + 1, :ns]
        ei = hend_ref[s:s + 1, ns:]
        cr, ci = er + lsr * cr - lsi * ci, ei + lsr * ci + lsi * cr
    carry_ref[:, :ns] = cr
    carry_ref[:, ns:] = ci

    for cb in range(ns // S5_COLS):
        re_cols = pl.ds(cb * S5_COLS, S5_COLS)
        im_cols = pl.ds(ns + cb * S5_COLS, S5_COLS)
        cinr = cin_ref[:, re_cols]
        cini = cin_ref[:, im_cols]

        def fix(k, _, re_cols=re_cols, im_cols=im_cols, cinr=cinr, cini=cini):
            rows = pl.ds(pl.multiple_of(k * 8, 8), 8)
            pr = pw_ref[pl.ds(k, 1), re_cols]
            pi = pw_ref[pl.ds(k, 1), im_cols]
            bu_ref[rows, re_cols] += pr * cinr - pi * cini
            bu_ref[rows, im_cols] += pr * cini + pi * cinr
            return 0

        lax.fori_loop(0, S5_SEG, fix, 0, unroll=4)

    y_parts = []
    for q in range(n_lane_blk):
        out_cols = pl.ds(q * 128, 128)
        acc = None
        for part in range(2):
            rows = pl.ds(part * ns + q * st_blk, st_blk)
            d = jnp.dot(bu_ref[:, rows].astype(BF16), cblk_ref[rows, out_cols], preferred_element_type=F32)
            acc = d if acc is None else acc + d
        y_parts.append(acc)
    y_seg = jnp.concatenate(y_parts, axis=1)
    y_hi = y_seg.astype(BF16)
    y_lo = (y_seg - y_hi.astype(F32)).astype(BF16)
    y_ref[...] = (jnp.dot(permt_ref[...], y_hi, preferred_element_type=F32)
                  + jnp.dot(permt_ref[...], y_lo, preferred_element_type=F32))

    @pl.when(is_last)
    def _():
        fin_ref[...] = carry_ref[...]


def s5_direction(u, tables, h0, *, reverse):
    b_blk, c_blk, lam, lam_seg, pw = tables
    s5_col = 3 * HY_CH // S5_CH
    n_seq = N_CTX_SEQ + N_LAT_SEQ
    row_blk = lambda i: _s5_step_info(i, reverse)[0]
    seq = lambda i: _s5_step_info(i, reverse)[1]
    const = lambda shp: pl.BlockSpec(shp, lambda i: (0, 0))
    t = np.arange(S5_CHUNK)
    perm_np = np.zeros((S5_CHUNK, S5_CHUNK), np.float32)
    perm_np[(t % S5_SEG) * 8 + t // S5_SEG, t] = 1.0
    perm = jnp.asarray(perm_np, dtype=BF16)
    perm_t = jnp.asarray(perm_np.T, dtype=BF16)
    return pl.pallas_call(
        functools.partial(_s5_kernel, reverse=reverse),
        grid=(N_S5_STEPS,),
        in_specs=[
            pl.BlockSpec((S5_CHUNK, S5_CH), lambda i: (row_blk(i), s5_col)),
            const((S5_CHUNK, S5_CHUNK)), const((S5_CHUNK, S5_CHUNK)),
            const((S5_CH, 2 * S5_NS)), const((2 * S5_NS, S5_CH)), const((1, 2 * S5_NS)), const((1, 2 * S5_NS)),
            const((S5_SEG, 2 * S5_NS)),
            pl.BlockSpec((None, 1, 2 * S5_NS), lambda i: (seq(i), 0, 0)),
        ],
        out_specs=[pl.BlockSpec((S5_CHUNK, S5_CH), lambda i: (row_blk(i), 0)),
                   pl.BlockSpec((None, 1, 2 * S5_NS), lambda i: (seq(i), 0, 0))],
        out_shape=[jax.ShapeDtypeStruct((T_ALL, S5_CH), F32),
                   jax.ShapeDtypeStruct((n_seq, 1, 2 * S5_NS), F32)],
        scratch_shapes=[pltpu.VMEM((S5_CHUNK, 2 * S5_NS), F32), pltpu.VMEM((8, 2 * S5_NS), F32),
                        pltpu.VMEM((8, 2 * S5_NS), F32), pltpu.VMEM((1, 2 * S5_NS), F32)],
        compiler_params=_cparams(("arbitrary",)),
        name="s5_bwd" if reverse else "s5_fwd",
    )(u, perm, perm_t, b_blk, c_blk, lam, lam_seg, pw, h0)


def _s5_post_kernel(u_ref, yf_ref, yb_ref, d_ref, w_ref, b_ref, o_ref):
    y = d_ref[...] * u_ref[...] + yf_ref[...] + yb_ref[...]
    cdf = 0.5 * (1.0 + jnp.tanh(math.sqrt(2.0 / math.pi) * (y + 0.044715 * (y * y * y))))
    g = y * cdf
    z = jnp.dot(g.astype(BF16), w_ref[...], preferred_element_type=F32) + b_ref[...]
    o_ref[...] = z[:, :S5_CH] * jax.nn.sigmoid(z[:, S5_CH:])


def s5_post(u, yf, yb, d, glu_w, glu_b):
    row = pl.BlockSpec((ROW_TILE, S5_CH), lambda i: (i, 0))
    return pl.pallas_call(
        _s5_post_kernel,
        grid=(T_ALL // ROW_TILE,),
        in_specs=[pl.BlockSpec((ROW_TILE, S5_CH), lambda i: (i, 3 * HY_CH // S5_CH)), row, row,
                  pl.BlockSpec((1, S5_CH), lambda i: (0, 0)),
                  pl.BlockSpec((S5_CH, 2 * S5_CH), lambda i: (0, 0)),
                  pl.BlockSpec((1, 2 * S5_CH), lambda i: (0, 0))],
        out_specs=row,
        out_shape=jax.ShapeDtypeStruct((T_ALL, S5_CH), F32),
        compiler_params=_cparams(("parallel",)),
        name="s5_post",
    )(u, yf, yb, d, glu_w, glu_b)


def _head_mask(shape):
    return lax.broadcasted_iota(jnp.int32, shape, 1) < HEAD_DIM


def _ctx_attn_kernel(q_ref, k_ref, v_ref, prev_ref, o_ref, kcache_ref, vcache_ref):
    del prev_ref
    kcache_ref[...] = k_ref[...].reshape(CTX_LEN, N_HEADS, HEAD_DIM)
    vcache_ref[...] = v_ref[...].reshape(CTX_LEN, N_HEADS, HEAD_DIM)
    low = _head_mask((CTX_LEN, 128))
    for hp in range(N_HEADS // 2):
        cols = pl.ds(hp * 128, 128)
        a = _stack_heads(q_ref[:, cols] * (HEAD_DIM ** -0.5), low)
        k = k_ref[:, cols].astype(BF16)
        v = v_ref[:, cols].astype(BF16)
        s = lax.dot_general(a, k, (((1,), (1,)), ((), ())), preferred_element_type=F32)
        m = jnp.max(s, axis=-1, keepdims=True)
        p = jnp.exp(s - m)
        den = jnp.sum(p, axis=-1, keepdims=True)
        o = jnp.dot(p.astype(BF16), v, preferred_element_type=F32) / den
        o_ref[:, cols] = jnp.where(low, o[:CTX_LEN], o[CTX_LEN:])


def _stack_heads(q, low):
    return jnp.concatenate([jnp.where(low, q, 0.0), jnp.where(low, 0.0, q)], axis=0).astype(BF16)


def ctx_attention(qkv, attn_prev):
    blk = lambda part: pl.BlockSpec((CTX_LEN, D), lambda b: (b, part))
    cache_shape = jax.ShapeDtypeStruct((N_CTX_SEQ, 1, CTX_LEN, N_HEADS, HEAD_DIM), F32)
    cache_spec = pl.BlockSpec((None, None, CTX_LEN, N_HEADS, HEAD_DIM), lambda b: (b, 0, 0, 0, 0))
    return pl.pallas_call(
        _ctx_attn_kernel,
        grid=(N_CTX_SEQ,),
        in_specs=[blk(0), blk(1), blk(2), pl.BlockSpec(memory_space=pl.ANY)],
        out_specs=[pl.BlockSpec((CTX_LEN, D), lambda b: (b, 0)), cache_spec, cache_spec],
        out_shape=[jax.ShapeDtypeStruct((T_ALL, D), F32), cache_shape, cache_shape],
        input_output_aliases={3: 0},
        compiler_params=_cparams(("parallel",)),
        name="ctx_attention",
    )(qkv, qkv, qkv, attn_prev)


NA_ROWS_PER_STEP = 8


def _na_kernel(q_ref, k_ref, v_ref, kc_ref, vc_ref, bias_ref, prev_ref, o_ref):
    del prev_ref
    rb = pl.program_id(2)
    kc = kc_ref[...]
    vc = vc_ref[...]
    low = _head_mask((GRID_W, 128))
    n_rows = LAT_LEN // GRID_W
    for i in range(NA_ROWS_PER_STEP):
        r = rb * NA_ROWS_PER_STEP + i
        rs = jnp.clip(r - WIN_R // 2, 0, n_rows - WIN_R)
        dlt = r - rs
        rows = pl.ds(pl.multiple_of(rs * GRID_W, GRID_W), WIN_R * GRID_W)
        kw = k_ref[rows, :].astype(BF16)
        vw = v_ref[rows, :].astype(BF16)
        a = _stack_heads(q_ref[i * GRID_W:(i + 1) * GRID_W, :] * (HEAD_DIM ** -0.5), low)
        s = lax.dot_general(a, kw, (((1,), (1,)), ((), ())), preferred_element_type=F32)
        s = s + bias_ref[dlt].reshape(2 * GRID_W, WIN_R * GRID_W)
        sc = lax.dot_general(a, kc, (((1,), (1,)), ((), ())), preferred_element_type=F32)
        m = jnp.maximum(jnp.max(s, axis=-1, keepdims=True), jnp.max(sc, axis=-1, keepdims=True))
        p = jnp.exp(s - m)
        pc = jnp.exp(sc - m)
        den = jnp.sum(p, axis=-1, keepdims=True) + jnp.sum(pc, axis=-1, keepdims=True)
        o = (jnp.dot(p.astype(BF16), vw, preferred_element_type=F32)
             + jnp.dot(pc.astype(BF16), vc, preferred_element_type=F32)) / den
        o_ref[i * GRID_W:(i + 1) * GRID_W, :] = jnp.where(low, o[:GRID_W], o[GRID_W:])


def _na_bias_table(rpb):
    j = np.arange(GRID_W)
    cs = np.clip(j - WIN_C // 2, 0, GRID_W - WIN_C)
    c = np.arange(GRID_W)
    inside = (c[None, :] >= cs[:, None]) & (c[None, :] < cs[:, None] + WIN_C)
    col_rel = c[None, :] - j[:, None] + WIN_C - 1
    n_rel_c = 2 * WIN_C - 1
    onehot = ((np.arange(n_rel_c)[:, None, None] == col_rel[None]) & inside[None]).astype(np.float32)
    t = jnp.dot(rpb.reshape(N_HEADS * (2 * WIN_R - 1), n_rel_c), jnp.asarray(onehot.reshape(n_rel_c, -1)),
                precision=HIGHEST)
    t = t.reshape(N_HEADS, 2 * WIN_R - 1, GRID_W, GRID_W)
    t = t + jnp.asarray(np.where(inside, 0.0, NEG_BIG).astype(np.float32))
    t = jnp.stack([t[:, WIN_R - 1 - dl:2 * WIN_R - 1 - dl] for dl in range(WIN_R)], axis=0)
    t = t.transpose(0, 1, 3, 2, 4)
    return t.reshape(WIN_R, N_HEADS, GRID_W, WIN_R * GRID_W)


def na_attention(qkv, kcb, vcb, bias, attn_prev):
    hp = N_HEADS // 2
    q_rows = NA_ROWS_PER_STEP * GRID_W
    steps = LAT_LEN // q_rows
    q_off = T_CTX // q_rows
    seq_off = T_CTX // LAT_LEN
    return pl.pallas_call(
        _na_kernel,
        grid=(N_LAT_SEQ, hp, steps),
        in_specs=[
            pl.BlockSpec((q_rows, 128), lambda b, h, r: (q_off + b * steps + r, h)),
            pl.BlockSpec((LAT_LEN, 128), lambda b, h, r: (seq_off + b, hp + h)),
            pl.BlockSpec((LAT_LEN, 128), lambda b, h, r: (seq_off + b, 2 * hp + h)),
            pl.BlockSpec((CTX_LEN, 128), lambda b, h, r: (b, h)),
            pl.BlockSpec((CTX_LEN, 128), lambda b, h, r: (b, h)),
            pl.BlockSpec((WIN_R, 2, GRID_W, WIN_R * GRID_W), lambda b, h, r: (0, h, 0, 0)),
            pl.BlockSpec(memory_space=pl.ANY),
        ],
        out_specs=pl.BlockSpec((q_rows, 128), lambda b, h, r: (q_off + b * steps + r, h)),
        out_shape=jax.ShapeDtypeStruct((T_ALL, D), F32),
        input_output_aliases={6: 0},
        compiler_params=_cparams(("parallel", "parallel", "arbitrary")),
        name="na_attention",
    )(qkv, qkv, qkv, kcb, vcb, bias, attn_prev)


def _layer_norm(r, g, b):
    mu = jnp.mean(r, axis=-1, keepdims=True)
    c = r - mu
    var = jnp.mean(c * c, axis=-1, keepdims=True)
    return c * lax.rsqrt(var + LN_EPS) * g + b


def _proj_kernel(*refs, n_in):
    a_refs = refs[:n_in]
    w_refs = refs[n_in:2 * n_in]
    (x_ref, mod_ref, g_ref, b_ref, rwh_ref, rwl_ref, rb_ref, tri_ref,
     x1_ref, h2_ref, exp_ref, pos_ref, gate_ref, cnt_ref, run_ref) = refs[2 * n_in:]

    @pl.when(pl.program_id(0) == 0)
    def _():
        run_ref[...] = jnp.zeros_like(run_ref)

    y = None
    for a_ref, w_ref in zip(a_refs, w_refs):
        part = jnp.dot(a_ref[...].astype(BF16), w_ref[...], preferred_element_type=F32)
        y = part if y is None else y + part
    m = mod_ref[...]
    x1 = _layer_norm(DN_ALPHA * x_ref[...] + m[2:3] * y, g_ref[...], b_ref[...])
    x1_ref[...] = x1
    h2 = x1 * (1.0 + m[4:5]) + m[3:4]
    h2_ref[...] = h2.reshape(ROW_TILE, *ROW_AS_TILE)

    hh = h2.astype(BF16)
    hl = (h2 - hh.astype(F32)).astype(BF16)
    logits = (jnp.dot(hh, rwh_ref[...], preferred_element_type=F32)
              + jnp.dot(hh, rwl_ref[...], preferred_element_type=F32)
              + jnp.dot(hl, rwh_ref[...], preferred_element_type=F32)) + rb_ref[...]

    lane = lax.broadcasted_iota(jnp.int32, logits.shape, 1)
    work = logits
    vals, sels = [], []
    for k in range(TOP_K):
        mx = jnp.max(work, axis=-1, keepdims=True)
        idx = jnp.min(jnp.where(work == mx, lane, 128), axis=-1, keepdims=True)
        sel = lane == idx
        work = jnp.where(sel, -jnp.inf, work)
        exp_ref[:, k:k + 1] = idx
        vals.append(mx)
        sels.append(sel)
    exps = [jnp.exp(v - vals[0]) for v in vals]
    den = exps[0] + exps[1] + exps[2] + exps[3]
    for k in range(TOP_K):
        gate_ref[:, k:k + 1] = exps[k] / den

    picked = sels[0] | sels[1] | sels[2] | sels[3]
    onehot = jnp.where(picked, 1.0, 0.0)
    incl = jnp.dot(tri_ref[...], onehot.astype(BF16), preferred_element_type=F32)
    before = incl - onehot + run_ref[...]
    for k in range(TOP_K):
        pos = jnp.sum(jnp.where(sels[k], before, 0.0), axis=-1, keepdims=True)
        pos_ref[:, k:k + 1] = pos.astype(jnp.int32)
    run_ref[...] += jnp.sum(onehot, axis=0, keepdims=True)
    cnt_ref[...] = run_ref[...]


def proj_res_ln(acts, weights, x, mod, ln_g, ln_b, rw_hi, rw_lo, rb):
    n_in = len(acts)
    row = lambda c: pl.BlockSpec((ROW_TILE, c), lambda i: (i, 0))
    full = lambda shp: pl.BlockSpec(shp, lambda i: (0, 0))
    tri = jnp.asarray(np.tril(np.ones((ROW_TILE, ROW_TILE), np.float32)), dtype=BF16)
    in_specs = ([row(a.shape[1]) for a in acts] + [full(w.shape) for w in weights]
                + [row(D), pl.BlockSpec((None, 8, D), lambda i: (_mod_index(i, ROW_TILE), 0, 0)),
                   full((1, D)), full((1, D)), full((D, 128)), full((D, 128)), full((1, 128)),
                   full((ROW_TILE, ROW_TILE))])
    return pl.pallas_call(
        functools.partial(_proj_kernel, n_in=n_in),
        grid=(T_ALL // ROW_TILE,),
        in_specs=in_specs,
        out_specs=[row(D), pl.BlockSpec((ROW_TILE, *ROW_AS_TILE), lambda i: (i, 0, 0)),
                   row(TOP_K), row(TOP_K), row(TOP_K), full((1, 128))],
        out_shape=[jax.ShapeDtypeStruct((T_ALL, D), F32), jax.ShapeDtypeStruct((T_ALL, *ROW_AS_TILE), F32),
                   jax.ShapeDtypeStruct((T_ALL, TOP_K), jnp.int32), jax.ShapeDtypeStruct((T_ALL, TOP_K), jnp.int32),
                   jax.ShapeDtypeStruct((T_ALL, TOP_K), F32), jax.ShapeDtypeStruct((1, 128), F32)],
        scratch_shapes=[pltpu.VMEM((1, 128), F32)],
        compiler_params=_cparams(("arbitrary",)),
        name="proj_res_ln",
    )(*acts, *weights, x, mod, ln_g, ln_b, rw_hi, rw_lo, rb, tri)


def _dispatch_kernel(dest_ref, h_ref, xs_ref, sem):
    def copy(r, d):
        return pltpu.make_async_copy(h_ref.at[r], xs_ref.at[d], sem)

    def issue(r, _):
        for k in range(TOP_K):
            copy(r, dest_ref[r * TOP_K + k]).start(priority=k % 2)
        return 0

    lax.fori_loop(0, ROW_TILE, issue, 0, unroll=4)

    def drain(r, _):
        for k in range(TOP_K):
            copy(r, dest_ref[r * TOP_K + k]).wait()
        return 0

    lax.fori_loop(0, ROW_TILE, drain, 0, unroll=4)


def moe_dispatch(h2, dest_flat):
    return pl.pallas_call(
        _dispatch_kernel,
        grid=(T_ALL // ROW_TILE,),
        in_specs=[pl.BlockSpec((ROW_TILE * TOP_K,), lambda i: (i,), memory_space=pltpu.SMEM),
                  pl.BlockSpec((ROW_TILE, *ROW_AS_TILE), lambda i: (i, 0, 0))],
        out_specs=pl.BlockSpec(memory_space=pl.ANY),
        out_shape=jax.ShapeDtypeStruct((N_ASSIGN, *ROW_AS_TILE), F32),
        scratch_shapes=[pltpu.SemaphoreType.DMA],
        compiler_params=_cparams(("arbitrary",)),
        name="moe_dispatch",
    )(dest_flat, h2)


W_CAST_ROWS = 128


def _moe_ffn_kernel(blk_ref, exp_ref, lo_ref, hi_ref, first_ref, newexp_ref, nxt_ref,
                    x_ref, w1_ref, b1_ref, w2_ref, b2_ref, o_ref, w1f_ref, w2f_ref, w1b_ref, w2b_ref, sems, *, layer):
    i = pl.program_id(0)
    lo = lo_ref[i]
    hi = hi_ref[i]

    def weights(e):
        return (pltpu.make_async_copy(w1_ref.at[layer, e], w1f_ref, sems.at[0]),
                pltpu.make_async_copy(w2_ref.at[layer, e], w2f_ref, sems.at[1]))

    @pl.when(i == 0)
    def _():
        for c in weights(exp_ref[0]):
            c.start()

    @pl.when(newexp_ref[i] == 1)
    def _():
        for c in weights(exp_ref[i]):
            c.wait()

        def cast(c, _):
            rows = pl.ds(pl.multiple_of(c * W_CAST_ROWS, W_CAST_ROWS), W_CAST_ROWS)
            w1b_ref[rows, :] = w1f_ref[rows, :].astype(BF16)
            w2b_ref[rows, :] = w2f_ref[rows, :].astype(BF16)
            return 0

        lax.fori_loop(0, D // W_CAST_ROWS, cast, 0)

        @pl.when(nxt_ref[i] >= 0)
        def _():
            for c in weights(nxt_ref[i]):
                c.start()

    @pl.when(first_ref[i] == 1)
    def _():
        o_ref[...] = jnp.zeros_like(o_ref)

    @pl.when(hi > lo)
    def _():
        x = x_ref[...].reshape(MOE_TILE, D).astype(BF16)
        h = jnp.dot(x, w1b_ref[...], preferred_element_type=F32) + b1_ref[...]
        g = jnp.minimum(h[:, :D_FF], SWIGLU_LIMIT)
        u = jnp.clip(h[:, D_FF:], -SWIGLU_LIMIT, SWIGLU_LIMIT)
        a = g * jax.nn.sigmoid(SWIGLU_ALPHA * g) * (u + 1.0)
        y = jnp.dot(a.astype(BF16), w2b_ref[...], preferred_element_type=F32) + b2_ref[...]
        row = lax.broadcasted_iota(jnp.int32, (MOE_TILE, 1), 0)
        o_ref[...] += jnp.where((row >= lo) & (row < hi), y, 0.0).reshape(MOE_TILE, *ROW_AS_TILE)


def moe_ffn(items, xs, w1, b1, w2, b2, layer):
    assert D == D_FF
    blk, exp, lo, hi, first, newexp, nxt = items
    grid_spec = pltpu.PrefetchScalarGridSpec(
        num_scalar_prefetch=7,
        grid=(N_MOE_ITEMS,),
        in_specs=[
            pl.BlockSpec((MOE_TILE, *ROW_AS_TILE), lambda i, blk, exp, *_: (blk[i], 0, 0)),
            pl.BlockSpec(memory_space=pl.ANY),
            pl.BlockSpec((None, None, 1, 2 * D_FF), lambda i, blk, exp, *_: (layer, exp[i], 0, 0)),
            pl.BlockSpec(memory_space=pl.ANY),
            pl.BlockSpec((None, None, 1, D), lambda i, blk, exp, *_: (layer, exp[i], 0, 0)),
        ],
        out_specs=pl.BlockSpec((MOE_TILE, *ROW_AS_TILE), lambda i, blk, exp, *_: (blk[i], 0, 0)),
        scratch_shapes=[pltpu.VMEM((D, 2 * D_FF), F32), pltpu.VMEM((D_FF, D), F32),
                        pltpu.VMEM((D, 2 * D_FF), BF16), pltpu.VMEM((D_FF, D), BF16),
                        pltpu.SemaphoreType.DMA((2,))],
    )
    return pl.pallas_call(
        functools.partial(_moe_ffn_kernel, layer=layer),
        grid_spec=grid_spec,
        out_shape=jax.ShapeDtypeStruct((N_ASSIGN, *ROW_AS_TILE), F32),
        compiler_params=_cparams(("arbitrary",)),
        name="moe_ffn",
    )(blk, exp, lo, hi, first, newexp, nxt, xs, w1, b1, w2, b2)


N_COMB_STEPS = T_ALL // COMB_TILE


def _combine_kernel(dest_ref, dest_next_ref, ys_ref, gates_ref, x_ref, mod_ref, g_ref, b_ref, o_ref, buf_ref, sems):
    i = pl.program_id(0)
    slot = i % 2

    def copy(idx_ref, s, r, k):
        d = idx_ref[r * TOP_K + k]
        return pltpu.make_async_copy(ys_ref.at[d], buf_ref.at[s, k, r], sems.at[s])

    def gather(idx_ref, s):
        def issue(r, _):
            for k in range(TOP_K):
                copy(idx_ref, s, r, k).start(priority=k % 2)
            return 0

        lax.fori_loop(0, COMB_TILE, issue, 0, unroll=4)

    @pl.when(i == 0)
    def _():
        gather(dest_ref, 0)

    @pl.when(i + 1 < N_COMB_STEPS)
    def _():
        gather(dest_next_ref, 1 - slot)

    def drain(r, _):
        for k in range(TOP_K):
            copy(dest_ref, slot, r, k).wait()
        return 0

    lax.fori_loop(0, COMB_TILE, drain, 0, unroll=4)

    gates = gates_ref[...]
    y = gates[:, 0:1] * buf_ref[slot, 0].reshape(COMB_TILE, D)
    for k in range(1, TOP_K):
        y = y + gates[:, k:k + 1] * buf_ref[slot, k].reshape(COMB_TILE, D)
    m = mod_ref[...]
    o_ref[...] = _layer_norm(DN_ALPHA * x_ref[...] + m[5:6] * y, g_ref[...], b_ref[...])


def moe_combine(ys, dest_flat, gates, x1, mod, ln_g, ln_b):
    full = lambda shp: pl.BlockSpec(shp, lambda i: (0, 0))
    idx = lambda fn: pl.BlockSpec((COMB_TILE * TOP_K,), fn, memory_space=pltpu.SMEM)
    return pl.pallas_call(
        _combine_kernel,
        grid=(N_COMB_STEPS,),
        in_specs=[idx(lambda i: (i,)), idx(lambda i: (jnp.minimum(i + 1, N_COMB_STEPS - 1),)),
                  pl.BlockSpec(memory_space=pl.ANY),
                  pl.BlockSpec((COMB_TILE, TOP_K), lambda i: (i, 0)),
                  pl.BlockSpec((COMB_TILE, D), lambda i: (i, 0)),
                  pl.BlockSpec((None, 8, D), lambda i: (_mod_index(i, COMB_TILE), 0, 0)),
                  full((1, D)), full((1, D))],
        out_specs=pl.BlockSpec((COMB_TILE, D), lambda i: (i, 0)),
        out_shape=jax.ShapeDtypeStruct((T_ALL, D), F32),
        scratch_shapes=[pltpu.VMEM((2, TOP_K, COMB_TILE, *ROW_AS_TILE), F32), pltpu.SemaphoreType.DMA((2,))],
        compiler_params=_cparams(("arbitrary",)),
        name="moe_combine",
    )(dest_flat, dest_flat, ys, gates, x1, mod, ln_g, ln_b)


def _routing_tables(experts, pos, counts):
    counts = counts[0, :N_EXPERTS].astype(jnp.int32)
    starts = jnp.cumsum(counts) - counts
    eids = jnp.arange(N_EXPERTS, dtype=jnp.int32)
    start_of_pick = jnp.sum(jnp.where(experts[..., None] == eids, starts, 0), axis=-1)
    dest = start_of_pick + pos
    bnd = jnp.sort(jnp.concatenate([jnp.arange(N_MOE_BLOCKS + 1, dtype=jnp.int32) * MOE_TILE, starts[1:]]))
    a, b = bnd[:-1], bnd[1:]
    blk = jnp.minimum(a // MOE_TILE, N_MOE_BLOCKS - 1)
    lo = a - blk * MOE_TILE
    hi = b - blk * MOE_TILE
    exp = jnp.clip(jnp.sum((starts[None, :] <= a[:, None]).astype(jnp.int32), axis=1) - 1, 0, N_EXPERTS - 1)
    one = jnp.ones((1,), jnp.int32)
    first = jnp.concatenate([one, (blk[1:] != blk[:-1]).astype(jnp.int32)])
    newexp = jnp.concatenate([one, (exp[1:] != exp[:-1]).astype(jnp.int32)])
    idx = jnp.arange(N_MOE_ITEMS, dtype=jnp.int32)
    change_at = jnp.where(newexp == 1, idx, N_MOE_ITEMS)
    next_change = jnp.concatenate([lax.cummin(change_at, reverse=True)[1:], jnp.full((1,), N_MOE_ITEMS, jnp.int32)])
    nxt = jnp.where(next_change < N_MOE_ITEMS, exp[jnp.minimum(next_change, N_MOE_ITEMS - 1)], -1)
    items = tuple(v.astype(jnp.int32) for v in (blk, exp, lo, hi, first, newexp, nxt))
    return dest.astype(jnp.int32).reshape(-1), items


def moe_layer(h2, experts, pos, gates, counts, x1, mod, ln_g, ln_b, w1, b1, w2, b2, layer):
    dest_flat, items = _routing_tables(experts, pos, counts)
    xs = moe_dispatch(h2, dest_flat)
    ys = moe_ffn(items, xs, w1, b1, w2, b2, layer)
    return moe_combine(ys, dest_flat, gates, x1, mod, ln_g, ln_b)


def _router_operands(router_w, router_b):
    w = jnp.pad(router_w, ((0, 0), (0, 128 - N_EXPERTS)))
    hi = w.astype(BF16)
    lo = (w - hi.astype(F32)).astype(BF16)
    b = jnp.concatenate([router_b, jnp.full((128 - N_EXPERTS,), -jnp.inf, F32)])[None]
    return hi, lo, b


def kernel(x_prompt, x_sample, state_s5, cache_na_k, cache_na_v, c, c_ctx, w_mod, b_mod, ln1_g, ln1_b, ln2_g, ln2_b, ev_w_in, ev_w_out, hy_conv_w, hy_conv_b, hy_f_w1, hy_f_b1, hy_f_w2, hy_f_b2, hy_f_w3, hy_f_freq, hy_bias, s5_lam_re, s5_lam_im, s5_log_dt, s5_b_re, s5_b_im, s5_c_re, s5_c_im, s5_d, s5_glu_w, s5_glu_b, od_w_in, od_w_out, na_rpb, router_w, router_b, moe_w1, moe_b1, moe_w2, moe_b2):
    x = jnp.concatenate([x_prompt.reshape(T_CTX, D), x_sample.reshape(T_LAT, D)], axis=0)
    cvec = jnp.pad(jnp.concatenate([c_ctx[None], c], axis=0), ((0, 5), (0, 0)))

    new_state = None
    new_k = new_v = None
    for l in range(DEPTH):
        i = l // 2
        mod = matmul(cvec, w_mod, b_mod[l][None], b_layer=l, tm=8, tn=512, tk=D, precise=True, silu_a=True)
        mod = jnp.pad(mod.reshape(8, 6, D)[:3], ((0, 0), (0, 2), (0, 0)))

        if l % 2 == 0:
            u = modlinear(x, mod, ev_w_in[i].astype(BF16))
            y_hy = jnp.zeros((T_ALL, HY_CH), F32)
            for n_seq, seq_len, row_off, tile in ((N_CTX_SEQ, CTX_LEN, 0, CTX_LEN), (N_LAT_SEQ, LAT_LEN, T_CTX, 1024)):
                taps = hyena_filter_taps(seq_len, hy_f_w1[i], hy_f_b1[i], hy_f_w2[i], hy_f_b2[i], hy_f_w3[i],
                                         hy_f_freq[i])
                y_hy = hyena_group(u, hy_conv_w[i], hy_conv_b[i][None], taps, hy_bias[i][None], y_hy,
                                   n_seq=n_seq, seq_len=seq_len, row_off=row_off, tile=tile)

            h0_lat = state_s5[:, i]
            yf_parts, yb_parts, finals = [], [], []
            for r in range(2):
                tables = _s5_tables(s5_lam_re[i, r], s5_lam_im[i, r], s5_log_dt[i, r], s5_b_re[i, r], s5_b_im[i, r],
                                    s5_c_re[i, r], s5_c_im[i, r], reverse=(r == 1))
                h0_l = jnp.concatenate([h0_lat[:, r, :, :, 0].reshape(N_LAT_SEQ, 1, S5_NS),
                                        h0_lat[:, r, :, :, 1].reshape(N_LAT_SEQ, 1, S5_NS)], axis=-1)
                h0 = jnp.concatenate([jnp.zeros((N_CTX_SEQ, 1, 2 * S5_NS), F32), h0_l], axis=0)
                y_dir, fin_all = s5_direction(u, tables, h0, reverse=(r == 1))
                (yf_parts if r == 0 else yb_parts).append(y_dir)
                fin = fin_all[:N_CTX_SEQ].reshape(N_CTX_SEQ, 2, S5_GROUPS, S5_STATE)
                finals.append(jnp.stack([fin[:, 0], fin[:, 1]], axis=-1))
            new_state = jnp.stack(finals, axis=1)[:, None]
            y_s5 = s5_post(u, yf_parts[0], yb_parts[0], s5_d[i].reshape(1, S5_CH), s5_glu_w[i].astype(BF16),
                           s5_glu_b[i][None])
            acts = [y_hy, y_s5]
            w_out = ev_w_out[i].astype(BF16)
            weights = [w_out[:HY_CH], w_out[HY_CH:]]
        else:
            qkv = modlinear(x, mod, od_w_in[i].astype(BF16))
            attn, new_k, new_v = ctx_attention(qkv, jnp.zeros((T_ALL, D), F32))
            kcb = cache_na_k[:, i].reshape(N_LAT_SEQ * CTX_LEN, D).astype(BF16)
            vcb = cache_na_v[:, i].reshape(N_LAT_SEQ * CTX_LEN, D).astype(BF16)
            attn = na_attention(qkv, kcb, vcb, _na_bias_table(na_rpb[i]), attn)
            acts = [attn]
            weights = [od_w_out[i].astype(BF16)]

        rw_hi, rw_lo, rb = _router_operands(router_w[l], router_b[l])
        x1, h2, experts, pos, gates, counts = proj_res_ln(acts, weights, x, mod, ln1_g[l][None], ln1_b[l][None],
                                                          rw_hi, rw_lo, rb)
        x = moe_layer(h2, experts, pos, gates, counts, x1, mod, ln2_g[l][None], ln2_b[l][None],
                      moe_w1, moe_b1[:, :, None, :], moe_w2, moe_b2[:, :, None, :], l)

    y_prompt = x[:T_CTX].reshape(N_CTX_SEQ, CTX_LEN, D)
    y_sample = x[T_CTX:].reshape(N_LAT_SEQ, LAT_LEN, D)
    return (y_prompt, y_sample, new_state, new_k, new_v)
```

```python
import functools
import math

import jax
import jax.numpy as jnp
import numpy as np
from jax import lax
from jax.experimental import pallas as pl
from jax.experimental.pallas import tpu as pltpu

F32 = jnp.float32
BF16 = jnp.bfloat16
HIGHEST = lax.Precision.HIGHEST

D = 1024
N_CTX_SEQ, CTX_LEN = 32, 256
N_LAT_SEQ, LAT_LEN = 2, 4096
T_CTX = N_CTX_SEQ * CTX_LEN
T_LAT = N_LAT_SEQ * LAT_LEN
T_ALL = T_CTX + T_LAT
DEPTH = 2
HY_CH = 512
S5_CH = 512
S5_GROUPS, S5_GROUP, S5_STATE = 32, 16, 64
S5_NS = S5_GROUPS * S5_STATE
HY_EMB, HY_BANDS, HY_ORDER = 33, 16, 64
N_HEADS, HEAD_DIM = 16, 64
GRID_W, WIN_R, WIN_C = 64, 8, 16
N_EXPERTS, TOP_K, D_FF = 32, 4, 1024
SWIGLU_LIMIT, SWIGLU_ALPHA = 7.0, 1.702
LN_EPS = 1e-5
DN_ALPHA = (2 * DEPTH) ** 0.25
NEG_BIG = -1e30

ROW_TILE = 256
MOE_TILE = 256
N_ASSIGN = T_ALL * TOP_K
N_MOE_BLOCKS = N_ASSIGN // MOE_TILE
N_MOE_ITEMS = N_MOE_BLOCKS + N_EXPERTS - 1
COMB_TILE = 128
ROW_AS_TILE = (8, D // 8)
S5_CHUNK = 256
S5_SEG = S5_CHUNK // 8
S5_COLS = 512
S5_LANE_BLOCKS = S5_CH // 128
S5_LANE_GROUPS = S5_GROUPS // S5_LANE_BLOCKS
S5_LANE_STATES = S5_LANE_GROUPS * S5_STATE
VMEM_LIMIT = 56 * 1024 * 1024


def _cparams(sem, vmem=None):
    return pltpu.CompilerParams(dimension_semantics=sem, vmem_limit_bytes=vmem or VMEM_LIMIT)


def _mod_index(i, rows_per_tile):
    n_ctx = T_CTX // rows_per_tile
    per_lat = LAT_LEN // rows_per_tile
    return jnp.where(i < n_ctx, 0, 1 + (i - n_ctx) // per_lat)


def _mm_kernel(a_ref, b_ref, bias_ref, o_ref, acc_ref, *, nk, precise, silu_a):
    k = pl.program_id(2)

    @pl.when(k == 0)
    def _():
        acc_ref[...] = jnp.zeros_like(acc_ref)

    a = a_ref[...]
    if silu_a:
        a = a * jax.nn.sigmoid(a)
    if precise:
        acc_ref[...] += jnp.dot(a, b_ref[...], preferred_element_type=F32, precision=HIGHEST)
    else:
        acc_ref[...] += jnp.dot(a.astype(BF16), b_ref[...].astype(BF16), preferred_element_type=F32)

    @pl.when(k == nk - 1)
    def _():
        o_ref[...] = acc_ref[...] + bias_ref[...]


def matmul(a, b, bias=None, *, tm, tn, tk, b_layer=None, precise=False, silu_a=False):
    kdim, n = b.shape[-2:]
    m = a.shape[0]
    if bias is None:
        bias = jnp.zeros((1, n), F32)
    nk = kdim // tk
    if b.ndim == 3:
        b_spec = pl.BlockSpec((None, tk, tn), lambda i, j, k: (b_layer, k, j))
    else:
        b_spec = pl.BlockSpec((tk, tn), lambda i, j, k: (k, j))
    return pl.pallas_call(
        functools.partial(_mm_kernel, nk=nk, precise=precise, silu_a=silu_a),
        grid=(m // tm, n // tn, nk),
        in_specs=[pl.BlockSpec((tm, tk), lambda i, j, k: (i, k)), b_spec,
                  pl.BlockSpec((1, tn), lambda i, j, k: (0, j))],
        out_specs=pl.BlockSpec((tm, tn), lambda i, j, k: (i, j)),
        out_shape=jax.ShapeDtypeStruct((m, n), F32),
        scratch_shapes=[pltpu.VMEM((tm, tn), F32)],
        compiler_params=_cparams(("parallel", "parallel", "arbitrary")),
        name="matmul",
    )(a, b, bias)


def _modlinear_kernel(x_ref, mod_ref, w_ref, o_ref):
    m = mod_ref[...]
    h = x_ref[...] * (1.0 + m[1:2]) + m[0:1]
    o_ref[...] = jnp.dot(h.astype(BF16), w_ref[...], preferred_element_type=F32)


def modlinear(x, mod, w_bf16):
    n = w_bf16.shape[1]
    return pl.pallas_call(
        _modlinear_kernel,
        grid=(T_ALL // ROW_TILE,),
        in_specs=[
            pl.BlockSpec((ROW_TILE, D), lambda i: (i, 0)),
            pl.BlockSpec((None, 8, D), lambda i: (_mod_index(i, ROW_TILE), 0, 0)),
            pl.BlockSpec((D, n), lambda i: (0, 0)),
        ],
        out_specs=pl.BlockSpec((ROW_TILE, n), lambda i: (i, 0)),
        out_shape=jax.ShapeDtypeStruct((T_ALL, n), F32),
        compiler_params=_cparams(("parallel",)),
        name="modlinear",
    )(x, mod, w_bf16)


def _filter_kernel(z_ref, w1_ref, b1_ref, w2_ref, b2_ref, w3_ref, fq_ref, dl_ref, hsum_ref, hdiff_ref, *, tile):
    z = z_ref[...]
    fq = fq_ref[...]
    h = jnp.sin(fq * (jnp.dot(z, w1_ref[...], preferred_element_type=F32, precision=HIGHEST) + b1_ref[...]))
    h = jnp.sin(fq * (jnp.dot(h, w2_ref[...], preferred_element_type=F32, precision=HIGHEST) + b2_ref[...]))
    h = jnp.dot(h, w3_ref[...], preferred_element_type=F32, precision=HIGHEST)
    decay = jnp.exp(-z[:, 0:1] * dl_ref[...])
    hf = h[:, :HY_CH] * decay
    hb = h[:, HY_CH:] * decay
    row = lax.broadcasted_iota(jnp.int32, (tile, 1), 0) + pl.program_id(0) * tile
    hsum_ref[...] = hf + hb
    hdiff_ref[...] = jnp.where(row == 0, hf + hb, hf - hb)


def hyena_filter_taps(seq_len, w1, b1, w2, b2, w3, freq):
    t = jnp.linspace(0.0, 1.0, seq_len, dtype=F32)[:, None]
    w = 2.0 * math.pi * jnp.arange(seq_len, dtype=F32)[:, None] / seq_len
    f = jnp.linspace(1e-4, HY_BANDS - 1, HY_BANDS, dtype=F32)[None, :]
    z = jnp.concatenate([t, jnp.cos(f * w), -jnp.sin(f * w)], -1)
    z = jnp.pad(z, ((0, 0), (0, 128 - HY_EMB)))
    pad_o = 128 - HY_ORDER
    w1p = jnp.pad(w1, ((0, 128 - HY_EMB), (0, pad_o)))
    w2p = jnp.pad(w2, ((0, pad_o), (0, pad_o)))
    w3p = jnp.pad(w3, ((0, pad_o), (0, 0)))
    b1p = jnp.pad(b1, (0, pad_o))[None]
    b2p = jnp.pad(b2, (0, pad_o))[None]
    fqp = jnp.pad(freq, (0, pad_o))[None]
    max_decay = math.log(1e-2) / 0.3
    min_decay = math.log(1e-2) / 1.5
    absdelta = jnp.abs(jnp.linspace(min_decay, max_decay, HY_CH, dtype=F32))[None]
    tile = 256
    full = lambda shp: pl.BlockSpec(shp, lambda i: (0, 0))
    return pl.pallas_call(
        functools.partial(_filter_kernel, tile=tile),
        grid=(seq_len // tile,),
        in_specs=[pl.BlockSpec((tile, 128), lambda i: (i, 0)), full((128, 128)), full((1, 128)), full((128, 128)),
                  full((1, 128)), full((128, 2 * HY_CH)), full((1, 128)), full((1, HY_CH))],
        out_specs=[pl.BlockSpec((tile, HY_CH), lambda i: (i, 0))] * 2,
        out_shape=[jax.ShapeDtypeStruct((seq_len, HY_CH), F32)] * 2,
        compiler_params=_cparams(("parallel",)),
        name="hyena_filter",
    )(z, w1p, b1p, w2p, b2p, w3p, fqp, absdelta)


def _dft_tables(seq_len):
    s = int(round(math.sqrt(seq_len)))
    j = jnp.arange(s, dtype=jnp.int32)[:, None]
    c = jnp.arange(seq_len, dtype=jnp.int32)[None, :]
    unit = math.pi / (2 * seq_len)

    def cos_sin(idx):
        ang = (idx % (4 * seq_len)).astype(F32) * unit
        return jnp.cos(ang), jnp.sin(ang)

    def expand(ca, sa, cb, sb):
        cos = ca[:, None, :] * cb[None, :, :] - sa[:, None, :] * sb[None, :, :]
        nsin = -(sa[:, None, :] * cb[None, :, :] + ca[:, None, :] * sb[None, :, :])
        return cos.reshape(seq_len, seq_len).astype(BF16), nsin.reshape(seq_len, seq_len).astype(BF16)

    cm, nsm = expand(*cos_sin((2 * s * j) * c), *cos_sin((2 * j + 1) * c))
    cmt, nsmt = expand(*cos_sin((2 * c + 1) * (s * j)), *cos_sin((2 * c + 1) * j))
    return cm, nsm, cmt, nsmt


def _hyena_pre_kernel(u0_ref, u1_ref, u2_ref, w0_ref, w1_ref, w2_ref, b0_ref, b1_ref, b2_ref,
                      vx_ref, vxb_ref, x0_ref, *, seq_len):
    row = lax.broadcasted_iota(jnp.int32, (seq_len, 1), 0)

    def short_conv(u_ref, w_ref, b_ref):
        a = u_ref[...]
        w = w_ref[...]
        prev = jnp.where(row == 0, 0.0, pltpu.roll(a, 1, 0))
        nxt = jnp.where(row == seq_len - 1, 0.0, pltpu.roll(a, seq_len - 1, 0))
        return prev * w[0:1] + a * w[1:2] + nxt * w[2:3] + b_ref[...]

    x0 = short_conv(u0_ref, w0_ref, b0_ref)
    x1 = short_conv(u1_ref, w1_ref, b1_ref)
    v = short_conv(u2_ref, w2_ref, b2_ref)
    vx = v * x1
    vx_ref[...] = vx
    vxb_ref[...] = vx.astype(BF16)
    x0_ref[...] = x0


def hyena_pre(u, conv_w, conv_b, *, n_seq, seq_len, row_off):
    cb = min(HY_CH, 128 * LAT_LEN // seq_len)
    ncb = HY_CH // cb
    rb0 = row_off // seq_len
    uspec = lambda part: pl.BlockSpec((seq_len, cb), lambda b, j: (rb0 + b, part * ncb + j))
    wspec = lambda part: pl.BlockSpec((3, cb), lambda b, j: (0, part * ncb + j))
    bspec = lambda part: pl.BlockSpec((1, cb), lambda b, j: (0, part * ncb + j))
    ospec = pl.BlockSpec((seq_len, cb), lambda b, j: (b, j))
    rows = n_seq * seq_len
    return pl.pallas_call(
        functools.partial(_hyena_pre_kernel, seq_len=seq_len),
        grid=(n_seq, ncb),
        in_specs=[uspec(0), uspec(1), uspec(2), wspec(0), wspec(1), wspec(2), bspec(0), bspec(1), bspec(2)],
        out_specs=[ospec, ospec, ospec],
        out_shape=[jax.ShapeDtypeStruct((rows, HY_CH), F32), jax.ShapeDtypeStruct((rows, HY_CH), BF16),
                   jax.ShapeDtypeStruct((rows, HY_CH), F32)],
        compiler_params=_cparams(("parallel", "parallel")),
        name="hyena_pre",
    )(u, u, u, conv_w, conv_w, conv_w, conv_b, conv_b, conv_b)


def _dft_fwd_kernel(cm_ref, nsm_ref, v_ref, hr_ref, hi_ref, zr_ref, zi_ref, accr_ref, acci_ref, *, nk):
    k = pl.program_id(3)

    @pl.when(k == 0)
    def _():
        accr_ref[...] = jnp.zeros_like(accr_ref)
        acci_ref[...] = jnp.zeros_like(acci_ref)

    v = v_ref[...]
    accr_ref[...] += jnp.dot(cm_ref[...], v, preferred_element_type=F32)
    acci_ref[...] += jnp.dot(nsm_ref[...], v, preferred_element_type=F32)

    @pl.when(k == nk - 1)
    def _():
        xr, xi = accr_ref[...], acci_ref[...]
        hr, hi = hr_ref[...], hi_ref[...]
        zr_ref[...] = (xr * hr - xi * hi).astype(BF16)
        zi_ref[...] = (xr * hi + xi * hr).astype(BF16)


def dft_fwd(cm, nsm, vxb, hr, hi, *, n_seq, seq_len, tile):
    nk = seq_len // tile
    tn = HY_CH
    v3 = vxb.reshape(n_seq, seq_len, HY_CH)
    zspec = pl.BlockSpec((None, tile, tn), lambda s, i, j, k: (s, i, j))
    return pl.pallas_call(
        functools.partial(_dft_fwd_kernel, nk=nk),
        grid=(n_seq, seq_len // tile, HY_CH // tn, nk),
        in_specs=[
            pl.BlockSpec((tile, tile), lambda s, i, j, k: (i, k)),
            pl.BlockSpec((tile, tile), lambda s, i, j, k: (i, k)),
            pl.BlockSpec((None, tile, tn), lambda s, i, j, k: (s, k, j)),
            pl.BlockSpec((tile, tn), lambda s, i, j, k: (i, j)),
            pl.BlockSpec((tile, tn), lambda s, i, j, k: (i, j)),
        ],
        out_specs=[zspec, zspec],
        out_shape=[jax.ShapeDtypeStruct((n_seq, seq_len, HY_CH), BF16)] * 2,
        scratch_shapes=[pltpu.VMEM((tile, tn), F32), pltpu.VMEM((tile, tn), F32)],
        compiler_params=_cparams(("parallel", "parallel", "parallel", "arbitrary")),
        name="dft_fwd",
    )(cm, nsm, v3, hr, hi)


def _dft_inv_kernel(cmt_ref, nsmt_ref, zr_ref, zi_ref, vx_ref, x0_ref, bias_ref, o_ref, acc_ref, *, nk, inv_len):
    k = pl.program_id(3)

    @pl.when(k == 0)
    def _():
        acc_ref[...] = jnp.zeros_like(acc_ref)

    acc_ref[...] += (jnp.dot(cmt_ref[...], zr_ref[...], preferred_element_type=F32)
                     + jnp.dot(nsmt_ref[...], zi_ref[...], preferred_element_type=F32))

    @pl.when(k == nk - 1)
    def _():
        conv = acc_ref[...] * inv_len
        o_ref[...] = (conv + vx_ref[...] * bias_ref[...]) * x0_ref[...]


def dft_inv(cmt, nsmt, zr, zi, vx, x0, bias, *, n_seq, seq_len, tile):
    nk = seq_len // tile
    tn = HY_CH
    per_seq = seq_len // tile
    zspec = pl.BlockSpec((None, tile, tn), lambda s, i, j, k: (s, k, j))
    espec = pl.BlockSpec((None, tile, tn), lambda s, i, j, k: (s, i, j))
    return pl.pallas_call(
        functools.partial(_dft_inv_kernel, nk=nk, inv_len=1.0 / seq_len),
        grid=(n_seq, per_seq, HY_CH // tn, nk),
        in_specs=[
            pl.BlockSpec((tile, tile), lambda s, i, j, k: (i, k)),
            pl.BlockSpec((tile, tile), lambda s, i, j, k: (i, k)),
            zspec, zspec, espec, espec,
            pl.BlockSpec((1, tn), lambda s, i, j, k: (0, j)),
        ],
        out_specs=pl.BlockSpec((tile, tn), lambda s, i, j, k: (s * per_seq + i, j)),
        out_shape=jax.ShapeDtypeStruct((n_seq * seq_len, HY_CH), F32),
        scratch_shapes=[pltpu.VMEM((tile, tn), F32)],
        compiler_params=_cparams(("parallel", "parallel", "parallel", "arbitrary")),
        name="dft_inv",
    )(cmt, nsmt, zr, zi, vx.reshape(n_seq, seq_len, HY_CH), x0.reshape(n_seq, seq_len, HY_CH), bias)


def hyena_group(u, conv_w, conv_b, taps, bias, *, n_seq, seq_len, row_off, tile):
    hsum, hdiff = taps
    cm, nsm, cmt, nsmt = _dft_tables(seq_len)
    hr = matmul(cm, hsum, tm=tile, tn=HY_CH, tk=tile)
    hi = matmul(nsm, hdiff, tm=tile, tn=HY_CH, tk=tile)
    vx, vxb, x0 = hyena_pre(u, conv_w, conv_b, n_seq=n_seq, seq_len=seq_len, row_off=row_off)
    zr, zi = dft_fwd(cm, nsm, vxb, hr, hi, n_seq=n_seq, seq_len=seq_len, tile=tile)
    return dft_inv(cmt, nsmt, zr, zi, vx, x0, bias, n_seq=n_seq, seq_len=seq_len, tile=tile)


def _s5_tables(lam_re, lam_im, log_dt, b_re, b_im, c_re, c_im, reverse):
    dt = jnp.exp(log_dt)[:, None]
    a = lam_re * dt
    b = lam_im * dt
    mag = jnp.exp(a)
    lbr, lbi = mag * jnp.cos(b), mag * jnp.sin(b)
    den = lam_re * lam_re + lam_im * lam_im
    qr = ((lbr - 1.0) * lam_re + lbi * lam_im) / den
    qi = (lbi * lam_re - (lbr - 1.0) * lam_im) / den
    bbr = qr[..., None] * b_re - qi[..., None] * b_im
    bbi = qr[..., None] * b_im + qi[..., None] * b_re
    eye = jnp.eye(S5_LANE_GROUPS, dtype=F32)

    def in_block(m):
        m = m.reshape(S5_LANE_BLOCKS, S5_LANE_GROUPS, S5_STATE, S5_GROUP)
        return jnp.einsum("qlnc,lm->qlcmn", m, eye).reshape(S5_LANE_BLOCKS, 128, S5_LANE_STATES)

    def out_block(m):
        m = m.reshape(S5_LANE_BLOCKS, S5_LANE_GROUPS, S5_GROUP, S5_STATE)
        return jnp.einsum("qlcn,lm->qlnmc", m, eye).reshape(S5_LANE_BLOCKS, S5_LANE_STATES, 128)

    b_blk = jnp.concatenate([in_block(bbr), in_block(bbi)], axis=2).astype(BF16)
    c_blk = jnp.concatenate([out_block(c_re), out_block(-c_im)], axis=1).astype(BF16)
    flat = lambda m: m.reshape(1, S5_NS)
    lam = jnp.concatenate([flat(lbr), flat(lbi)], axis=1)
    steps = jnp.arange(1, S5_SEG + 1, dtype=F32)
    if reverse:
        steps = steps[::-1]
    pa = a.reshape(1, S5_NS) * steps[:, None]
    pb = b.reshape(1, S5_NS) * steps[:, None]
    pw = jnp.concatenate([jnp.exp(pa) * jnp.cos(pb), jnp.exp(pa) * jnp.sin(pb)], axis=1)
    sa, sb = flat(a) * S5_SEG, flat(b) * S5_SEG
    lam_seg = jnp.concatenate([jnp.exp(sa) * jnp.cos(sb), jnp.exp(sa) * jnp.sin(sb)], axis=1)
    return b_blk, c_blk, lam, lam_seg, pw


N_S5_CTX_STEPS = T_CTX // S5_CHUNK
N_S5_LAT_CHUNKS = LAT_LEN // S5_CHUNK
N_S5_STEPS = T_ALL // S5_CHUNK


def _s5_step_info(i, reverse):
    k = jnp.maximum(i - N_S5_CTX_STEPS, 0)
    b, j = k // N_S5_LAT_CHUNKS, k % N_S5_LAT_CHUNKS
    chunk = (N_S5_LAT_CHUNKS - 1 - j) if reverse else j
    is_ctx = i < N_S5_CTX_STEPS
    row_blk = jnp.where(is_ctx, i, N_S5_CTX_STEPS + b * N_S5_LAT_CHUNKS + chunk)
    seq = jnp.where(is_ctx, i, N_CTX_SEQ + b)
    return row_blk, seq, is_ctx | (j == 0), is_ctx | (j == N_S5_LAT_CHUNKS - 1)


def _s5_kernel(u_ref, perm_ref, permt_ref, bblk_ref, cblk_ref, lam_ref, lseg_ref, pw_ref, h0_ref, y_ref, fin_ref,
               bu_ref, hend_ref, cin_ref, carry_ref, *, reverse):
    _, _, is_first, is_last = _s5_step_info(pl.program_id(0), reverse)
    ns = S5_NS

    @pl.when(is_first)
    def _():
        carry_ref[...] = h0_ref[...]

    u_seg = jnp.dot(perm_ref[...], u_ref[...].astype(BF16), preferred_element_type=F32).astype(BF16)
    st_blk = S5_LANE_STATES
    for q in range(S5_LANE_BLOCKS):
        u_q = u_seg[:, q * 128:(q + 1) * 128]
        for part in range(2):
            bu_ref[:, pl.ds(part * ns + q * st_blk, st_blk)] = jnp.dot(
                u_q, bblk_ref[q, :, part * st_blk:(part + 1) * st_blk], preferred_element_type=F32)

    for cb in range(ns // S5_COLS):
        re_cols = pl.ds(cb * S5_COLS, S5_COLS)
        im_cols = pl.ds(ns + cb * S5_COLS, S5_COLS)
        lr = jnp.broadcast_to(lam_ref[:, re_cols], (8, S5_COLS))
        li = jnp.broadcast_to(lam_ref[:, im_cols], (8, S5_COLS))

        def step(kk, carry, re_cols=re_cols, im_cols=im_cols, lr=lr, li=li):
            hr, hi = carry
            k = (S5_SEG - 1 - kk) if reverse else kk
            rows = pl.ds(pl.multiple_of(k * 8, 8), 8)
            nr = lr * hr - li * hi + bu_ref[rows, re_cols]
            ni = lr * hi + li * hr + bu_ref[rows, im_cols]
            bu_ref[rows, re_cols] = nr
            bu_ref[rows, im_cols] = ni
            return nr, ni

        zero = jnp.zeros((8, S5_COLS), F32)
        hr, hi = lax.fori_loop(0, S5_SEG, step, (zero, zero), unroll=4)
        hend_ref[:, re_cols] = hr
        hend_ref[:, im_cols] = hi

    cr = carry_ref[:, :ns]
    ci = carry_ref[:, ns:]
    lsr = lseg_ref[:, :ns]
    lsi = lseg_ref[:, ns:]
    for p in range(8):
        s = 7 - p if reverse else p
        cin_ref[s:s + 1, :ns] = cr
        cin_ref[s:s + 1, ns:] = ci
        er = hend_ref[s:s + 1, :ns]
        ei = hend_ref[s:s + 1, ns:]
        cr, ci = er + lsr * cr - lsi * ci, ei + lsr * ci + lsi * cr
    carry_ref[:, :ns] = cr
    carry_ref[:, ns:] = ci

    for cb in range(ns // S5_COLS):
        re_cols = pl.ds(cb * S5_COLS, S5_COLS)
        im_cols = pl.ds(ns + cb * S5_COLS, S5_COLS)
        cinr = cin_ref[:, re_cols]
        cini = cin_ref[:, im_cols]

        def fix(k, _, re_cols=re_cols, im_cols=im_cols, cinr=cinr, cini=cini):
            rows = pl.ds(pl.multiple_of(k * 8, 8), 8)
            pr = pw_ref[pl.ds(k, 1), re_cols]
            pi = pw_ref[pl.ds(k, 1), im_cols]
            bu_ref[rows, re_cols] += pr * cinr - pi * cini
            bu_ref[rows, im_cols] += pr * cini + pi * cinr
            return 0

        lax.fori_loop(0, S5_SEG, fix, 0, unroll=4)

    y_parts = []
    for q in range(S5_LANE_BLOCKS):
        acc = None
        for part in range(2):
            h_q = bu_ref[:, pl.ds(part * ns + q * st_blk, st_blk)].astype(BF16)
            d = jnp.dot(h_q, cblk_ref[q, part * st_blk:(part + 1) * st_blk, :], preferred_element_type=F32)
            acc = d if acc is None else acc + d
        y_parts.append(acc)
    y_seg = jnp.concatenate(y_parts, axis=1)
    y_hi = y_seg.astype(BF16)
    y_lo = (y_seg - y_hi.astype(F32)).astype(BF16)
    y_ref[...] = (jnp.dot(permt_ref[...], y_hi, preferred_element_type=F32)
                  + jnp.dot(permt_ref[...], y_lo, preferred_element_type=F32))

    @pl.when(is_last)
    def _():
        fin_ref[...] = carry_ref[...]


def s5_direction(u, tables, h0, *, reverse):
    b_blk, c_blk, lam, lam_seg, pw = tables
    s5_col = 3 * HY_CH // S5_CH
    n_seq = N_CTX_SEQ + N_LAT_SEQ
    row_blk = lambda i: _s5_step_info(i, reverse)[0]
    seq = lambda i: _s5_step_info(i, reverse)[1]
    const = lambda shp: pl.BlockSpec(shp, lambda i: (0, 0))
    t = np.arange(S5_CHUNK)
    perm_np = np.zeros((S5_CHUNK, S5_CHUNK), np.float32)
    perm_np[(t % S5_SEG) * 8 + t // S5_SEG, t] = 1.0
    perm = jnp.asarray(perm_np, dtype=BF16)
    perm_t = jnp.asarray(perm_np.T, dtype=BF16)
    return pl.pallas_call(
        functools.partial(_s5_kernel, reverse=reverse),
        grid=(N_S5_STEPS,),
        in_specs=[
            pl.BlockSpec((S5_CHUNK, S5_CH), lambda i: (row_blk(i), s5_col)),
            const((S5_CHUNK, S5_CHUNK)), const((S5_CHUNK, S5_CHUNK)),
            pl.BlockSpec((S5_LANE_BLOCKS, 128, 2 * S5_LANE_STATES), lambda i: (0, 0, 0)),
            pl.BlockSpec((S5_LANE_BLOCKS, 2 * S5_LANE_STATES, 128), lambda i: (0, 0, 0)),
            const((1, 2 * S5_NS)), const((1, 2 * S5_NS)),
            const((S5_SEG, 2 * S5_NS)),
            pl.BlockSpec((None, 1, 2 * S5_NS), lambda i: (seq(i), 0, 0)),
        ],
        out_specs=[pl.BlockSpec((S5_CHUNK, S5_CH), lambda i: (row_blk(i), 0)),
                   pl.BlockSpec((None, 1, 2 * S5_NS), lambda i: (seq(i), 0, 0))],
        out_shape=[jax.ShapeDtypeStruct((T_ALL, S5_CH), F32),
                   jax.ShapeDtypeStruct((n_seq, 1, 2 * S5_NS), F32)],
        scratch_shapes=[pltpu.VMEM((S5_CHUNK, 2 * S5_NS), F32), pltpu.VMEM((8, 2 * S5_NS), F32),
                        pltpu.VMEM((8, 2 * S5_NS), F32), pltpu.VMEM((1, 2 * S5_NS), F32)],
        compiler_params=_cparams(("arbitrary",)),
        name="s5_bwd" if reverse else "s5_fwd",
    )(u, perm, perm_t, b_blk, c_blk, lam, lam_seg, pw, h0)


def _s5_post_kernel(u_ref, yf_ref, yb_ref, d_ref, w_ref, b_ref, o_ref):
    y = d_ref[...] * u_ref[...] + yf_ref[...] + yb_ref[...]
    cdf = 0.5 * (1.0 + jnp.tanh(math.sqrt(2.0 / math.pi) * (y + 0.044715 * (y * y * y))))
    g = y * cdf
    z = jnp.dot(g.astype(BF16), w_ref[...], preferred_element_type=F32) + b_ref[...]
    o_ref[...] = z[:, :S5_CH] * jax.nn.sigmoid(z[:, S5_CH:])


def s5_post(u, yf, yb, d, glu_w, glu_b):
    row = pl.BlockSpec((ROW_TILE, S5_CH), lambda i: (i, 0))
    return pl.pallas_call(
        _s5_post_kernel,
        grid=(T_ALL // ROW_TILE,),
        in_specs=[pl.BlockSpec((ROW_TILE, S5_CH), lambda i: (i, 3 * HY_CH // S5_CH)), row, row,
                  pl.BlockSpec((1, S5_CH), lambda i: (0, 0)),
                  pl.BlockSpec((S5_CH, 2 * S5_CH), lambda i: (0, 0)),
                  pl.BlockSpec((1, 2 * S5_CH), lambda i: (0, 0))],
        out_specs=row,
        out_shape=jax.ShapeDtypeStruct((T_ALL, S5_CH), F32),
        compiler_params=_cparams(("parallel",)),
        name="s5_post",
    )(u, yf, yb, d, glu_w, glu_b)


def _head_mask(shape):
    return lax.broadcasted_iota(jnp.int32, shape, 1) < HEAD_DIM


def _ctx_attn_kernel(q_ref, k_ref, v_ref, o_ref, kcache_ref, vcache_ref):
    kcache_ref[...] = k_ref[...].reshape(CTX_LEN, N_HEADS, HEAD_DIM)
    vcache_ref[...] = v_ref[...].reshape(CTX_LEN, N_HEADS, HEAD_DIM)
    low = _head_mask((CTX_LEN, 128))
    for hp in range(N_HEADS // 2):
        cols = pl.ds(hp * 128, 128)
        a = _stack_heads(q_ref[:, cols] * (HEAD_DIM ** -0.5), low)
        k = k_ref[:, cols].astype(BF16)
        v = v_ref[:, cols].astype(BF16)
        s = lax.dot_general(a, k, (((1,), (1,)), ((), ())), preferred_element_type=F32)
        m = jnp.max(s, axis=-1, keepdims=True)
        p = jnp.exp(s - m)
        den = jnp.sum(p, axis=-1, keepdims=True)
        o = jnp.dot(p.astype(BF16), v, preferred_element_type=F32) / den
        o_ref[:, cols] = jnp.where(low, o[:CTX_LEN], o[CTX_LEN:])


def _stack_heads(q, low):
    return jnp.concatenate([jnp.where(low, q, 0.0), jnp.where(low, 0.0, q)], axis=0).astype(BF16)


def ctx_attention(qkv):
    blk = lambda part: pl.BlockSpec((CTX_LEN, D), lambda b: (b, part))
    cache_shape = jax.ShapeDtypeStruct((N_CTX_SEQ, 1, CTX_LEN, N_HEADS, HEAD_DIM), F32)
    cache_spec = pl.BlockSpec((None, None, CTX_LEN, N_HEADS, HEAD_DIM), lambda b: (b, 0, 0, 0, 0))
    return pl.pallas_call(
        _ctx_attn_kernel,
        grid=(N_CTX_SEQ,),
        in_specs=[blk(0), blk(1), blk(2)],
        out_specs=[pl.BlockSpec((CTX_LEN, D), lambda b: (b, 0)), cache_spec, cache_spec],
        out_shape=[jax.ShapeDtypeStruct((T_CTX, D), F32), cache_shape, cache_shape],
        compiler_params=_cparams(("parallel",)),
        name="ctx_attention",
    )(qkv, qkv, qkv)


NA_ROWS_PER_STEP = 8


def _na_kernel(q_ref, k_ref, v_ref, kc_ref, vc_ref, bias_ref, o_ref):
    rb = pl.program_id(2)
    kc = kc_ref[...]
    vc = vc_ref[...]
    low = _head_mask((GRID_W, 128))
    n_rows = LAT_LEN // GRID_W
    for i in range(NA_ROWS_PER_STEP):
        r = rb * NA_ROWS_PER_STEP + i
        rs = jnp.clip(r - WIN_R // 2, 0, n_rows - WIN_R)
        dlt = r - rs
        rows = pl.ds(pl.multiple_of(rs * GRID_W, GRID_W), WIN_R * GRID_W)
        kw = k_ref[rows, :].astype(BF16)
        vw = v_ref[rows, :].astype(BF16)
        a = _stack_heads(q_ref[i * GRID_W:(i + 1) * GRID_W, :] * (HEAD_DIM ** -0.5), low)
        s = lax.dot_general(a, kw, (((1,), (1,)), ((), ())), preferred_element_type=F32)
        s = s + bias_ref[dlt].reshape(2 * GRID_W, WIN_R * GRID_W)
        sc = lax.dot_general(a, kc, (((1,), (1,)), ((), ())), preferred_element_type=F32)
        m = jnp.maximum(jnp.max(s, axis=-1, keepdims=True), jnp.max(sc, axis=-1, keepdims=True))
        p = jnp.exp(s - m)
        pc = jnp.exp(sc - m)
        den = jnp.sum(p, axis=-1, keepdims=True) + jnp.sum(pc, axis=-1, keepdims=True)
        o = (jnp.dot(p.astype(BF16), vw, preferred_element_type=F32)
             + jnp.dot(pc.astype(BF16), vc, preferred_element_type=F32)) / den
        o_ref[i * GRID_W:(i + 1) * GRID_W, :] = jnp.where(low, o[:GRID_W], o[GRID_W:])


def _na_bias_table(rpb):
    j = np.arange(GRID_W)
    cs = np.clip(j - WIN_C // 2, 0, GRID_W - WIN_C)
    c = np.arange(GRID_W)
    inside = (c[None, :] >= cs[:, None]) & (c[None, :] < cs[:, None] + WIN_C)
    col_rel = c[None, :] - j[:, None] + WIN_C - 1
    n_rel_c = 2 * WIN_C - 1
    onehot = ((np.arange(n_rel_c)[:, None, None] == col_rel[None]) & inside[None]).astype(np.float32)
    t = jnp.dot(rpb.reshape(N_HEADS * (2 * WIN_R - 1), n_rel_c), jnp.asarray(onehot.reshape(n_rel_c, -1)),
                precision=HIGHEST)
    t = t.reshape(N_HEADS, 2 * WIN_R - 1, GRID_W, GRID_W)
    t = t + jnp.asarray(np.where(inside, 0.0, NEG_BIG).astype(np.float32))
    t = jnp.stack([t[:, WIN_R - 1 - dl:2 * WIN_R - 1 - dl] for dl in range(WIN_R)], axis=0)
    t = t.transpose(0, 1, 3, 2, 4)
    return t.reshape(WIN_R, N_HEADS, GRID_W, WIN_R * GRID_W)


def na_attention(qkv, kcb, vcb, bias):
    hp = N_HEADS // 2
    q_rows = NA_ROWS_PER_STEP * GRID_W
    steps = LAT_LEN // q_rows
    q_off = T_CTX // q_rows
    seq_off = T_CTX // LAT_LEN
    return pl.pallas_call(
        _na_kernel,
        grid=(N_LAT_SEQ, hp, steps),
        in_specs=[
            pl.BlockSpec((q_rows, 128), lambda b, h, r: (q_off + b * steps + r, h)),
            pl.BlockSpec((LAT_LEN, 128), lambda b, h, r: (seq_off + b, hp + h)),
            pl.BlockSpec((LAT_LEN, 128), lambda b, h, r: (seq_off + b, 2 * hp + h)),
            pl.BlockSpec((CTX_LEN, 128), lambda b, h, r: (b, h)),
            pl.BlockSpec((CTX_LEN, 128), lambda b, h, r: (b, h)),
            pl.BlockSpec((WIN_R, 2, GRID_W, WIN_R * GRID_W), lambda b, h, r: (0, h, 0, 0)),
        ],
        out_specs=pl.BlockSpec((q_rows, 128), lambda b, h, r: (b * steps + r, h)),
        out_shape=jax.ShapeDtypeStruct((T_LAT, D), F32),
        compiler_params=_cparams(("parallel", "parallel", "arbitrary")),
        name="na_attention",
    )(qkv, qkv, qkv, kcb, vcb, bias)


def _layer_norm(r, g, b):
    mu = jnp.mean(r, axis=-1, keepdims=True)
    c = r - mu
    var = jnp.mean(c * c, axis=-1, keepdims=True)
    return c * lax.rsqrt(var + LN_EPS) * g + b


def _proj_kernel(*refs, n_in):
    a_refs = refs[:2 * n_in]
    w_refs = refs[2 * n_in:3 * n_in]
    (x_ref, mod_ref, g_ref, b_ref, rwh_ref, rwl_ref, rb_ref, tri_ref,
     x1_ref, h2_ref, exp_ref, pos_ref, gate_ref, cnt_ref, run_ref) = refs[3 * n_in:]
    is_ctx = pl.program_id(0) < T_CTX // ROW_TILE

    @pl.when(pl.program_id(0) == 0)
    def _():
        run_ref[...] = jnp.zeros_like(run_ref)

    y = None
    for n, w_ref in enumerate(w_refs):
        a = jnp.where(is_ctx, a_refs[2 * n][...], a_refs[2 * n + 1][...])
        part = jnp.dot(a.astype(BF16), w_ref[...], preferred_element_type=F32)
        y = part if y is None else y + part
    m = mod_ref[...]
    x1 = _layer_norm(DN_ALPHA * x_ref[...] + m[2:3] * y, g_ref[...], b_ref[...])
    x1_ref[...] = x1
    h2 = x1 * (1.0 + m[4:5]) + m[3:4]
    h2_ref[...] = h2.reshape(ROW_TILE, *ROW_AS_TILE)

    hh = h2.astype(BF16)
    hl = (h2 - hh.astype(F32)).astype(BF16)
    logits = (jnp.dot(hh, rwh_ref[...], preferred_element_type=F32)
              + jnp.dot(hh, rwl_ref[...], preferred_element_type=F32)
              + jnp.dot(hl, rwh_ref[...], preferred_element_type=F32)) + rb_ref[...]

    lane = lax.broadcasted_iota(jnp.int32, logits.shape, 1)
    work = logits
    vals, sels = [], []
    for k in range(TOP_K):
        mx = jnp.max(work, axis=-1, keepdims=True)
        idx = jnp.min(jnp.where(work == mx, lane, 128), axis=-1, keepdims=True)
        sel = lane == idx
        work = jnp.where(sel, -jnp.inf, work)
        exp_ref[:, k:k + 1] = idx
        vals.append(mx)
        sels.append(sel)
    exps = [jnp.exp(v - vals[0]) for v in vals]
    den = exps[0] + exps[1] + exps[2] + exps[3]
    for k in range(TOP_K):
        gate_ref[:, k:k + 1] = exps[k] / den

    picked = sels[0] | sels[1] | sels[2] | sels[3]
    onehot = jnp.where(picked, 1.0, 0.0)
    incl = jnp.dot(tri_ref[...], onehot.astype(BF16), preferred_element_type=F32)
    before = incl - onehot + run_ref[...]
    for k in range(TOP_K):
        pos = jnp.sum(jnp.where(sels[k], before, 0.0), axis=-1, keepdims=True)
        pos_ref[:, k:k + 1] = pos.astype(jnp.int32)
    run_ref[...] += jnp.sum(onehot, axis=0, keepdims=True)
    cnt_ref[...] = run_ref[...]


def proj_res_ln(acts, weights, x, mod, ln_g, ln_b, rw_hi, rw_lo, rb):
    n_in = len(acts)
    n_ctx = T_CTX // ROW_TILE
    row = lambda c: pl.BlockSpec((ROW_TILE, c), lambda i: (i, 0))
    full = lambda shp: pl.BlockSpec(shp, lambda i: (0, 0))
    tri = jnp.asarray(np.tril(np.ones((ROW_TILE, ROW_TILE), np.float32)), dtype=BF16)
    act_arrays, act_specs = [], []
    for a_ctx, a_lat in acts:
        lat_off = n_ctx if a_lat.shape[0] == T_ALL else 0
        act_arrays += [a_ctx, a_lat]
        act_specs += [pl.BlockSpec((ROW_TILE, a_ctx.shape[1]), lambda i: (jnp.minimum(i, n_ctx - 1), 0)),
                      pl.BlockSpec((ROW_TILE, a_lat.shape[1]),
                                   lambda i, lat_off=lat_off: (jnp.maximum(i - n_ctx, 0) + lat_off, 0))]
    in_specs = (act_specs + [full(w.shape) for w in weights]
                + [row(D), pl.BlockSpec((None, 8, D), lambda i: (_mod_index(i, ROW_TILE), 0, 0)),
                   full((1, D)), full((1, D)), full((D, 128)), full((D, 128)), full((1, 128)),
                   full((ROW_TILE, ROW_TILE))])
    return pl.pallas_call(
        functools.partial(_proj_kernel, n_in=n_in),
        grid=(T_ALL // ROW_TILE,),
        in_specs=in_specs,
        out_specs=[row(D), pl.BlockSpec((ROW_TILE, *ROW_AS_TILE), lambda i: (i, 0, 0)),
                   row(TOP_K), row(TOP_K), row(TOP_K), full((1, 128))],
        out_shape=[jax.ShapeDtypeStruct((T_ALL, D), F32), jax.ShapeDtypeStruct((T_ALL, *ROW_AS_TILE), F32),
                   jax.ShapeDtypeStruct((T_ALL, TOP_K), jnp.int32), jax.ShapeDtypeStruct((T_ALL, TOP_K), jnp.int32),
                   jax.ShapeDtypeStruct((T_ALL, TOP_K), F32), jax.ShapeDtypeStruct((1, 128), F32)],
        scratch_shapes=[pltpu.VMEM((1, 128), F32)],
        compiler_params=_cparams(("arbitrary",)),
        name="proj_res_ln",
    )(*act_arrays, *weights, x, mod, ln_g, ln_b, rw_hi, rw_lo, rb, tri)


def _dispatch_kernel(dest_ref, h_ref, xs_ref, sem):
    def copy(r, d):
        return pltpu.make_async_copy(h_ref.at[r], xs_ref.at[d], sem)

    def issue(r, _):
        for k in range(TOP_K):
            copy(r, dest_ref[r * TOP_K + k]).start(priority=k % 2)
        return 0

    lax.fori_loop(0, ROW_TILE, issue, 0, unroll=4)

    def drain(r, _):
        for k in range(TOP_K):
            copy(r, dest_ref[r * TOP_K + k]).wait()
        return 0

    lax.fori_loop(0, ROW_TILE, drain, 0, unroll=4)


def moe_dispatch(h2, dest_flat):
    return pl.pallas_call(
        _dispatch_kernel,
        grid=(T_ALL // ROW_TILE,),
        in_specs=[pl.BlockSpec((ROW_TILE * TOP_K,), lambda i: (i,), memory_space=pltpu.SMEM),
                  pl.BlockSpec((ROW_TILE, *ROW_AS_TILE), lambda i: (i, 0, 0))],
        out_specs=pl.BlockSpec(memory_space=pl.ANY),
        out_shape=jax.ShapeDtypeStruct((N_ASSIGN, *ROW_AS_TILE), F32),
        scratch_shapes=[pltpu.SemaphoreType.DMA],
        compiler_params=_cparams(("arbitrary",)),
        name="moe_dispatch",
    )(dest_flat, h2)


W_CAST_ROWS = 128


def _moe_ffn_kernel(blk_ref, exp_ref, lo_ref, hi_ref, first_ref, newexp_ref, nxt_ref,
                    x_ref, w1_ref, b1_ref, w2_ref, b2_ref, o_ref, w1f_ref, w2f_ref, w1b_ref, w2b_ref, sems, *, layer):
    i = pl.program_id(0)
    lo = lo_ref[i]
    hi = hi_ref[i]

    def weights(e):
        return (pltpu.make_async_copy(w1_ref.at[layer, e], w1f_ref, sems.at[0]),
                pltpu.make_async_copy(w2_ref.at[layer, e], w2f_ref, sems.at[1]))

    @pl.when(i == 0)
    def _():
        for c in weights(exp_ref[0]):
            c.start()

    @pl.when(newexp_ref[i] == 1)
    def _():
        for c in weights(exp_ref[i]):
            c.wait()

        def cast(c, _):
            rows = pl.ds(pl.multiple_of(c * W_CAST_ROWS, W_CAST_ROWS), W_CAST_ROWS)
            w1b_ref[rows, :] = w1f_ref[rows, :].astype(BF16)
            w2b_ref[rows, :] = w2f_ref[rows, :].astype(BF16)
            return 0

        lax.fori_loop(0, D // W_CAST_ROWS, cast, 0)

        @pl.when(nxt_ref[i] >= 0)
        def _():
            for c in weights(nxt_ref[i]):
                c.start()

    @pl.when(first_ref[i] == 1)
    def _():
        o_ref[...] = jnp.zeros_like(o_ref)

    @pl.when(hi > lo)
    def _():
        x = x_ref[...].reshape(MOE_TILE, D).astype(BF16)
        h = jnp.dot(x, w1b_ref[...], preferred_element_type=F32) + b1_ref[...]
        g = jnp.minimum(h[:, :D_FF], SWIGLU_LIMIT)
        u = jnp.clip(h[:, D_FF:], -SWIGLU_LIMIT, SWIGLU_LIMIT)
        a = g * jax.nn.sigmoid(SWIGLU_ALPHA * g) * (u + 1.0)
        y = jnp.dot(a.astype(BF16), w2b_ref[...], preferred_element_type=F32) + b2_ref[...]
        row = lax.broadcasted_iota(jnp.int32, (MOE_TILE, 1), 0)
        o_ref[...] += jnp.where((row >= lo) & (row < hi), y, 0.0).reshape(MOE_TILE, *ROW_AS_TILE)


def moe_ffn(items, xs, w1, b1, w2, b2, layer):
    assert D == D_FF
    blk, exp, lo, hi, first, newexp, nxt = items
    grid_spec = pltpu.PrefetchScalarGridSpec(
        num_scalar_prefetch=7,
        grid=(N_MOE_ITEMS,),
        in_specs=[
            pl.BlockSpec((MOE_TILE, *ROW_AS_TILE), lambda i, blk, exp, *_: (blk[i], 0, 0)),
            pl.BlockSpec(memory_space=pl.ANY),
            pl.BlockSpec((None, None, 1, 2 * D_FF), lambda i, blk, exp, *_: (layer, exp[i], 0, 0)),
            pl.BlockSpec(memory_space=pl.ANY),
            pl.BlockSpec((None, None, 1, D), lambda i, blk, exp, *_: (layer, exp[i], 0, 0)),
        ],
        out_specs=pl.BlockSpec((MOE_TILE, *ROW_AS_TILE), lambda i, blk, exp, *_: (blk[i], 0, 0)),
        scratch_shapes=[pltpu.VMEM((D, 2 * D_FF), F32), pltpu.VMEM((D_FF, D), F32),
                        pltpu.VMEM((D, 2 * D_FF), BF16), pltpu.VMEM((D_FF, D), BF16),
                        pltpu.SemaphoreType.DMA((2,))],
    )
    return pl.pallas_call(
        functools.partial(_moe_ffn_kernel, layer=layer),
        grid_spec=grid_spec,
        out_shape=jax.ShapeDtypeStruct((N_ASSIGN, *ROW_AS_TILE), F32),
        compiler_params=_cparams(("arbitrary",)),
        name="moe_ffn",
    )(blk, exp, lo, hi, first, newexp, nxt, xs, w1, b1, w2, b2)


def _combine_kernel(dest_ref, dest_next_ref, ys_ref, gates_ref, x_ref, mod_ref, g_ref, b_ref, o_ref, buf_ref, sems,
                    *, n_tiles):
    i = pl.program_id(0)
    slot = i % 2

    def copy(idx_ref, s, r, k):
        d = idx_ref[r * TOP_K + k]
        return pltpu.make_async_copy(ys_ref.at[d], buf_ref.at[s, k, r], sems.at[s])

    def gather(idx_ref, s):
        def issue(r, _):
            for k in range(TOP_K):
                copy(idx_ref, s, r, k).start(priority=k % 2)
            return 0

        lax.fori_loop(0, COMB_TILE, issue, 0, unroll=4)

    @pl.when(i == 0)
    def _():
        gather(dest_ref, 0)

    @pl.when(i + 1 < n_tiles)
    def _():
        gather(dest_next_ref, 1 - slot)

    def drain(r, _):
        for k in range(TOP_K):
            copy(dest_ref, slot, r, k).wait()
        return 0

    lax.fori_loop(0, COMB_TILE, drain, 0, unroll=4)

    gates = gates_ref[...]
    y = gates[:, 0:1] * buf_ref[slot, 0].reshape(COMB_TILE, D)
    for k in range(1, TOP_K):
        y = y + gates[:, k:k + 1] * buf_ref[slot, k].reshape(COMB_TILE, D)
    m = mod_ref[...]
    o_ref[...] = _layer_norm(DN_ALPHA * x_ref[...] + m[5:6] * y, g_ref[...], b_ref[...])


def moe_combine(ys, dest_flat, gates, x1, mod, ln_g, ln_b, *, row_off=0, n_rows=T_ALL):
    full = lambda shp: pl.BlockSpec(shp, lambda i: (0, 0))
    idx = lambda fn: pl.BlockSpec((COMB_TILE * TOP_K,), fn, memory_space=pltpu.SMEM)
    t0 = row_off // COMB_TILE
    n_tiles = n_rows // COMB_TILE
    return pl.pallas_call(
        functools.partial(_combine_kernel, n_tiles=n_tiles),
        grid=(n_tiles,),
        in_specs=[idx(lambda i: (t0 + i,)), idx(lambda i: (t0 + jnp.minimum(i + 1, n_tiles - 1),)),
                  pl.BlockSpec(memory_space=pl.ANY),
                  pl.BlockSpec((COMB_TILE, TOP_K), lambda i: (t0 + i, 0)),
                  pl.BlockSpec((COMB_TILE, D), lambda i: (t0 + i, 0)),
                  pl.BlockSpec((None, 8, D), lambda i: (_mod_index(t0 + i, COMB_TILE), 0, 0)),
                  full((1, D)), full((1, D))],
        out_specs=pl.BlockSpec((COMB_TILE, D), lambda i: (i, 0)),
        out_shape=jax.ShapeDtypeStruct((n_rows, D), F32),
        scratch_shapes=[pltpu.VMEM((2, TOP_K, COMB_TILE, *ROW_AS_TILE), F32), pltpu.SemaphoreType.DMA((2,))],
        compiler_params=_cparams(("arbitrary",)),
        name="moe_combine",
    )(dest_flat, dest_flat, ys, gates, x1, mod, ln_g, ln_b)


def _routing_tables(experts, pos, counts):
    counts = counts[0, :N_EXPERTS].astype(jnp.int32)
    starts = jnp.cumsum(counts) - counts
    eids = jnp.arange(N_EXPERTS, dtype=jnp.int32)
    start_of_pick = jnp.sum(jnp.where(experts[..., None] == eids, starts, 0), axis=-1)
    dest = start_of_pick + pos
    bnd = jnp.sort(jnp.concatenate([jnp.arange(N_MOE_BLOCKS + 1, dtype=jnp.int32) * MOE_TILE, starts[1:]]))
    a, b = bnd[:-1], bnd[1:]
    blk = jnp.minimum(a // MOE_TILE, N_MOE_BLOCKS - 1)
    lo = a - blk * MOE_TILE
    hi = b - blk * MOE_TILE
    exp = jnp.clip(jnp.sum((starts[None, :] <= a[:, None]).astype(jnp.int32), axis=1) - 1, 0, N_EXPERTS - 1)
    one = jnp.ones((1,), jnp.int32)
    first = jnp.concatenate([one, (blk[1:] != blk[:-1]).astype(jnp.int32)])
    newexp = jnp.concatenate([one, (exp[1:] != exp[:-1]).astype(jnp.int32)])
    idx = jnp.arange(N_MOE_ITEMS, dtype=jnp.int32)
    change_at = jnp.where(newexp == 1, idx, N_MOE_ITEMS)
    next_change = jnp.concatenate([lax.cummin(change_at, reverse=True)[1:], jnp.full((1,), N_MOE_ITEMS, jnp.int32)])
    nxt = jnp.where(next_change < N_MOE_ITEMS, exp[jnp.minimum(next_change, N_MOE_ITEMS - 1)], -1)
    items = tuple(v.astype(jnp.int32) for v in (blk, exp, lo, hi, first, newexp, nxt))
    return dest.astype(jnp.int32).reshape(-1), items


def moe_layer(h2, experts, pos, gates, counts, x1, mod, ln_g, ln_b, w1, b1, w2, b2, layer, split_groups):
    dest_flat, items = _routing_tables(experts, pos, counts)
    xs = moe_dispatch(h2, dest_flat)
    ys = moe_ffn(items, xs, w1, b1, w2, b2, layer)
    if not split_groups:
        return moe_combine(ys, dest_flat, gates, x1, mod, ln_g, ln_b)
    return (moe_combine(ys, dest_flat, gates, x1, mod, ln_g, ln_b, row_off=0, n_rows=T_CTX),
            moe_combine(ys, dest_flat, gates, x1, mod, ln_g, ln_b, row_off=T_CTX, n_rows=T_LAT))


def _router_operands(router_w, router_b):
    w = jnp.pad(router_w, ((0, 0), (0, 128 - N_EXPERTS)))
    hi = w.astype(BF16)
    lo = (w - hi.astype(F32)).astype(BF16)
    b = jnp.concatenate([router_b, jnp.full((128 - N_EXPERTS,), -jnp.inf, F32)])[None]
    return hi, lo, b


def kernel(x_prompt, x_sample, state_s5, cache_na_k, cache_na_v, c, c_ctx, w_mod, b_mod, ln1_g, ln1_b, ln2_g, ln2_b, ev_w_in, ev_w_out, hy_conv_w, hy_conv_b, hy_f_w1, hy_f_b1, hy_f_w2, hy_f_b2, hy_f_w3, hy_f_freq, hy_bias, s5_lam_re, s5_lam_im, s5_log_dt, s5_b_re, s5_b_im, s5_c_re, s5_c_im, s5_d, s5_glu_w, s5_glu_b, od_w_in, od_w_out, na_rpb, router_w, router_b, moe_w1, moe_b1, moe_w2, moe_b2):
    x = jnp.concatenate([x_prompt.reshape(T_CTX, D), x_sample.reshape(T_LAT, D)], axis=0)
    cvec = jnp.pad(jnp.concatenate([c_ctx[None], c], axis=0), ((0, 5), (0, 0)))

    new_state = None
    new_k = new_v = None
    for l in range(DEPTH):
        i = l // 2
        mod = matmul(cvec, w_mod, b_mod[l][None], b_layer=l, tm=8, tn=512, tk=D, precise=True, silu_a=True)
        mod = jnp.pad(mod.reshape(8, 6, D)[:3], ((0, 0), (0, 2), (0, 0)))

        if l % 2 == 0:
            u = modlinear(x, mod, ev_w_in[i].astype(BF16))
            y_hy = []
            for n_seq, seq_len, row_off, tile in ((N_CTX_SEQ, CTX_LEN, 0, CTX_LEN), (N_LAT_SEQ, LAT_LEN, T_CTX, 1024)):
                taps = hyena_filter_taps(seq_len, hy_f_w1[i], hy_f_b1[i], hy_f_w2[i], hy_f_b2[i], hy_f_w3[i],
                                         hy_f_freq[i])
                y_hy.append(hyena_group(u, hy_conv_w[i], hy_conv_b[i][None], taps, hy_bias[i][None],
                                        n_seq=n_seq, seq_len=seq_len, row_off=row_off, tile=tile))

            h0_lat = state_s5[:, i]
            yf_parts, yb_parts, finals = [], [], []
            for r in range(2):
                tables = _s5_tables(s5_lam_re[i, r], s5_lam_im[i, r], s5_log_dt[i, r], s5_b_re[i, r], s5_b_im[i, r],
                                    s5_c_re[i, r], s5_c_im[i, r], reverse=(r == 1))
                h0_l = jnp.concatenate([h0_lat[:, r, :, :, 0].reshape(N_LAT_SEQ, 1, S5_NS),
                                        h0_lat[:, r, :, :, 1].reshape(N_LAT_SEQ, 1, S5_NS)], axis=-1)
                h0 = jnp.concatenate([jnp.zeros((N_CTX_SEQ, 1, 2 * S5_NS), F32), h0_l], axis=0)
                y_dir, fin_all = s5_direction(u, tables, h0, reverse=(r == 1))
                (yf_parts if r == 0 else yb_parts).append(y_dir)
                fin = fin_all[:N_CTX_SEQ].reshape(N_CTX_SEQ, 2, S5_GROUPS, S5_STATE)
                finals.append(jnp.stack([fin[:, 0], fin[:, 1]], axis=-1))
            new_state = jnp.stack(finals, axis=1)[:, None]
            y_s5 = s5_post(u, yf_parts[0], yb_parts[0], s5_d[i].reshape(1, S5_CH), s5_glu_w[i].astype(BF16),
                           s5_glu_b[i][None])
            acts = [tuple(y_hy), (y_s5, y_s5)]
            w_out = ev_w_out[i].astype(BF16)
            weights = [w_out[:HY_CH], w_out[HY_CH:]]
        else:
            qkv = modlinear(x, mod, od_w_in[i].astype(BF16))
            attn_ctx, new_k, new_v = ctx_attention(qkv)
            kcb = cache_na_k[:, i].reshape(N_LAT_SEQ * CTX_LEN, D).astype(BF16)
            vcb = cache_na_v[:, i].reshape(N_LAT_SEQ * CTX_LEN, D).astype(BF16)
            attn_lat = na_attention(qkv, kcb, vcb, _na_bias_table(na_rpb[i]))
            acts = [(attn_ctx, attn_lat)]
            weights = [od_w_out[i].astype(BF16)]

        rw_hi, rw_lo, rb = _router_operands(router_w[l], router_b[l])
        x1, h2, experts, pos, gates, counts = proj_res_ln(acts, weights, x, mod, ln1_g[l][None], ln1_b[l][None],
                                                          rw_hi, rw_lo, rb)
        x = moe_layer(h2, experts, pos, gates, counts, x1, mod, ln2_g[l][None], ln2_b[l][None],
                      moe_w1, moe_b1[:, :, None, :], moe_w2, moe_b2[:, :, None, :], l, split_groups=(l == DEPTH - 1))

    y_ctx, y_lat = x
    return (y_ctx.reshape(N_CTX_SEQ, CTX_LEN, D), y_lat.reshape(N_LAT_SEQ, LAT_LEN, D), new_state, new_k, new_v)
```

```python
import functools
import math

import jax
import jax.numpy as jnp
import numpy as np
from jax import lax
from jax.experimental import pallas as pl
from jax.experimental.pallas import tpu as pltpu

F32 = jnp.float32
BF16 = jnp.bfloat16
HIGHEST = lax.Precision.HIGHEST

D = 1024
N_CTX_SEQ, CTX_LEN = 32, 256
N_LAT_SEQ, LAT_LEN = 2, 4096
T_CTX = N_CTX_SEQ * CTX_LEN
T_LAT = N_LAT_SEQ * LAT_LEN
T_ALL = T_CTX + T_LAT
DEPTH = 2
HY_CH = 512
S5_CH = 512
S5_GROUPS, S5_GROUP, S5_STATE = 32, 16, 64
S5_NS = S5_GROUPS * S5_STATE
HY_EMB, HY_BANDS, HY_ORDER = 33, 16, 64
N_HEADS, HEAD_DIM = 16, 64
GRID_W, WIN_R, WIN_C = 64, 8, 16
N_EXPERTS, TOP_K, D_FF = 32, 4, 1024
SWIGLU_LIMIT, SWIGLU_ALPHA = 7.0, 1.702
LN_EPS = 1e-5
DN_ALPHA = (2 * DEPTH) ** 0.25
NEG_BIG = -1e30

ROW_TILE = 256
MOE_TILE = 256
N_ASSIGN = T_ALL * TOP_K
N_MOE_BLOCKS = N_ASSIGN // MOE_TILE
N_MOE_ITEMS = N_MOE_BLOCKS + N_EXPERTS - 1
COMB_TILE = 128
ROW_AS_TILE = (8, D // 8)
S5_CHUNK = 256
S5_SEG = S5_CHUNK // 8
S5_COLS = 512
S5_LANE_BLOCKS = S5_CH // 128
S5_LANE_GROUPS = S5_GROUPS // S5_LANE_BLOCKS
S5_LANE_STATES = S5_LANE_GROUPS * S5_STATE
VMEM_LIMIT = 56 * 1024 * 1024


def _cparams(sem, vmem=None):
    return pltpu.CompilerParams(dimension_semantics=sem, vmem_limit_bytes=vmem or VMEM_LIMIT)


def _mod_index(i, rows_per_tile):
    n_ctx = T_CTX // rows_per_tile
    per_lat = LAT_LEN // rows_per_tile
    return jnp.where(i < n_ctx, 0, 1 + (i - n_ctx) // per_lat)


def _mm_kernel(a_ref, b_ref, bias_ref, o_ref, acc_ref, *, nk, precise, silu_a):
    k = pl.program_id(2)

    @pl.when(k == 0)
    def _():
        acc_ref[...] = jnp.zeros_like(acc_ref)

    a = a_ref[...]
    if silu_a:
        a = a * jax.nn.sigmoid(a)
    if precise:
        acc_ref[...] += jnp.dot(a, b_ref[...], preferred_element_type=F32, precision=HIGHEST)
    else:
        acc_ref[...] += jnp.dot(a.astype(BF16), b_ref[...].astype(BF16), preferred_element_type=F32)

    @pl.when(k == nk - 1)
    def _():
        o_ref[...] = acc_ref[...] + bias_ref[...]


def matmul(a, b, bias=None, *, tm, tn, tk, b_layer=None, precise=False, silu_a=False):
    kdim, n = b.shape[-2:]
    m = a.shape[0]
    if bias is None:
        bias = jnp.zeros((1, n), F32)
    nk = kdim // tk
    if b.ndim == 3:
        b_spec = pl.BlockSpec((None, tk, tn), lambda i, j, k: (b_layer, k, j))
    else:
        b_spec = pl.BlockSpec((tk, tn), lambda i, j, k: (k, j))
    return pl.pallas_call(
        functools.partial(_mm_kernel, nk=nk, precise=precise, silu_a=silu_a),
        grid=(m // tm, n // tn, nk),
        in_specs=[pl.BlockSpec((tm, tk), lambda i, j, k: (i, k)), b_spec,
                  pl.BlockSpec((1, tn), lambda i, j, k: (0, j))],
        out_specs=pl.BlockSpec((tm, tn), lambda i, j, k: (i, j)),
        out_shape=jax.ShapeDtypeStruct((m, n), F32),
        scratch_shapes=[pltpu.VMEM((tm, tn), F32)],
        compiler_params=_cparams(("parallel", "parallel", "arbitrary")),
        name="matmul",
    )(a, b, bias)


N_CTX_TILES = T_CTX // ROW_TILE


def _group_pair_specs(a_ctx, a_lat):
    lat_off = N_CTX_TILES if a_lat.shape[0] == T_ALL else 0
    return [pl.BlockSpec((ROW_TILE, a_ctx.shape[1]), lambda i: (jnp.minimum(i, N_CTX_TILES - 1), 0)),
            pl.BlockSpec((ROW_TILE, a_lat.shape[1]), lambda i: (jnp.maximum(i - N_CTX_TILES, 0) + lat_off, 0))]


def _group_pair_load(ctx_ref, lat_ref):
    return jnp.where(pl.program_id(0) < N_CTX_TILES, ctx_ref[...], lat_ref[...])


def _modlinear_kernel(xc_ref, xl_ref, mod_ref, w_ref, o_ref):
    m = mod_ref[...]
    h = _group_pair_load(xc_ref, xl_ref) * (1.0 + m[1:2]) + m[0:1]
    o_ref[...] = jnp.dot(h.astype(BF16), w_ref[...], preferred_element_type=F32)


def modlinear(x_pair, mod, w_bf16):
    n = w_bf16.shape[1]
    return pl.pallas_call(
        _modlinear_kernel,
        grid=(T_ALL // ROW_TILE,),
        in_specs=_group_pair_specs(*x_pair) + [
            pl.BlockSpec((None, 8, D), lambda i: (_mod_index(i, ROW_TILE), 0, 0)),
            pl.BlockSpec((D, n), lambda i: (0, 0)),
        ],
        out_specs=pl.BlockSpec((ROW_TILE, n), lambda i: (i, 0)),
        out_shape=jax.ShapeDtypeStruct((T_ALL, n), F32),
        compiler_params=_cparams(("parallel",)),
        name="modlinear",
    )(*x_pair, mod, w_bf16)


def _filter_kernel(z_ref, w1_ref, b1_ref, w2_ref, b2_ref, w3_ref, fq_ref, dl_ref, hsum_ref, hdiff_ref, *, tile):
    z = z_ref[...]
    fq = fq_ref[...]
    h = jnp.sin(fq * (jnp.dot(z, w1_ref[...], preferred_element_type=F32, precision=HIGHEST) + b1_ref[...]))
    h = jnp.sin(fq * (jnp.dot(h, w2_ref[...], preferred_element_type=F32, precision=HIGHEST) + b2_ref[...]))
    h = jnp.dot(h, w3_ref[...], preferred_element_type=F32, precision=HIGHEST)
    decay = jnp.exp(-z[:, 0:1] * dl_ref[...])
    hf = h[:, :HY_CH] * decay
    hb = h[:, HY_CH:] * decay
    row = lax.broadcasted_iota(jnp.int32, (tile, 1), 0) + pl.program_id(0) * tile
    hsum_ref[...] = hf + hb
    hdiff_ref[...] = jnp.where(row == 0, hf + hb, hf - hb)


def hyena_filter_taps(seq_len, w1, b1, w2, b2, w3, freq):
    t = jnp.linspace(0.0, 1.0, seq_len, dtype=F32)[:, None]
    w = 2.0 * math.pi * jnp.arange(seq_len, dtype=F32)[:, None] / seq_len
    f = jnp.linspace(1e-4, HY_BANDS - 1, HY_BANDS, dtype=F32)[None, :]
    z = jnp.concatenate([t, jnp.cos(f * w), -jnp.sin(f * w)], -1)
    z = jnp.pad(z, ((0, 0), (0, 128 - HY_EMB)))
    pad_o = 128 - HY_ORDER
    w1p = jnp.pad(w1, ((0, 128 - HY_EMB), (0, pad_o)))
    w2p = jnp.pad(w2, ((0, pad_o), (0, pad_o)))
    w3p = jnp.pad(w3, ((0, pad_o), (0, 0)))
    b1p = jnp.pad(b1, (0, pad_o))[None]
    b2p = jnp.pad(b2, (0, pad_o))[None]
    fqp = jnp.pad(freq, (0, pad_o))[None]
    max_decay = math.log(1e-2) / 0.3
    min_decay = math.log(1e-2) / 1.5
    absdelta = jnp.abs(jnp.linspace(min_decay, max_decay, HY_CH, dtype=F32))[None]
    tile = 256
    full = lambda shp: pl.BlockSpec(shp, lambda i: (0, 0))
    return pl.pallas_call(
        functools.partial(_filter_kernel, tile=tile),
        grid=(seq_len // tile,),
        in_specs=[pl.BlockSpec((tile, 128), lambda i: (i, 0)), full((128, 128)), full((1, 128)), full((128, 128)),
                  full((1, 128)), full((128, 2 * HY_CH)), full((1, 128)), full((1, HY_CH))],
        out_specs=[pl.BlockSpec((tile, HY_CH), lambda i: (i, 0))] * 2,
        out_shape=[jax.ShapeDtypeStruct((seq_len, HY_CH), F32)] * 2,
        compiler_params=_cparams(("parallel",)),
        name="hyena_filter",
    )(z, w1p, b1p, w2p, b2p, w3p, fqp, absdelta)


def _dft_tables(seq_len):
    s = int(round(math.sqrt(seq_len)))
    j = jnp.arange(s, dtype=jnp.int32)[:, None]
    c = jnp.arange(seq_len, dtype=jnp.int32)[None, :]
    unit = math.pi / (2 * seq_len)

    def cos_sin(idx):
        ang = (idx % (4 * seq_len)).astype(F32) * unit
        return jnp.cos(ang), jnp.sin(ang)

    def expand(ca, sa, cb, sb):
        cos = ca[:, None, :] * cb[None, :, :] - sa[:, None, :] * sb[None, :, :]
        nsin = -(sa[:, None, :] * cb[None, :, :] + ca[:, None, :] * sb[None, :, :])
        return cos.reshape(seq_len, seq_len).astype(BF16), nsin.reshape(seq_len, seq_len).astype(BF16)

    cm, nsm = expand(*cos_sin((2 * s * j) * c), *cos_sin((2 * j + 1) * c))
    cmt, nsmt = expand(*cos_sin((2 * c + 1) * (s * j)), *cos_sin((2 * c + 1) * j))
    return cm, nsm, cmt, nsmt


def _hyena_pre_kernel(u0_ref, u1_ref, u2_ref, w0_ref, w1_ref, w2_ref, b0_ref, b1_ref, b2_ref,
                      vx_ref, vxb_ref, x0_ref, *, seq_len):
    row = lax.broadcasted_iota(jnp.int32, (seq_len, 1), 0)

    def short_conv(u_ref, w_ref, b_ref):
        a = u_ref[...]
        w = w_ref[...]
        prev = jnp.where(row == 0, 0.0, pltpu.roll(a, 1, 0))
        nxt = jnp.where(row == seq_len - 1, 0.0, pltpu.roll(a, seq_len - 1, 0))
        return prev * w[0:1] + a * w[1:2] + nxt * w[2:3] + b_ref[...]

    x0 = short_conv(u0_ref, w0_ref, b0_ref)
    x1 = short_conv(u1_ref, w1_ref, b1_ref)
    v = short_conv(u2_ref, w2_ref, b2_ref)
    vx = v * x1
    vx_ref[...] = vx
    vxb_ref[...] = vx.astype(BF16)
    x0_ref[...] = x0


def hyena_pre(u, conv_w, conv_b, *, n_seq, seq_len, row_off):
    cb = min(HY_CH, 128 * LAT_LEN // seq_len)
    ncb = HY_CH // cb
    rb0 = row_off // seq_len
    uspec = lambda part: pl.BlockSpec((seq_len, cb), lambda b, j: (rb0 + b, part * ncb + j))
    wspec = lambda part: pl.BlockSpec((3, cb), lambda b, j: (0, part * ncb + j))
    bspec = lambda part: pl.BlockSpec((1, cb), lambda b, j: (0, part * ncb + j))
    ospec = pl.BlockSpec((seq_len, cb), lambda b, j: (b, j))
    rows = n_seq * seq_len
    return pl.pallas_call(
        functools.partial(_hyena_pre_kernel, seq_len=seq_len),
        grid=(n_seq, ncb),
        in_specs=[uspec(0), uspec(1), uspec(2), wspec(0), wspec(1), wspec(2), bspec(0), bspec(1), bspec(2)],
        out_specs=[ospec, ospec, ospec],
        out_shape=[jax.ShapeDtypeStruct((rows, HY_CH), F32), jax.ShapeDtypeStruct((rows, HY_CH), BF16),
                   jax.ShapeDtypeStruct((rows, HY_CH), F32)],
        compiler_params=_cparams(("parallel", "parallel")),
        name="hyena_pre",
    )(u, u, u, conv_w, conv_w, conv_w, conv_b, conv_b, conv_b)


def _dft_fwd_kernel(cm_ref, nsm_ref, v_ref, hr_ref, hi_ref, zr_ref, zi_ref, accr_ref, acci_ref, *, nk):
    k = pl.program_id(3)

    @pl.when(k == 0)
    def _():
        accr_ref[...] = jnp.zeros_like(accr_ref)
        acci_ref[...] = jnp.zeros_like(acci_ref)

    v = v_ref[...]
    accr_ref[...] += jnp.dot(cm_ref[...], v, preferred_element_type=F32)
    acci_ref[...] += jnp.dot(nsm_ref[...], v, preferred_element_type=F32)

    @pl.when(k == nk - 1)
    def _():
        xr, xi = accr_ref[...], acci_ref[...]
        hr, hi = hr_ref[...], hi_ref[...]
        zr_ref[...] = (xr * hr - xi * hi).astype(BF16)
        zi_ref[...] = (xr * hi + xi * hr).astype(BF16)


def dft_fwd(cm, nsm, vxb, hr, hi, *, n_seq, seq_len, tile):
    nk = seq_len // tile
    tn = HY_CH
    v3 = vxb.reshape(n_seq, seq_len, HY_CH)
    zspec = pl.BlockSpec((None, tile, tn), lambda s, i, j, k: (s, i, j))
    return pl.pallas_call(
        functools.partial(_dft_fwd_kernel, nk=nk),
        grid=(n_seq, seq_len // tile, HY_CH // tn, nk),
        in_specs=[
            pl.BlockSpec((tile, tile), lambda s, i, j, k: (i, k)),
            pl.BlockSpec((tile, tile), lambda s, i, j, k: (i, k)),
            pl.BlockSpec((None, tile, tn), lambda s, i, j, k: (s, k, j)),
            pl.BlockSpec((tile, tn), lambda s, i, j, k: (i, j)),
            pl.BlockSpec((tile, tn), lambda s, i, j, k: (i, j)),
        ],
        out_specs=[zspec, zspec],
        out_shape=[jax.ShapeDtypeStruct((n_seq, seq_len, HY_CH), BF16)] * 2,
        scratch_shapes=[pltpu.VMEM((tile, tn), F32), pltpu.VMEM((tile, tn), F32)],
        compiler_params=_cparams(("parallel", "parallel", "parallel", "arbitrary")),
        name="dft_fwd",
    )(cm, nsm, v3, hr, hi)


def _dft_inv_kernel(cmt_ref, nsmt_ref, zr_ref, zi_ref, vx_ref, x0_ref, bias_ref, o_ref, acc_ref, *, nk, inv_len):
    k = pl.program_id(3)

    @pl.when(k == 0)
    def _():
        acc_ref[...] = jnp.zeros_like(acc_ref)

    acc_ref[...] += (jnp.dot(cmt_ref[...], zr_ref[...], preferred_element_type=F32)
                     + jnp.dot(nsmt_ref[...], zi_ref[...], preferred_element_type=F32))

    @pl.when(k == nk - 1)
    def _():
        conv = acc_ref[...] * inv_len
        o_ref[...] = (conv + vx_ref[...] * bias_ref[...]) * x0_ref[...]


def dft_inv(cmt, nsmt, zr, zi, vx, x0, bias, *, n_seq, seq_len, tile):
    nk = seq_len // tile
    tn = HY_CH
    per_seq = seq_len // tile
    zspec = pl.BlockSpec((None, tile, tn), lambda s, i, j, k: (s, k, j))
    espec = pl.BlockSpec((None, tile, tn), lambda s, i, j, k: (s, i, j))
    return pl.pallas_call(
        functools.partial(_dft_inv_kernel, nk=nk, inv_len=1.0 / seq_len),
        grid=(n_seq, per_seq, HY_CH // tn, nk),
        in_specs=[
            pl.BlockSpec((tile, tile), lambda s, i, j, k: (i, k)),
            pl.BlockSpec((tile, tile), lambda s, i, j, k: (i, k)),
            zspec, zspec, espec, espec,
            pl.BlockSpec((1, tn), lambda s, i, j, k: (0, j)),
        ],
        out_specs=pl.BlockSpec((tile, tn), lambda s, i, j, k: (s * per_seq + i, j)),
        out_shape=jax.ShapeDtypeStruct((n_seq * seq_len, HY_CH), F32),
        scratch_shapes=[pltpu.VMEM((tile, tn), F32)],
        compiler_params=_cparams(("parallel", "parallel", "parallel", "arbitrary")),
        name="dft_inv",
    )(cmt, nsmt, zr, zi, vx.reshape(n_seq, seq_len, HY_CH), x0.reshape(n_seq, seq_len, HY_CH), bias)


def hyena_group(u, conv_w, conv_b, taps, bias, *, n_seq, seq_len, row_off, tile):
    hsum, hdiff = taps
    cm, nsm, cmt, nsmt = _dft_tables(seq_len)
    hr = matmul(cm, hsum, tm=tile, tn=HY_CH, tk=tile)
    hi = matmul(nsm, hdiff, tm=tile, tn=HY_CH, tk=tile)
    vx, vxb, x0 = hyena_pre(u, conv_w, conv_b, n_seq=n_seq, seq_len=seq_len, row_off=row_off)
    zr, zi = dft_fwd(cm, nsm, vxb, hr, hi, n_seq=n_seq, seq_len=seq_len, tile=tile)
    return dft_inv(cmt, nsmt, zr, zi, vx, x0, bias, n_seq=n_seq, seq_len=seq_len, tile=tile)


def _s5_tables(lam_re, lam_im, log_dt, b_re, b_im, c_re, c_im, reverse):
    dt = jnp.exp(log_dt)[:, None]
    a = lam_re * dt
    b = lam_im * dt
    mag = jnp.exp(a)
    lbr, lbi = mag * jnp.cos(b), mag * jnp.sin(b)
    den = lam_re * lam_re + lam_im * lam_im
    qr = ((lbr - 1.0) * lam_re + lbi * lam_im) / den
    qi = (lbi * lam_re - (lbr - 1.0) * lam_im) / den
    bbr = qr[..., None] * b_re - qi[..., None] * b_im
    bbi = qr[..., None] * b_im + qi[..., None] * b_re
    eye = jnp.eye(S5_LANE_GROUPS, dtype=F32)

    def in_block(m):
        m = m.reshape(S5_LANE_BLOCKS, S5_LANE_GROUPS, S5_STATE, S5_GROUP)
        return jnp.einsum("qlnc,lm->qlcmn", m, eye).reshape(S5_LANE_BLOCKS, 128, S5_LANE_STATES)

    def out_block(m):
        m = m.reshape(S5_LANE_BLOCKS, S5_LANE_GROUPS, S5_GROUP, S5_STATE)
        return jnp.einsum("qlcn,lm->qlnmc", m, eye).reshape(S5_LANE_BLOCKS, S5_LANE_STATES, 128)

    b_blk = jnp.concatenate([in_block(bbr), in_block(bbi)], axis=2).astype(BF16)
    c_blk = jnp.concatenate([out_block(c_re), out_block(-c_im)], axis=1).astype(BF16)
    flat = lambda m: m.reshape(1, S5_NS)
    lam = jnp.concatenate([flat(lbr), flat(lbi)], axis=1)
    steps = jnp.arange(1, S5_SEG + 1, dtype=F32)
    if reverse:
        steps = steps[::-1]
    pa = a.reshape(1, S5_NS) * steps[:, None]
    pb = b.reshape(1, S5_NS) * steps[:, None]
    pw = jnp.concatenate([jnp.exp(pa) * jnp.cos(pb), jnp.exp(pa) * jnp.sin(pb)], axis=1)
    sa, sb = flat(a) * S5_SEG, flat(b) * S5_SEG
    lam_seg = jnp.concatenate([jnp.exp(sa) * jnp.cos(sb), jnp.exp(sa) * jnp.sin(sb)], axis=1)
    return b_blk, c_blk, lam, lam_seg, pw


N_S5_CTX_STEPS = T_CTX // S5_CHUNK
N_S5_LAT_CHUNKS = LAT_LEN // S5_CHUNK
N_S5_STEPS = T_ALL // S5_CHUNK


def _s5_step_info(i, reverse):
    k = jnp.maximum(i - N_S5_CTX_STEPS, 0)
    b, j = k // N_S5_LAT_CHUNKS, k % N_S5_LAT_CHUNKS
    chunk = (N_S5_LAT_CHUNKS - 1 - j) if reverse else j
    is_ctx = i < N_S5_CTX_STEPS
    row_blk = jnp.where(is_ctx, i, N_S5_CTX_STEPS + b * N_S5_LAT_CHUNKS + chunk)
    seq = jnp.where(is_ctx, i, N_CTX_SEQ + b)
    return row_blk, seq, is_ctx | (j == 0), is_ctx | (j == N_S5_LAT_CHUNKS - 1)


def _s5_kernel(u_ref, perm_ref, permt_ref, bblk_ref, cblk_ref, lam_ref, lseg_ref, pw_ref, h0_ref, y_ref, fin_ref,
               bu_ref, hend_ref, cin_ref, carry_ref, *, reverse):
    _, _, is_first, is_last = _s5_step_info(pl.program_id(0), reverse)
    ns = S5_NS

    @pl.when(is_first)
    def _():
        carry_ref[...] = h0_ref[...]

    u_seg = jnp.dot(perm_ref[...], u_ref[...].astype(BF16), preferred_element_type=F32).astype(BF16)
    st_blk = S5_LANE_STATES
    for q in range(S5_LANE_BLOCKS):
        u_q = u_seg[:, q * 128:(q + 1) * 128]
        for part in range(2):
            bu_ref[:, pl.ds(part * ns + q * st_blk, st_blk)] = jnp.dot(
                u_q, bblk_ref[q, :, part * st_blk:(part + 1) * st_blk], preferred_element_type=F32)

    for cb in range(ns // S5_COLS):
        re_cols = pl.ds(cb * S5_COLS, S5_COLS)
        im_cols = pl.ds(ns + cb * S5_COLS, S5_COLS)
        lr = jnp.broadcast_to(lam_ref[:, re_cols], (8, S5_COLS))
        li = jnp.broadcast_to(lam_ref[:, im_cols], (8, S5_COLS))

        def step(kk, carry, re_cols=re_cols, im_cols=im_cols, lr=lr, li=li):
            hr, hi = carry
            k = (S5_SEG - 1 - kk) if reverse else kk
            rows = pl.ds(pl.multiple_of(k * 8, 8), 8)
            nr = lr * hr - li * hi + bu_ref[rows, re_cols]
            ni = lr * hi + li * hr + bu_ref[rows, im_cols]
            bu_ref[rows, re_cols] = nr
            bu_ref[rows, im_cols] = ni
            return nr, ni

        zero = jnp.zeros((8, S5_COLS), F32)
        hr, hi = lax.fori_loop(0, S5_SEG, step, (zero, zero), unroll=4)
        hend_ref[:, re_cols] = hr
        hend_ref[:, im_cols] = hi

    cr = carry_ref[:, :ns]
    ci = carry_ref[:, ns:]
    lsr = lseg_ref[:, :ns]
    lsi = lseg_ref[:, ns:]
    for p in range(8):
        s = 7 - p if reverse else p
        cin_ref[s:s + 1, :ns] = cr
        cin_ref[s:s + 1, ns:] = ci
        er = hend_ref[s:s + 1, :ns]
        ei = hend_ref[s:s + 1, ns:]
        cr, ci = er + lsr * cr - lsi * ci, ei + lsr * ci + lsi * cr
    carry_ref[:, :ns] = cr
    carry_ref[:, ns:] = ci

    for cb in range(ns // S5_COLS):
        re_cols = pl.ds(cb * S5_COLS, S5_COLS)
        im_cols = pl.ds(ns + cb * S5_COLS, S5_COLS)
        cinr = cin_ref[:, re_cols]
        cini = cin_ref[:, im_cols]

        def fix(k, _, re_cols=re_cols, im_cols=im_cols, cinr=cinr, cini=cini):
            rows = pl.ds(pl.multiple_of(k * 8, 8), 8)
            pr = pw_ref[pl.ds(k, 1), re_cols]
            pi = pw_ref[pl.ds(k, 1), im_cols]
            bu_ref[rows, re_cols] += pr * cinr - pi * cini
            bu_ref[rows, im_cols] += pr * cini + pi * cinr
            return 0

        lax.fori_loop(0, S5_SEG, fix, 0, unroll=4)

    y_parts = []
    for q in range(S5_LANE_BLOCKS):
        acc = None
        for part in range(2):
            h_q = bu_ref[:, pl.ds(part * ns + q * st_blk, st_blk)].astype(BF16)
            d = jnp.dot(h_q, cblk_ref[q, part * st_blk:(part + 1) * st_blk, :], preferred_element_type=F32)
            acc = d if acc is None else acc + d
        y_parts.append(acc)
    y_seg = jnp.concatenate(y_parts, axis=1)
    y_hi = y_seg.astype(BF16)
    y_lo = (y_seg - y_hi.astype(F32)).astype(BF16)
    y_ref[...] = (jnp.dot(permt_ref[...], y_hi, preferred_element_type=F32)
                  + jnp.dot(permt_ref[...], y_lo, preferred_element_type=F32))

    @pl.when(is_last)
    def _():
        fin_ref[...] = carry_ref[...]


def s5_direction(u, tables, h0, *, reverse):
    b_blk, c_blk, lam, lam_seg, pw = tables
    s5_col = 3 * HY_CH // S5_CH
    n_seq = N_CTX_SEQ + N_LAT_SEQ
    row_blk = lambda i: _s5_step_info(i, reverse)[0]
    seq = lambda i: _s5_step_info(i, reverse)[1]
    const = lambda shp: pl.BlockSpec(shp, lambda i: (0, 0))
    t = np.arange(S5_CHUNK)
    perm_np = np.zeros((S5_CHUNK, S5_CHUNK), np.float32)
    perm_np[(t % S5_SEG) * 8 + t // S5_SEG, t] = 1.0
    perm = jnp.asarray(perm_np, dtype=BF16)
    perm_t = jnp.asarray(perm_np.T, dtype=BF16)
    return pl.pallas_call(
        functools.partial(_s5_kernel, reverse=reverse),
        grid=(N_S5_STEPS,),
        in_specs=[
            pl.BlockSpec((S5_CHUNK, S5_CH), lambda i: (row_blk(i), s5_col)),
            const((S5_CHUNK, S5_CHUNK)), const((S5_CHUNK, S5_CHUNK)),
            pl.BlockSpec((S5_LANE_BLOCKS, 128, 2 * S5_LANE_STATES), lambda i: (0, 0, 0)),
            pl.BlockSpec((S5_LANE_BLOCKS, 2 * S5_LANE_STATES, 128), lambda i: (0, 0, 0)),
            const((1, 2 * S5_NS)), const((1, 2 * S5_NS)),
            const((S5_SEG, 2 * S5_NS)),
            pl.BlockSpec((None, 1, 2 * S5_NS), lambda i: (seq(i), 0, 0)),
        ],
        out_specs=[pl.BlockSpec((S5_CHUNK, S5_CH), lambda i: (row_blk(i), 0)),
                   pl.BlockSpec((None, 1, 2 * S5_NS), lambda i: (seq(i), 0, 0))],
        out_shape=[jax.ShapeDtypeStruct((T_ALL, S5_CH), F32),
                   jax.ShapeDtypeStruct((n_seq, 1, 2 * S5_NS), F32)],
        scratch_shapes=[pltpu.VMEM((S5_CHUNK, 2 * S5_NS), F32), pltpu.VMEM((8, 2 * S5_NS), F32),
                        pltpu.VMEM((8, 2 * S5_NS), F32), pltpu.VMEM((1, 2 * S5_NS), F32)],
        compiler_params=_cparams(("arbitrary",)),
        name="s5_bwd" if reverse else "s5_fwd",
    )(u, perm, perm_t, b_blk, c_blk, lam, lam_seg, pw, h0)


def _s5_post_kernel(u_ref, yf_ref, yb_ref, d_ref, w_ref, b_ref, o_ref):
    y = d_ref[...] * u_ref[...] + yf_ref[...] + yb_ref[...]
    cdf = 0.5 * (1.0 + jnp.tanh(math.sqrt(2.0 / math.pi) * (y + 0.044715 * (y * y * y))))
    g = y * cdf
    z = jnp.dot(g.astype(BF16), w_ref[...], preferred_element_type=F32) + b_ref[...]
    o_ref[...] = z[:, :S5_CH] * jax.nn.sigmoid(z[:, S5_CH:])


def s5_post(u, yf, yb, d, glu_w, glu_b):
    row = pl.BlockSpec((ROW_TILE, S5_CH), lambda i: (i, 0))
    return pl.pallas_call(
        _s5_post_kernel,
        grid=(T_ALL // ROW_TILE,),
        in_specs=[pl.BlockSpec((ROW_TILE, S5_CH), lambda i: (i, 3 * HY_CH // S5_CH)), row, row,
                  pl.BlockSpec((1, S5_CH), lambda i: (0, 0)),
                  pl.BlockSpec((S5_CH, 2 * S5_CH), lambda i: (0, 0)),
                  pl.BlockSpec((1, 2 * S5_CH), lambda i: (0, 0))],
        out_specs=row,
        out_shape=jax.ShapeDtypeStruct((T_ALL, S5_CH), F32),
        compiler_params=_cparams(("parallel",)),
        name="s5_post",
    )(u, yf, yb, d, glu_w, glu_b)


def _head_mask(shape):
    return lax.broadcasted_iota(jnp.int32, shape, 1) < HEAD_DIM


def _ctx_attn_kernel(q_ref, k_ref, v_ref, o_ref, kcache_ref, vcache_ref):
    kcache_ref[...] = k_ref[...].reshape(CTX_LEN, N_HEADS, HEAD_DIM)
    vcache_ref[...] = v_ref[...].reshape(CTX_LEN, N_HEADS, HEAD_DIM)
    low = _head_mask((CTX_LEN, 128))
    for hp in range(N_HEADS // 2):
        cols = pl.ds(hp * 128, 128)
        a = _stack_heads(q_ref[:, cols] * (HEAD_DIM ** -0.5), low)
        k = k_ref[:, cols].astype(BF16)
        v = v_ref[:, cols].astype(BF16)
        s = lax.dot_general(a, k, (((1,), (1,)), ((), ())), preferred_element_type=F32)
        m = jnp.max(s, axis=-1, keepdims=True)
        p = jnp.exp(s - m)
        den = jnp.sum(p, axis=-1, keepdims=True)
        o = jnp.dot(p.astype(BF16), v, preferred_element_type=F32) / den
        o_ref[:, cols] = jnp.where(low, o[:CTX_LEN], o[CTX_LEN:])


def _stack_heads(q, low):
    return jnp.concatenate([jnp.where(low, q, 0.0), jnp.where(low, 0.0, q)], axis=0).astype(BF16)


def ctx_attention(qkv):
    blk = lambda part: pl.BlockSpec((CTX_LEN, D), lambda b: (b, part))
    cache_shape = jax.ShapeDtypeStruct((N_CTX_SEQ, 1, CTX_LEN, N_HEADS, HEAD_DIM), F32)
    cache_spec = pl.BlockSpec((None, None, CTX_LEN, N_HEADS, HEAD_DIM), lambda b: (b, 0, 0, 0, 0))
    return pl.pallas_call(
        _ctx_attn_kernel,
        grid=(N_CTX_SEQ,),
        in_specs=[blk(0), blk(1), blk(2)],
        out_specs=[pl.BlockSpec((CTX_LEN, D), lambda b: (b, 0)), cache_spec, cache_spec],
        out_shape=[jax.ShapeDtypeStruct((T_CTX, D), F32), cache_shape, cache_shape],
        compiler_params=_cparams(("parallel",)),
        name="ctx_attention",
    )(qkv, qkv, qkv)


NA_ROWS_PER_STEP = 8


def _na_kernel(q_ref, k_ref, v_ref, kc_ref, vc_ref, bias_ref, o_ref):
    rb = pl.program_id(2)
    kc = kc_ref[...]
    vc = vc_ref[...]
    low = _head_mask((GRID_W, 128))
    n_rows = LAT_LEN // GRID_W
    for i in range(NA_ROWS_PER_STEP):
        r = rb * NA_ROWS_PER_STEP + i
        rs = jnp.clip(r - WIN_R // 2, 0, n_rows - WIN_R)
        dlt = r - rs
        rows = pl.ds(pl.multiple_of(rs * GRID_W, GRID_W), WIN_R * GRID_W)
        kw = k_ref[rows, :].astype(BF16)
        vw = v_ref[rows, :].astype(BF16)
        a = _stack_heads(q_ref[i * GRID_W:(i + 1) * GRID_W, :] * (HEAD_DIM ** -0.5), low)
        s = lax.dot_general(a, kw, (((1,), (1,)), ((), ())), preferred_element_type=F32)
        s = s + bias_ref[dlt].reshape(2 * GRID_W, WIN_R * GRID_W)
        sc = lax.dot_general(a, kc, (((1,), (1,)), ((), ())), preferred_element_type=F32)
        m = jnp.maximum(jnp.max(s, axis=-1, keepdims=True), jnp.max(sc, axis=-1, keepdims=True))
        p = jnp.exp(s - m)
        pc = jnp.exp(sc - m)
        den = jnp.sum(p, axis=-1, keepdims=True) + jnp.sum(pc, axis=-1, keepdims=True)
        o = (jnp.dot(p.astype(BF16), vw, preferred_element_type=F32)
             + jnp.dot(pc.astype(BF16), vc, preferred_element_type=F32)) / den
        o_ref[i * GRID_W:(i + 1) * GRID_W, :] = jnp.where(low, o[:GRID_W], o[GRID_W:])


def _na_bias_table(rpb):
    j = np.arange(GRID_W)
    cs = np.clip(j - WIN_C // 2, 0, GRID_W - WIN_C)
    c = np.arange(GRID_W)
    inside = (c[None, :] >= cs[:, None]) & (c[None, :] < cs[:, None] + WIN_C)
    col_rel = c[None, :] - j[:, None] + WIN_C - 1
    n_rel_c = 2 * WIN_C - 1
    onehot = ((np.arange(n_rel_c)[:, None, None] == col_rel[None]) & inside[None]).astype(np.float32)
    t = jnp.dot(rpb.reshape(N_HEADS * (2 * WIN_R - 1), n_rel_c), jnp.asarray(onehot.reshape(n_rel_c, -1)),
                precision=HIGHEST)
    t = t.reshape(N_HEADS, 2 * WIN_R - 1, GRID_W, GRID_W)
    t = t + jnp.asarray(np.where(inside, 0.0, NEG_BIG).astype(np.float32))
    t = t.transpose(0, 2, 1, 3)
    t = jnp.stack([t[:, :, WIN_R - 1 - dl:2 * WIN_R - 1 - dl] for dl in range(WIN_R)], axis=0)
    return t.reshape(WIN_R, N_HEADS, GRID_W, WIN_R * GRID_W)


def na_attention(qkv, kcb, vcb, bias):
    hp = N_HEADS // 2
    q_rows = NA_ROWS_PER_STEP * GRID_W
    steps = LAT_LEN // q_rows
    q_off = T_CTX // q_rows
    seq_off = T_CTX // LAT_LEN
    return pl.pallas_call(
        _na_kernel,
        grid=(N_LAT_SEQ, hp, steps),
        in_specs=[
            pl.BlockSpec((q_rows, 128), lambda b, h, r: (q_off + b * steps + r, h)),
            pl.BlockSpec((LAT_LEN, 128), lambda b, h, r: (seq_off + b, hp + h)),
            pl.BlockSpec((LAT_LEN, 128), lambda b, h, r: (seq_off + b, 2 * hp + h)),
            pl.BlockSpec((CTX_LEN, 128), lambda b, h, r: (b, h)),
            pl.BlockSpec((CTX_LEN, 128), lambda b, h, r: (b, h)),
            pl.BlockSpec((WIN_R, 2, GRID_W, WIN_R * GRID_W), lambda b, h, r: (0, h, 0, 0)),
        ],
        out_specs=pl.BlockSpec((q_rows, 128), lambda b, h, r: (b * steps + r, h)),
        out_shape=jax.ShapeDtypeStruct((T_LAT, D), F32),
        compiler_params=_cparams(("parallel", "parallel", "arbitrary")),
        name="na_attention",
    )(qkv, qkv, qkv, kcb, vcb, bias)


def _layer_norm(r, g, b):
    mu = jnp.mean(r, axis=-1, keepdims=True)
    c = r - mu
    var = jnp.mean(c * c, axis=-1, keepdims=True)
    return c * lax.rsqrt(var + LN_EPS) * g + b


def _proj_kernel(*refs, n_in):
    a_refs = refs[:2 * n_in]
    w_refs = refs[2 * n_in:3 * n_in]
    (xc_ref, xl_ref, mod_ref, g_ref, b_ref, rwh_ref, rwl_ref, rb_ref, tri_ref,
     x1_ref, h2_ref, exp_ref, pos_ref, gate_ref, cnt_ref, run_ref) = refs[3 * n_in:]

    @pl.when(pl.program_id(0) == 0)
    def _():
        run_ref[...] = jnp.zeros_like(run_ref)

    y = None
    for n, w_ref in enumerate(w_refs):
        a = _group_pair_load(a_refs[2 * n], a_refs[2 * n + 1])
        part = jnp.dot(a.astype(BF16), w_ref[...], preferred_element_type=F32)
        y = part if y is None else y + part
    m = mod_ref[...]
    x1 = _layer_norm(DN_ALPHA * _group_pair_load(xc_ref, xl_ref) + m[2:3] * y, g_ref[...], b_ref[...])
    x1_ref[...] = x1
    h2 = x1 * (1.0 + m[4:5]) + m[3:4]
    h2_ref[...] = h2.reshape(ROW_TILE, *ROW_AS_TILE)

    hh = h2.astype(BF16)
    hl = (h2 - hh.astype(F32)).astype(BF16)
    logits = (jnp.dot(hh, rwh_ref[...], preferred_element_type=F32)
              + jnp.dot(hh, rwl_ref[...], preferred_element_type=F32)
              + jnp.dot(hl, rwh_ref[...], preferred_element_type=F32)) + rb_ref[...]

    lane = lax.broadcasted_iota(jnp.int32, logits.shape, 1)
    work = logits
    vals, sels = [], []
    for k in range(TOP_K):
        mx = jnp.max(work, axis=-1, keepdims=True)
        idx = jnp.min(jnp.where(work == mx, lane, 128), axis=-1, keepdims=True)
        sel = lane == idx
        work = jnp.where(sel, -jnp.inf, work)
        exp_ref[:, k:k + 1] = idx
        vals.append(mx)
        sels.append(sel)
    exps = [jnp.exp(v - vals[0]) for v in vals]
    den = exps[0] + exps[1] + exps[2] + exps[3]
    for k in range(TOP_K):
        gate_ref[:, k:k + 1] = exps[k] / den

    picked = sels[0] | sels[1] | sels[2] | sels[3]
    onehot = jnp.where(picked, 1.0, 0.0)
    incl = jnp.dot(tri_ref[...], onehot.astype(BF16), preferred_element_type=F32)
    before = incl - onehot + run_ref[...]
    for k in range(TOP_K):
        pos = jnp.sum(jnp.where(sels[k], before, 0.0), axis=-1, keepdims=True)
        pos_ref[:, k:k + 1] = pos.astype(jnp.int32)
    run_ref[...] += jnp.sum(onehot, axis=0, keepdims=True)
    cnt_ref[...] = run_ref[...]


def proj_res_ln(acts, weights, x_pair, mod, ln_g, ln_b, rw_hi, rw_lo, rb):
    n_in = len(acts)
    row = lambda c: pl.BlockSpec((ROW_TILE, c), lambda i: (i, 0))
    full = lambda shp: pl.BlockSpec(shp, lambda i: (0, 0))
    tri = jnp.asarray(np.tril(np.ones((ROW_TILE, ROW_TILE), np.float32)), dtype=BF16)
    act_arrays, act_specs = [], []
    for pair in acts:
        act_arrays += list(pair)
        act_specs += _group_pair_specs(*pair)
    in_specs = (act_specs + [full(w.shape) for w in weights] + _group_pair_specs(*x_pair)
                + [pl.BlockSpec((None, 8, D), lambda i: (_mod_index(i, ROW_TILE), 0, 0)),
                   full((1, D)), full((1, D)), full((D, 128)), full((D, 128)), full((1, 128)),
                   full((ROW_TILE, ROW_TILE))])
    return pl.pallas_call(
        functools.partial(_proj_kernel, n_in=n_in),
        grid=(T_ALL // ROW_TILE,),
        in_specs=in_specs,
        out_specs=[row(D), pl.BlockSpec((ROW_TILE, *ROW_AS_TILE), lambda i: (i, 0, 0)),
                   row(TOP_K), row(TOP_K), row(TOP_K), full((1, 128))],
        out_shape=[jax.ShapeDtypeStruct((T_ALL, D), F32), jax.ShapeDtypeStruct((T_ALL, *ROW_AS_TILE), F32),
                   jax.ShapeDtypeStruct((T_ALL, TOP_K), jnp.int32), jax.ShapeDtypeStruct((T_ALL, TOP_K), jnp.int32),
                   jax.ShapeDtypeStruct((T_ALL, TOP_K), F32), jax.ShapeDtypeStruct((1, 128), F32)],
        scratch_shapes=[pltpu.VMEM((1, 128), F32)],
        compiler_params=_cparams(("arbitrary",)),
        name="proj_res_ln",
    )(*act_arrays, *weights, *x_pair, mod, ln_g, ln_b, rw_hi, rw_lo, rb, tri)


def _dispatch_kernel(dest_ref, h_ref, xs_ref, sem):
    def copy(r, d):
        return pltpu.make_async_copy(h_ref.at[r], xs_ref.at[d], sem)

    def issue(r, _):
        for k in range(TOP_K):
            copy(r, dest_ref[r * TOP_K + k]).start(priority=k % 2)
        return 0

    lax.fori_loop(0, ROW_TILE, issue, 0, unroll=4)

    def drain(r, _):
        for k in range(TOP_K):
            copy(r, dest_ref[r * TOP_K + k]).wait()
        return 0

    lax.fori_loop(0, ROW_TILE, drain, 0, unroll=4)


def moe_dispatch(h2, dest_flat):
    return pl.pallas_call(
        _dispatch_kernel,
        grid=(T_ALL // ROW_TILE,),
        in_specs=[pl.BlockSpec((ROW_TILE * TOP_K,), lambda i: (i,), memory_space=pltpu.SMEM),
                  pl.BlockSpec((ROW_TILE, *ROW_AS_TILE), lambda i: (i, 0, 0))],
        out_specs=pl.BlockSpec(memory_space=pl.ANY),
        out_shape=jax.ShapeDtypeStruct((N_ASSIGN, *ROW_AS_TILE), F32),
        scratch_shapes=[pltpu.SemaphoreType.DMA],
        compiler_params=_cparams(("arbitrary",)),
        name="moe_dispatch",
    )(dest_flat, h2)


W_CAST_ROWS = 128


def _moe_ffn_kernel(blk_ref, exp_ref, lo_ref, hi_ref, first_ref, newexp_ref, nxt_ref,
                    x_ref, w1_ref, b1_ref, w2_ref, b2_ref, o_ref, w1f_ref, w2f_ref, w1b_ref, w2b_ref, sems, *, layer):
    i = pl.program_id(0)
    lo = lo_ref[i]
    hi = hi_ref[i]

    def weights(e):
        return (pltpu.make_async_copy(w1_ref.at[layer, e], w1f_ref, sems.at[0]),
                pltpu.make_async_copy(w2_ref.at[layer, e], w2f_ref, sems.at[1]))

    @pl.when(i == 0)
    def _():
        for c in weights(exp_ref[0]):
            c.start()

    @pl.when(newexp_ref[i] == 1)
    def _():
        for c in weights(exp_ref[i]):
            c.wait()

        def cast(c, _):
            rows = pl.ds(pl.multiple_of(c * W_CAST_ROWS, W_CAST_ROWS), W_CAST_ROWS)
            w1b_ref[rows, :] = w1f_ref[rows, :].astype(BF16)
            w2b_ref[rows, :] = w2f_ref[rows, :].astype(BF16)
            return 0

        lax.fori_loop(0, D // W_CAST_ROWS, cast, 0)

        @pl.when(nxt_ref[i] >= 0)
        def _():
            for c in weights(nxt_ref[i]):
                c.start(priority=1)

    @pl.when(first_ref[i] == 1)
    def _():
        o_ref[...] = jnp.zeros_like(o_ref)

    @pl.when(hi > lo)
    def _():
        x = x_ref[...].reshape(MOE_TILE, D).astype(BF16)
        h = jnp.dot(x, w1b_ref[...], preferred_element_type=F32) + b1_ref[...]
        g = jnp.minimum(h[:, :D_FF], SWIGLU_LIMIT)
        u = jnp.clip(h[:, D_FF:], -SWIGLU_LIMIT, SWIGLU_LIMIT)
        a = g * jax.nn.sigmoid(SWIGLU_ALPHA * g) * (u + 1.0)
        y = jnp.dot(a.astype(BF16), w2b_ref[...], preferred_element_type=F32) + b2_ref[...]
        row = lax.broadcasted_iota(jnp.int32, (MOE_TILE, 1), 0)
        o_ref[...] += jnp.where((row >= lo) & (row < hi), y, 0.0).reshape(MOE_TILE, *ROW_AS_TILE)


def moe_ffn(items, xs, w1, b1, w2, b2, layer):
    assert D == D_FF
    blk, exp, lo, hi, first, newexp, nxt = items
    grid_spec = pltpu.PrefetchScalarGridSpec(
        num_scalar_prefetch=7,
        grid=(N_MOE_ITEMS,),
        in_specs=[
            pl.BlockSpec((MOE_TILE, *ROW_AS_TILE), lambda i, blk, exp, *_: (blk[i], 0, 0)),
            pl.BlockSpec(memory_space=pl.ANY),
            pl.BlockSpec((None, None, 1, 2 * D_FF), lambda i, blk, exp, *_: (layer, exp[i], 0, 0)),
            pl.BlockSpec(memory_space=pl.ANY),
            pl.BlockSpec((None, None, 1, D), lambda i, blk, exp, *_: (layer, exp[i], 0, 0)),
        ],
        out_specs=pl.BlockSpec((MOE_TILE, *ROW_AS_TILE), lambda i, blk, exp, *_: (blk[i], 0, 0)),
        scratch_shapes=[pltpu.VMEM((D, 2 * D_FF), F32), pltpu.VMEM((D_FF, D), F32),
                        pltpu.VMEM((D, 2 * D_FF), BF16), pltpu.VMEM((D_FF, D), BF16),
                        pltpu.SemaphoreType.DMA((2,))],
    )
    return pl.pallas_call(
        functools.partial(_moe_ffn_kernel, layer=layer),
        grid_spec=grid_spec,
        out_shape=jax.ShapeDtypeStruct((N_ASSIGN, *ROW_AS_TILE), F32),
        compiler_params=_cparams(("arbitrary",)),
        name="moe_ffn",
    )(blk, exp, lo, hi, first, newexp, nxt, xs, w1, b1, w2, b2)


def _combine_kernel(dest_ref, dest_next_ref, ys_ref, gates_ref, x_ref, mod_ref, g_ref, b_ref, o_ref, buf_ref, sems,
                    *, n_tiles):
    i = pl.program_id(0)
    slot = i % 2

    def copy(idx_ref, s, r, k):
        d = idx_ref[r * TOP_K + k]
        return pltpu.make_async_copy(ys_ref.at[d], buf_ref.at[s, k, r], sems.at[s])

    def gather(idx_ref, s):
        def issue(r, _):
            for k in range(TOP_K):
                copy(idx_ref, s, r, k).start(priority=k % 2)
            return 0

        lax.fori_loop(0, COMB_TILE, issue, 0, unroll=4)

    @pl.when(i == 0)
    def _():
        gather(dest_ref, 0)

    @pl.when(i + 1 < n_tiles)
    def _():
        gather(dest_next_ref, 1 - slot)

    def drain(r, _):
        for k in range(TOP_K):
            copy(dest_ref, slot, r, k).wait()
        return 0

    lax.fori_loop(0, COMB_TILE, drain, 0, unroll=4)

    gates = gates_ref[...]
    y = gates[:, 0:1] * buf_ref[slot, 0].reshape(COMB_TILE, D)
    for k in range(1, TOP_K):
        y = y + gates[:, k:k + 1] * buf_ref[slot, k].reshape(COMB_TILE, D)
    m = mod_ref[...]
    o_ref[...] = _layer_norm(DN_ALPHA * x_ref[...] + m[5:6] * y, g_ref[...], b_ref[...])


def moe_combine(ys, dest_flat, gates, x1, mod, ln_g, ln_b, *, row_off=0, n_rows=T_ALL):
    full = lambda shp: pl.BlockSpec(shp, lambda i: (0, 0))
    idx = lambda fn: pl.BlockSpec((COMB_TILE * TOP_K,), fn, memory_space=pltpu.SMEM)
    t0 = row_off // COMB_TILE
    n_tiles = n_rows // COMB_TILE
    return pl.pallas_call(
        functools.partial(_combine_kernel, n_tiles=n_tiles),
        grid=(n_tiles,),
        in_specs=[idx(lambda i: (t0 + i,)), idx(lambda i: (t0 + jnp.minimum(i + 1, n_tiles - 1),)),
                  pl.BlockSpec(memory_space=pl.ANY),
                  pl.BlockSpec((COMB_TILE, TOP_K), lambda i: (t0 + i, 0)),
                  pl.BlockSpec((COMB_TILE, D), lambda i: (t0 + i, 0)),
                  pl.BlockSpec((None, 8, D), lambda i: (_mod_index(t0 + i, COMB_TILE), 0, 0)),
                  full((1, D)), full((1, D))],
        out_specs=pl.BlockSpec((COMB_TILE, D), lambda i: (i, 0)),
        out_shape=jax.ShapeDtypeStruct((n_rows, D), F32),
        scratch_shapes=[pltpu.VMEM((2, TOP_K, COMB_TILE, *ROW_AS_TILE), F32), pltpu.SemaphoreType.DMA((2,))],
        compiler_params=_cparams(("arbitrary",)),
        name="moe_combine",
    )(dest_flat, dest_flat, ys, gates, x1, mod, ln_g, ln_b)


def _routing_tables(experts, pos, counts):
    counts = counts[0, :N_EXPERTS].astype(jnp.int32)
    starts = jnp.cumsum(counts) - counts
    eids = jnp.arange(N_EXPERTS, dtype=jnp.int32)
    start_of_pick = jnp.sum(jnp.where(experts[..., None] == eids, starts, 0), axis=-1)
    dest = start_of_pick + pos
    bnd = jnp.sort(jnp.concatenate([jnp.arange(N_MOE_BLOCKS + 1, dtype=jnp.int32) * MOE_TILE, starts[1:]]))
    a, b = bnd[:-1], bnd[1:]
    blk = jnp.minimum(a // MOE_TILE, N_MOE_BLOCKS - 1)
    lo = a - blk * MOE_TILE
    hi = b - blk * MOE_TILE
    exp = jnp.clip(jnp.sum((starts[None, :] <= a[:, None]).astype(jnp.int32), axis=1) - 1, 0, N_EXPERTS - 1)
    one = jnp.ones((1,), jnp.int32)
    first = jnp.concatenate([one, (blk[1:] != blk[:-1]).astype(jnp.int32)])
    newexp = jnp.concatenate([one, (exp[1:] != exp[:-1]).astype(jnp.int32)])
    idx = jnp.arange(N_MOE_ITEMS, dtype=jnp.int32)
    change_at = jnp.where(newexp == 1, idx, N_MOE_ITEMS)
    next_change = jnp.concatenate([lax.cummin(change_at, reverse=True)[1:], jnp.full((1,), N_MOE_ITEMS, jnp.int32)])
    nxt = jnp.where(next_change < N_MOE_ITEMS, exp[jnp.minimum(next_change, N_MOE_ITEMS - 1)], -1)
    items = tuple(v.astype(jnp.int32) for v in (blk, exp, lo, hi, first, newexp, nxt))
    return dest.astype(jnp.int32).reshape(-1), items


def moe_layer(h2, experts, pos, gates, counts, x1, mod, ln_g, ln_b, w1, b1, w2, b2, layer, split_groups):
    dest_flat, items = _routing_tables(experts, pos, counts)
    xs = moe_dispatch(h2, dest_flat)
    ys = moe_ffn(items, xs, w1, b1, w2, b2, layer)
    if not split_groups:
        y = moe_combine(ys, dest_flat, gates, x1, mod, ln_g, ln_b)
        return (y, y)
    return (moe_combine(ys, dest_flat, gates, x1, mod, ln_g, ln_b, row_off=0, n_rows=T_CTX),
            moe_combine(ys, dest_flat, gates, x1, mod, ln_g, ln_b, row_off=T_CTX, n_rows=T_LAT))


def _router_operands(router_w, router_b):
    w = jnp.pad(router_w, ((0, 0), (0, 128 - N_EXPERTS)))
    hi = w.astype(BF16)
    lo = (w - hi.astype(F32)).astype(BF16)
    b = jnp.concatenate([router_b, jnp.full((128 - N_EXPERTS,), -jnp.inf, F32)])[None]
    return hi, lo, b


def kernel(x_prompt, x_sample, state_s5, cache_na_k, cache_na_v, c, c_ctx, w_mod, b_mod, ln1_g, ln1_b, ln2_g, ln2_b, ev_w_in, ev_w_out, hy_conv_w, hy_conv_b, hy_f_w1, hy_f_b1, hy_f_w2, hy_f_b2, hy_f_w3, hy_f_freq, hy_bias, s5_lam_re, s5_lam_im, s5_log_dt, s5_b_re, s5_b_im, s5_c_re, s5_c_im, s5_d, s5_glu_w, s5_glu_b, od_w_in, od_w_out, na_rpb, router_w, router_b, moe_w1, moe_b1, moe_w2, moe_b2):
    x = (x_prompt.reshape(T_CTX, D), x_sample.reshape(T_LAT, D))
    cvec = jnp.pad(jnp.concatenate([c_ctx[None], c], axis=0), ((0, 5), (0, 0)))

    new_state = None
    new_k = new_v = None
    for l in range(DEPTH):
        i = l // 2
        mod = matmul(cvec, w_mod, b_mod[l][None], b_layer=l, tm=8, tn=512, tk=D, precise=True, silu_a=True)
        mod = jnp.pad(mod.reshape(8, 6, D)[:3], ((0, 0), (0, 2), (0, 0)))

        if l % 2 == 0:
            u = modlinear(x, mod, ev_w_in[i].astype(BF16))
            y_hy = []
            for n_seq, seq_len, row_off, tile in ((N_CTX_SEQ, CTX_LEN, 0, CTX_LEN), (N_LAT_SEQ, LAT_LEN, T_CTX, 1024)):
                taps = hyena_filter_taps(seq_len, hy_f_w1[i], hy_f_b1[i], hy_f_w2[i], hy_f_b2[i], hy_f_w3[i],
                                         hy_f_freq[i])
                y_hy.append(hyena_group(u, hy_conv_w[i], hy_conv_b[i][None], taps, hy_bias[i][None],
                                        n_seq=n_seq, seq_len=seq_len, row_off=row_off, tile=tile))

            h0_lat = state_s5[:, i]
            yf_parts, yb_parts, finals = [], [], []
            for r in range(2):
                tables = _s5_tables(s5_lam_re[i, r], s5_lam_im[i, r], s5_log_dt[i, r], s5_b_re[i, r], s5_b_im[i, r],
                                    s5_c_re[i, r], s5_c_im[i, r], reverse=(r == 1))
                h0_l = jnp.concatenate([h0_lat[:, r, :, :, 0].reshape(N_LAT_SEQ, 1, S5_NS),
                                        h0_lat[:, r, :, :, 1].reshape(N_LAT_SEQ, 1, S5_NS)], axis=-1)
                h0 = jnp.concatenate([jnp.zeros((N_CTX_SEQ, 1, 2 * S5_NS), F32), h0_l], axis=0)
                y_dir, fin_all = s5_direction(u, tables, h0, reverse=(r == 1))
                (yf_parts if r == 0 else yb_parts).append(y_dir)
                fin = fin_all[:N_CTX_SEQ].reshape(N_CTX_SEQ, 2, S5_GROUPS, S5_STATE)
                finals.append(jnp.stack([fin[:, 0], fin[:, 1]], axis=-1))
            new_state = jnp.stack(finals, axis=1)[:, None]
            y_s5 = s5_post(u, yf_parts[0], yb_parts[0], s5_d[i].reshape(1, S5_CH), s5_glu_w[i].astype(BF16),
                           s5_glu_b[i][None])
            acts = [tuple(y_hy), (y_s5, y_s5)]
            w_out = ev_w_out[i].astype(BF16)
            weights = [w_out[:HY_CH], w_out[HY_CH:]]
        else:
            qkv = modlinear(x, mod, od_w_in[i].astype(BF16))
            attn_ctx, new_k, new_v = ctx_attention(qkv)
            kcb = cache_na_k[:, i].reshape(N_LAT_SEQ * CTX_LEN, D).astype(BF16)
            vcb = cache_na_v[:, i].reshape(N_LAT_SEQ * CTX_LEN, D).astype(BF16)
            attn_lat = na_attention(qkv, kcb, vcb, _na_bias_table(na_rpb[i]))
            acts = [(attn_ctx, attn_lat)]
            weights = [od_w_out[i].astype(BF16)]

        rw_hi, rw_lo, rb = _router_operands(router_w[l], router_b[l])
        x1, h2, experts, pos, gates, counts = proj_res_ln(acts, weights, x, mod, ln1_g[l][None], ln1_b[l][None],
                                                          rw_hi, rw_lo, rb)
        x = moe_layer(h2, experts, pos, gates, counts, x1, mod, ln2_g[l][None], ln2_b[l][None],
                      moe_w1, moe_b1[:, :, None, :], moe_w2, moe_b2[:, :, None, :], l, split_groups=(l == DEPTH - 1))

    y_ctx, y_lat = x
    return (y_ctx.reshape(N_CTX_SEQ, CTX_LEN, D), y_lat.reshape(N_LAT_SEQ, LAT_LEN, D), new_state, new_k, new_v)
```
